```python
import jax, jax.numpy as jnp
from jax import lax
import numpy as np

D_MODEL = 1024
BATCH = 4
SEQ = 4096
DEPTH = 4
DEC_BATCH = 32
DEC_SEQ = 8
PAST_LEN = 8192
PAGE_SIZE = 128

N_MIXERS = 4
GROUP_W = D_MODEL // N_MIXERS
HEAD_DIM = 64
N_HEADS = GROUP_W // HEAD_DIM
D_MIX = N_MIXERS * GROUP_W
RW_W_RANK = 32
RW_A_RANK = 32
RW_G_RANK = 64
RW_COLS = 3 * GROUP_W + RW_W_RANK + RW_A_RANK + RW_G_RANK
MOBA_COLS = 3 * GROUP_W
CONV_COLS = 3 * GROUP_W
CMLP_COLS = 2 * GROUP_W
N_COLS = RW_COLS + MOBA_COLS + CONV_COLS + CMLP_COLS
MOBA_BLOCK = 256
MOBA_TOPK = 3
Q_BLOCK = 128
CONV_W = 3
CHUNK = 128
PEER_KEYS = 128
PEER_EXPERTS = PEER_KEYS * PEER_KEYS
PEER_HEADS = 8
PEER_TOPK = 16
PEER_QDIM = 256
PEER_HALF = PEER_QDIM // 2
PEER_TOKEN_BLOCK = 256
ALPHA = (2.0 * DEPTH) ** 0.25
BETA = (8.0 * DEPTH) ** -0.25
LN_EPS = 1e-5
GN_EPS = 64e-5

kernel_name = 'hybrid_rwkv7_moba_conv_gmlp_peer_step'

F32 = jnp.float32


def layer_norm(x, g, b, eps=LN_EPS):
    xf = x.astype(F32)
    mu = jnp.mean(xf, -1, keepdims=True)
    var = jnp.mean(jnp.square(xf - mu), -1, keepdims=True)
    return ((xf - mu) * lax.rsqrt(var + eps) * g.astype(F32) + b.astype(F32)).astype(x.dtype)


def rwkv7_mix(p, shift_prev, s0, mu, w0, w2, a0, a2, g2, k_k, k_a, r_k, lnx_g, lnx_b):
    Bn, T, _ = p.shape
    G = GROUP_W
    prev = jnp.concatenate([shift_prev[:, None].astype(p.dtype), p[:, :-1]], axis=1)
    xm = p + (prev - p) * mu
    r, k, v = xm[..., :G], xm[..., G:2 * G], xm[..., 2 * G:3 * G]
    o = 3 * G
    wl = xm[..., o:o + RW_W_RANK]
    o += RW_W_RANK
    al = xm[..., o:o + RW_A_RANK]
    o += RW_A_RANK
    gl = xm[..., o:o + RW_G_RANK]
    w = -jax.nn.softplus(-(w0 + jnp.tanh(wl) @ w2)) - 0.5
    decay = jnp.exp(-jnp.exp(w.astype(F32)))
    a = jax.nn.sigmoid(a0 + al @ a2)
    g = jax.nn.sigmoid(gl) @ g2
    heads = lambda t: t.reshape(Bn, T, N_HEADS, HEAD_DIM).astype(F32)
    kk = heads(k * k_k)
    kk = kk / jnp.maximum(jnp.linalg.norm(kk, axis=-1, keepdims=True), 1e-12)
    k = k * (1 + (a - 1) * k_a)
    rh, kh, vh, wh, ah = heads(r), heads(k), heads(v), heads(decay), heads(a)

    def step(S, inp):
        r_t, w_t, k_t, v_t, kk_t, a_t = inp
        sa = jnp.einsum('bhvk,bhk->bhv', S, -kk_t)
        S = (S * w_t[:, :, None, :] + sa[..., None] * (kk_t * a_t)[:, :, None, :]
             + v_t[..., None] * k_t[:, :, None, :])
        return S, jnp.einsum('bhvk,bhk->bhv', S, r_t)

    xs = tuple(jnp.moveaxis(t, 1, 0) for t in (rh, wh, kh, vh, kk, ah))
    s_fin, ys = lax.scan(step, s0.astype(F32), xs)
    y = jnp.moveaxis(ys, 0, 1)
    ym = jnp.mean(y, -1, keepdims=True)
    yv = jnp.mean(jnp.square(y - ym), -1, keepdims=True)
    yn = ((y - ym) * lax.rsqrt(yv + GN_EPS)).reshape(Bn, T, G) * lnx_g + lnx_b
    bonus = (jnp.sum(rh * kh * r_k, -1, keepdims=True) * vh).reshape(Bn, T, G)
    out = (yn + bonus) * g
    return out.astype(p.dtype), s_fin.astype(s0.dtype), p[:, -1]


def moba_prompt(q, k, v):
    Bn, S = q.shape[:2]
    n_blk = -(-S // MOBA_BLOCK)
    pad = ((0, 0), (0, n_blk * MOBA_BLOCK - S), (0, 0), (0, 0))
    kp, vp = jnp.pad(k, pad), jnp.pad(v, pad)
    kblk = kp.reshape(Bn, n_blk, MOBA_BLOCK, N_HEADS, HEAD_DIM)
    vblk = vp.reshape(Bn, n_blk, MOBA_BLOCK, N_HEADS, HEAD_DIM)
    k_mean = jnp.mean(kblk.astype(F32), axis=2)
    kblk_h = kblk.transpose(0, 3, 1, 2, 4)
    vblk_h = vblk.transpose(0, 3, 1, 2, 4)
    n_sel = min(MOBA_TOPK, n_blk - 1)
    scale = HEAD_DIM ** -0.5
    bi = jnp.arange(Bn)[:, None, None, None]
    hi = jnp.arange(N_HEADS)[None, :, None, None]

    def one(i):
        q0 = i * Q_BLOCK
        qi = lax.dynamic_slice_in_dim(q, q0, Q_BLOCK, axis=1)
        q_pos = q0 + jnp.arange(Q_BLOCK)
        own = q0 // MOBA_BLOCK
        own_start = own * MOBA_BLOCK
        k_own = lax.dynamic_slice_in_dim(kp, own_start, MOBA_BLOCK, axis=1)
        v_own = lax.dynamic_slice_in_dim(vp, own_start, MOBA_BLOCK, axis=1)
        s_own = jnp.einsum('bqhd,bshd->bhqs', qi, k_own).astype(F32) * scale
        own_mask = (own_start + jnp.arange(MOBA_BLOCK))[None, :] <= q_pos[:, None]
        s_own = jnp.where(own_mask, s_own, -jnp.inf)
        if n_sel > 0:
            gate = jnp.einsum('bqhd,bnhd->bhqn', qi.astype(F32), k_mean)
            gate = jnp.where(jnp.arange(n_blk) < own, gate, -jnp.inf)
            _, sel = lax.top_k(gate, n_sel)
            sel_ok = sel < own
            k_sel = kblk_h[bi, hi, sel]
            v_sel = vblk_h[bi, hi, sel]
            s_sel = jnp.einsum('bqhd,bhqjsd->bhqjs', qi, k_sel).astype(F32) * scale
            s_sel = jnp.where(sel_ok[..., None], s_sel, -jnp.inf)
            s_sel = s_sel.reshape(Bn, N_HEADS, Q_BLOCK, n_sel * MOBA_BLOCK)
            pr = jax.nn.softmax(jnp.concatenate([s_sel, s_own], -1), axis=-1)
            p_sel = pr[..., :n_sel * MOBA_BLOCK].reshape(Bn, N_HEADS, Q_BLOCK, n_sel, MOBA_BLOCK)
            p_own = pr[..., n_sel * MOBA_BLOCK:]
            return (jnp.einsum('bhqjs,bhqjsd->bqhd', p_sel.astype(v.dtype), v_sel)
                    + jnp.einsum('bhqs,bshd->bqhd', p_own.astype(v.dtype), v_own))
        pr = jax.nn.softmax(s_own, axis=-1)
        return jnp.einsum('bhqs,bshd->bqhd', pr.astype(v.dtype), v_own)

    out = lax.map(one, jnp.arange(S // Q_BLOCK))
    return out.transpose(1, 0, 2, 3, 4).reshape(Bn, S, N_HEADS * HEAD_DIM)


def moba_sample(q, k, v, cache_k_l, cache_v_l, page_table):
    Bn, T = q.shape[:2]
    kp = cache_k_l[page_table].reshape(Bn, PAST_LEN, N_HEADS, HEAD_DIM)
    vp = cache_v_l[page_table].reshape(Bn, PAST_LEN, N_HEADS, HEAD_DIM)
    n_pf = PAST_LEN // MOBA_BLOCK
    own_start = n_pf * MOBA_BLOCK
    tail = PAST_LEN - own_start
    k_own = jnp.concatenate([kp[:, own_start:], k], axis=1)
    v_own = jnp.concatenate([vp[:, own_start:], v], axis=1)
    scale = HEAD_DIM ** -0.5
    s_own = jnp.einsum('bqhd,bshd->bhqs', q, k_own).astype(F32) * scale
    own_mask = jnp.concatenate([jnp.ones((T, tail), bool), jnp.tril(jnp.ones((T, T), bool))], axis=1)
    s_own = jnp.where(own_mask, s_own, -jnp.inf)
    n_sel = min(MOBA_TOPK, n_pf)
    if n_sel > 0:
        kblk = kp[:, :own_start].reshape(Bn, n_pf, MOBA_BLOCK, N_HEADS, HEAD_DIM)
        vblk = vp[:, :own_start].reshape(Bn, n_pf, MOBA_BLOCK, N_HEADS, HEAD_DIM)
        k_mean = jnp.mean(kblk.astype(F32), axis=2)
        gate = jnp.einsum('bqhd,bnhd->bhqn', q.astype(F32), k_mean)
        _, sel = lax.top_k(gate, n_sel)
        bi = jnp.arange(Bn)[:, None, None, None]
        hi = jnp.arange(N_HEADS)[None, :, None, None]
        k_sel = kblk.transpose(0, 3, 1, 2, 4)[bi, hi, sel]
        v_sel = vblk.transpose(0, 3, 1, 2, 4)[bi, hi, sel]
        s_sel = jnp.einsum('bqhd,bhqjsd->bhqjs', q, k_sel).astype(F32) * scale
        s_sel = s_sel.reshape(Bn, N_HEADS, T, n_sel * MOBA_BLOCK)
        pr = jax.nn.softmax(jnp.concatenate([s_sel, s_own], -1), axis=-1)
        p_sel = pr[..., :n_sel * MOBA_BLOCK].reshape(Bn, N_HEADS, T, n_sel, MOBA_BLOCK)
        p_own = pr[..., n_sel * MOBA_BLOCK:]
        o = (jnp.einsum('bhqjs,bhqjsd->bqhd', p_sel.astype(v.dtype), v_sel)
             + jnp.einsum('bhqs,bshd->bqhd', p_own.astype(v.dtype), v_own))
    else:
        pr = jax.nn.softmax(s_own, axis=-1)
        o = jnp.einsum('bhqs,bshd->bqhd', pr.astype(v.dtype), v_own)
    return o.reshape(Bn, T, N_HEADS * HEAD_DIM)


def short_conv_mix(p, conv_prev, conv_w):
    G = GROUP_W
    T = p.shape[1]
    bg, cg, hv = p[..., :G], p[..., G:2 * G], p[..., 2 * G:]
    padded = jnp.concatenate([conv_prev.astype(p.dtype), cg * hv], axis=1)
    y = sum(conv_w[j] * padded[:, j:j + T] for j in range(CONV_W))
    return bg * y, padded[:, T:]


def chunk_mlp_mix(p, ln_g, ln_b, ws, bs):
    Bn, T, _ = p.shape
    G = GROUP_W
    u = jax.nn.gelu(p[..., :G])
    v = layer_norm(jax.nn.gelu(p[..., G:]), ln_g, ln_b)
    n_ch = -(-T // CHUNK)
    vp = jnp.pad(v, ((0, 0), (0, n_ch * CHUNK - T), (0, 0))).reshape(Bn, n_ch, CHUNK, N_HEADS, HEAD_DIM)
    wm = ws * jnp.tril(jnp.ones((CHUNK, CHUNK), ws.dtype))
    mixed = jnp.einsum('hij,bcjhd->bcihd', wm, vp) + bs.T[None, None, :, :, None]
    mixed = mixed.reshape(Bn, n_ch * CHUNK, G)[:, :T]
    return u * mixed, v


def peer_ffn(h, wq, keys, U, V):
    Bn, T, D = h.shape
    n = Bn * T
    blk = min(PEER_TOKEN_BLOCK, n)
    n_blocks = -(-n // blk)
    flat = jnp.pad(h.reshape(n, D), ((0, n_blocks * blk - n), (0, 0)))

    def one(xb):
        q = (xb @ wq).reshape(blk, PEER_HEADS, 2, PEER_HALF)
        s = jnp.einsum('nhcd,hckd->nhck', q, keys).astype(F32)
        v1, i1 = lax.top_k(s[:, :, 0], PEER_TOPK)
        v2, i2 = lax.top_k(s[:, :, 1], PEER_TOPK)
        cand = (v1[..., :, None] + v2[..., None, :]).reshape(blk, PEER_HEADS, PEER_TOPK * PEER_TOPK)
        cidx = (i1[..., :, None] * PEER_KEYS + i2[..., None, :]).reshape(blk, PEER_HEADS, PEER_TOPK * PEER_TOPK)
        best, pos = lax.top_k(cand, PEER_TOPK)
        eidx = jnp.take_along_axis(cidx, pos, axis=-1)
        gates = jax.nn.softmax(best, axis=-1)
        act = jax.nn.gelu(jnp.einsum('nhkd,nd->nhk', U[eidx], xb).astype(F32))
        return jnp.einsum('nhk,nhkd->nd', (gates * act).astype(xb.dtype), V[eidx])

    out = lax.map(one, flat.reshape(n_blocks, blk, D))
    return out.reshape(n_blocks * blk, D)[:n].reshape(Bn, T, D)


def forward_group(x, c, rw_s0, shift0, conv0, attend, W):
    G = GROUP_W
    Bn, T, _ = x.shape
    x = layer_norm(x, W['ln_in_g'], W['ln_in_b'])
    ks, vs, rws, shs, cvs, cms = [], [], [], [], [], []
    o_at = RW_COLS
    o_cv = o_at + MOBA_COLS
    o_cm = o_cv + CONV_COLS
    for l in range(DEPTH):
        m = jax.nn.silu(c) @ W['w_ada'][l] + W['b_ada'][l]
        sh1, sc1, g1, sh2, sc2, g2 = jnp.split(m[:, None, :], 6, axis=-1)
        pj = (x * (1 + sc1) + sh1) @ W['w_mix'][l]
        y_rw, s_rw, sh_rw = rwkv7_mix(pj[..., :RW_COLS], shift0[l], rw_s0[l], W['rw_mu'][l],
                                      W['rw_w0'][l], W['rw_w2'][l], W['rw_a0'][l], W['rw_a2'][l],
                                      W['rw_g2'][l], W['rw_kk'][l], W['rw_ka'][l], W['rw_rk'][l],
                                      W['rw_lnx_g'][l], W['rw_lnx_b'][l])
        q = pj[..., o_at:o_at + G].reshape(Bn, T, N_HEADS, HEAD_DIM)
        k = pj[..., o_at + G:o_at + 2 * G].reshape(Bn, T, N_HEADS, HEAD_DIM)
        v = pj[..., o_at + 2 * G:o_cv].reshape(Bn, T, N_HEADS, HEAD_DIM)
        y_at = attend(l, q, k, v)
        y_cv, cv_st = short_conv_mix(pj[..., o_cv:o_cm], conv0[l], W['conv_w'][l])
        y_cm, v_rows = chunk_mlp_mix(pj[..., o_cm:], W['cm_ln_g'][l], W['cm_ln_b'][l],
                                     W['cm_ws'][l], W['cm_bs'][l])
        mix = jnp.concatenate([y_rw, y_at, y_cv, y_cm], axis=-1)
        x = layer_norm(ALPHA * x + (1 + g1) * (mix @ W['w_out'][l]), W['ln1_g'][l], W['ln1_b'][l])
        f = peer_ffn(x * (1 + sc2) + sh2, W['peer_wq'][l], W['peer_keys'][l],
                     W['peer_u'][l], W['peer_v'][l])
        x = layer_norm(ALPHA * x + (1 + g2) * f, W['ln2_g'][l], W['ln2_b'][l])
        ks.append(k)
        vs.append(v)
        rws.append(s_rw)
        shs.append(sh_rw)
        cvs.append(cv_st)
        cms.append(v_rows)
    return (x, jnp.stack(ks), jnp.stack(vs), jnp.stack(rws), jnp.stack(shs),
            jnp.stack(cvs), jnp.stack(cms))


def setup_inputs(seed: int = 0) -> dict:
    key = jax.random.key(seed)
    k = jax.random.split(key, 48)

    def nrm(i, shape, s):
        return jax.random.normal(k[i], shape, jnp.float32) * s

    G = GROUP_W
    n_pages = PAST_LEN // PAGE_SIZE
    n_used = DEC_BATCH * n_pages
    n_pool = n_used + max(1, n_used // 4)
    page_table = jax.random.permutation(k[0], n_pool)[:n_used].reshape(DEC_BATCH, n_pages).astype(jnp.int32)
    return {
        'x_prompt': nrm(1, (BATCH, SEQ, D_MODEL), 1.0),
        'x_sample': nrm(2, (DEC_BATCH, DEC_SEQ, D_MODEL), 1.0),
        'cache_k': nrm(3, (DEPTH, n_pool, PAGE_SIZE, N_HEADS, HEAD_DIM), 1.0),
        'cache_v': nrm(4, (DEPTH, n_pool, PAGE_SIZE, N_HEADS, HEAD_DIM), 1.0),
        'state_rwkv': nrm(5, (DEPTH, DEC_BATCH, N_HEADS, HEAD_DIM, HEAD_DIM), 0.3),
        'state_shift': nrm(6, (DEPTH, DEC_BATCH, RW_COLS), 1.0),
        'state_conv': nrm(7, (DEPTH, DEC_BATCH, CONV_W - 1, G), 1.0),
        'page_table': page_table,
        'c_prompt': nrm(8, (BATCH, D_MODEL), 1.0),
        'c_sample': nrm(9, (DEC_BATCH, D_MODEL), 1.0),
        'ln_in_g': 1.0 + nrm(10, (D_MODEL,), 0.02),
        'ln_in_b': nrm(11, (D_MODEL,), 0.02),
        'w_ada': nrm(12, (DEPTH, D_MODEL, 6 * D_MODEL), 0.1 * D_MODEL ** -0.5),
        'b_ada': nrm(13, (DEPTH, 6 * D_MODEL), 0.01),
        'w_mix': nrm(14, (DEPTH, D_MODEL, N_COLS), D_MODEL ** -0.5),
        'rw_mu': jax.random.uniform(k[15], (DEPTH, RW_COLS), jnp.float32),
        'rw_w0': nrm(16, (DEPTH, G), 0.5),
        'rw_w2': nrm(17, (DEPTH, RW_W_RANK, G), 0.1 * RW_W_RANK ** -0.5),
        'rw_a0': nrm(18, (DEPTH, G), 0.1),
        'rw_a2': nrm(19, (DEPTH, RW_A_RANK, G), 0.1 * RW_A_RANK ** -0.5),
        'rw_g2': nrm(20, (DEPTH, RW_G_RANK, G), RW_G_RANK ** -0.5),
        'rw_kk': 0.85 + nrm(21, (DEPTH, G), 0.05),
        'rw_ka': 1.0 + nrm(22, (DEPTH, G), 0.05),
        'rw_rk': nrm(23, (DEPTH, N_HEADS, HEAD_DIM), 0.1),
        'rw_lnx_g': 1.0 + nrm(24, (DEPTH, G), 0.02),
        'rw_lnx_b': nrm(25, (DEPTH, G), 0.02),
        'conv_w': nrm(26, (DEPTH, CONV_W, G), CONV_W ** -0.5),
        'cm_ln_g': 1.0 + nrm(27, (DEPTH, G), 0.02),
        'cm_ln_b': nrm(28, (DEPTH, G), 0.02),
        'cm_ws': nrm(29, (DEPTH, N_HEADS, CHUNK, CHUNK), CHUNK ** -0.5),
        'cm_bs': 1.0 + nrm(30, (DEPTH, N_HEADS, CHUNK), 0.1),
        'w_out': nrm(31, (DEPTH, D_MIX, D_MODEL), BETA * D_MIX ** -0.5),
        'ln1_g': 1.0 + nrm(32, (DEPTH, D_MODEL), 0.02),
        'ln1_b': nrm(33, (DEPTH, D_MODEL), 0.02),
        'ln2_g': 1.0 + nrm(34, (DEPTH, D_MODEL), 0.02),
        'ln2_b': nrm(35, (DEPTH, D_MODEL), 0.02),
        'peer_wq': nrm(36, (DEPTH, D_MODEL, PEER_HEADS * PEER_QDIM), D_MODEL ** -0.5),
        'peer_keys': nrm(37, (DEPTH, PEER_HEADS, 2, PEER_KEYS, PEER_HALF), PEER_HALF ** -0.5),
        'peer_u': nrm(38, (DEPTH, PEER_EXPERTS, D_MODEL), D_MODEL ** -0.5),
        'peer_v': nrm(39, (DEPTH, PEER_EXPERTS, D_MODEL), BETA * PEER_HEADS ** -0.5),
    }


def reference(x_prompt, x_sample, cache_k, cache_v, state_rwkv, state_shift, state_conv, page_table,
              c_prompt, c_sample, ln_in_g, ln_in_b, w_ada, b_ada, w_mix, rw_mu, rw_w0, rw_w2, rw_a0,
              rw_a2, rw_g2, rw_kk, rw_ka, rw_rk, rw_lnx_g, rw_lnx_b, conv_w, cm_ln_g, cm_ln_b, cm_ws,
              cm_bs, w_out, ln1_g, ln1_b, ln2_g, ln2_b, peer_wq, peer_keys, peer_u, peer_v):
    W = {'ln_in_g': ln_in_g, 'ln_in_b': ln_in_b, 'w_ada': w_ada, 'b_ada': b_ada, 'w_mix': w_mix,
         'rw_mu': rw_mu, 'rw_w0': rw_w0, 'rw_w2': rw_w2, 'rw_a0': rw_a0, 'rw_a2': rw_a2,
         'rw_g2': rw_g2, 'rw_kk': rw_kk, 'rw_ka': rw_ka, 'rw_rk': rw_rk, 'rw_lnx_g': rw_lnx_g,
         'rw_lnx_b': rw_lnx_b, 'conv_w': conv_w, 'cm_ln_g': cm_ln_g, 'cm_ln_b': cm_ln_b,
         'cm_ws': cm_ws, 'cm_bs': cm_bs, 'w_out': w_out, 'ln1_g': ln1_g, 'ln1_b': ln1_b,
         'ln2_g': ln2_g, 'ln2_b': ln2_b, 'peer_wq': peer_wq, 'peer_keys': peer_keys,
         'peer_u': peer_u, 'peer_v': peer_v}
    Bp = x_prompt.shape[0]
    dt = x_prompt.dtype
    z_rw = jnp.zeros((DEPTH, Bp, N_HEADS, HEAD_DIM, HEAD_DIM), dt)
    z_sh = jnp.zeros((DEPTH, Bp, RW_COLS), dt)
    z_cv = jnp.zeros((DEPTH, Bp, CONV_W - 1, GROUP_W), dt)
    y_p, k_p, v_p, rw_p, sh_p, cv_p, _ = forward_group(
        x_prompt, c_prompt, z_rw, z_sh, z_cv,
        lambda l, q, k, v: moba_prompt(q, k, v), W)
    y_s, k_s, v_s, rw_s, sh_s, cv_s, cm_s = forward_group(
        x_sample, c_sample, state_rwkv, state_shift, state_conv,
        lambda l, q, k, v: moba_sample(q, k, v, cache_k[l], cache_v[l], page_table), W)
    return (y_p, y_s, k_p, v_p, k_s, v_s, rw_p, rw_s, sh_p, sh_s, cv_p, cv_s, cm_s)
```

```python
import functools
import math

import jax
import jax.numpy as jnp
from jax import lax
from jax.experimental import pallas as pl
from jax.experimental.pallas import tpu as pltpu

F32 = jnp.float32
BF16 = jnp.bfloat16
HI = lax.Precision.HIGHEST

D_MODEL = 1024
N_MIXERS = 4
GROUP_W = D_MODEL // N_MIXERS
HEAD_DIM = 64
N_HEADS = GROUP_W // HEAD_DIM
RW_W_RANK = 32
RW_A_RANK = 32
RW_G_RANK = 64
RW_COLS = 3 * GROUP_W + RW_W_RANK + RW_A_RANK + RW_G_RANK
MOBA_COLS = 3 * GROUP_W
CONV_COLS = 3 * GROUP_W
CMLP_COLS = 2 * GROUP_W
N_COLS = RW_COLS + MOBA_COLS + CONV_COLS + CMLP_COLS
MOBA_BLOCK = 256
MOBA_TOPK = 3
Q_BLOCK = 128
PAGE_SIZE = 128
CONV_W = 3
CHUNK = 128
PEER_KEYS = 128
PEER_EXPERTS = PEER_KEYS * PEER_KEYS
PEER_HEADS = 8
PEER_TOPK = 16
PEER_QDIM = 256
PEER_HALF = PEER_QDIM // 2
LN_EPS = 1e-5
GN_EPS = 64e-5
NEG_INF = float("-inf")

VMEM_LIMIT = 48 * 1024 * 1024
TIME_CHUNK = 128
HALF_CHUNK = TIME_CHUNK // 2


def _params(*sem):
    return pltpu.CompilerParams(dimension_semantics=sem, vmem_limit_bytes=VMEM_LIMIT)


def _full(shape):
    n = len(shape)
    return pl.BlockSpec(shape, lambda *_: (0,) * n)


def _sigmoid(x):
    return 1.0 / (1.0 + jnp.exp(-x))


def _gelu(x):
    return 0.5 * x * (1.0 + jnp.tanh(0.7978845608028654 * (x + 0.044715 * (x * x * x))))


def _ln_rows(x, g, b):
    mu = jnp.mean(x, axis=-1, keepdims=True)
    d = x - mu
    var = jnp.mean(d * d, axis=-1, keepdims=True)
    return d * lax.rsqrt(var + LN_EPS) * g + b


def _dot_hi(a, b):
    return jnp.dot(a, b, precision=HI, preferred_element_type=F32)


def _dot_nt(a, b):
    return lax.dot_general(a, b, (((1,), (1,)), ((), ())), preferred_element_type=F32)


def _ln_kernel(x_ref, g_ref, b_ref, o_ref):
    o_ref[...] = _ln_rows(x_ref[...], g_ref[...], b_ref[...])


def layer_norm_rows(x2d, g, b):
    n, d = x2d.shape
    tm = min(512, n)
    return pl.pallas_call(
        _ln_kernel,
        grid=(n // tm,),
        in_specs=[pl.BlockSpec((tm, d), lambda i: (i, 0)), _full((1, d)), _full((1, d))],
        out_specs=pl.BlockSpec((tm, d), lambda i: (i, 0)),
        out_shape=jax.ShapeDtypeStruct((n, d), F32),
        compiler_params=_params("parallel"),
        name="ln_in",
    )(x2d, g.reshape(1, d), b.reshape(1, d))


def _ada_kernel(c_ref, w_ref, b_ref, o_ref):
    c = c_ref[...]
    s = (c * _sigmoid(c)).astype(BF16)
    o_ref[0] = jnp.dot(s, w_ref[0].astype(BF16), preferred_element_type=F32) + b_ref[0]


def ada_vectors(c_all, w_ada, b_ada):
    depth, d, n6 = w_ada.shape
    m = c_all.shape[0]
    tn = 1536
    return pl.pallas_call(
        _ada_kernel,
        grid=(depth, n6 // tn),
        in_specs=[_full((m, d)),
                  pl.BlockSpec((1, d, tn), lambda l, j: (l, 0, j)),
                  pl.BlockSpec((1, 1, tn), lambda l, j: (l, 0, j))],
        out_specs=pl.BlockSpec((1, m, tn), lambda l, j: (l, 0, j)),
        out_shape=jax.ShapeDtypeStruct((depth, m, n6), F32),
        compiler_params=_params("parallel", "parallel"),
        name="ada",
    )(c_all, w_ada, b_ada.reshape(depth, 1, n6))


_MIX_WIDTHS = (RW_COLS, GROUP_W, GROUP_W, GROUP_W, CONV_COLS, CMLP_COLS)


def _mix_kernel(x_ref, sc_ref, sh_ref, w_ref, *out_refs):
    h = (x_ref[0] * (1.0 + sc_ref[0]) + sh_ref[0]).astype(BF16)
    off = 0
    for ref, width in zip(out_refs, _MIX_WIDTHS):
        ref[0] = jnp.dot(h, w_ref[:, off:off + width], preferred_element_type=F32)
        off += width


def mix_projection(x, sc, sh, w_bf16):
    bn, t, d = x.shape
    tm = min(256, t)
    rows = sc.shape[1]
    per_row = rows == t and t > 1
    mod_block = (1, tm, d) if per_row else (1, 1, d)
    mod_map = (lambda b, i: (b, i, 0)) if per_row else (lambda b, i: (b, 0, 0))
    out_shape = [jax.ShapeDtypeStruct((bn, t, w), F32) for w in _MIX_WIDTHS]
    out_specs = [pl.BlockSpec((1, tm, w), lambda b, i: (b, i, 0)) for w in _MIX_WIDTHS]
    return pl.pallas_call(
        _mix_kernel,
        grid=(bn, t // tm),
        in_specs=[pl.BlockSpec((1, tm, d), lambda b, i: (b, i, 0)),
                  pl.BlockSpec(mod_block, mod_map),
                  pl.BlockSpec(mod_block, mod_map),
                  _full(w_bf16.shape)],
        out_specs=out_specs,
        out_shape=out_shape,
        compiler_params=_params("parallel", "parallel"),
        name="mix_proj",
    )(x, sc, sh, w_bf16)


def _softplus(x):
    return jnp.maximum(x, 0.0) + jnp.log(1.0 + jnp.exp(-jnp.abs(x)))


def _rwkv_kernel(n_steps, bk,
                 p_ref, shift_ref, s0_ref, bd_ref, mu_ref, w0_ref, w2_ref, a0_ref, a2_ref, g2_ref,
                 kkp_ref, kap_ref, rk_ref, lng_ref, lnb_ref,
                 y_ref, s_ref,
                 carry_ref, dec_ref, kk_ref, bb_ref, k2_ref, wr_ref, c1_ref, c2_ref, g_ref, bonus_ref,
                 vt_ref, acc_ref):
    assert n_steps % 8 == 0
    ci = pl.program_id(1)
    G = GROUP_W
    bd = bd_ref[...]

    @pl.when(ci == 0)
    def _():
        s_ref[...] = s0_ref[...]
        carry_ref[...] = shift_ref[...]

    acc_ref[...] = jnp.zeros(acc_ref.shape, F32)
    row_id = lax.broadcasted_iota(jnp.int32, (TIME_CHUNK, 1), 0)

    for b in range(bk):
        p = p_ref[b]
        prev = jnp.where(row_id == 0, carry_ref[b], pltpu.roll(p, 1, axis=0))
        carry_ref[b] = p[TIME_CHUNK - 1:TIME_CHUNK, :]
        xm = p + (prev - p) * mu_ref[...]
        r, k, v = xm[:, :G], xm[:, G:2 * G], xm[:, 2 * G:3 * G]
        o = 3 * G
        wl = xm[:, o:o + RW_W_RANK]
        al = xm[:, o + RW_W_RANK:o + RW_W_RANK + RW_A_RANK]
        gl = xm[:, o + RW_W_RANK + RW_A_RANK:]
        w = -_softplus(-(w0_ref[...] + _dot_hi(jnp.tanh(wl), w2_ref[...]))) - 0.5
        decay = jnp.exp(-jnp.exp(w))
        a = _sigmoid(a0_ref[...] + _dot_hi(al, a2_ref[...]))
        g = _dot_hi(_sigmoid(gl), g2_ref[...])
        kk = k * kkp_ref[...]
        kk = kk / jnp.maximum(jnp.sqrt(_dot_hi(kk * kk, bd)), 1e-12)
        k2 = k * (1.0 + (a - 1.0) * kap_ref[...])
        bb = kk * a
        dec_ref[b] = decay
        kk_ref[b] = kk
        bb_ref[b] = bb
        k2_ref[b] = k2
        wr_ref[b] = decay * r
        c1_ref[b] = _dot_hi(bb * r, bd)
        c2_ref[b] = _dot_hi(k2 * r, bd)
        g_ref[b] = g
        bonus_ref[b] = _dot_hi(r * k2 * rk_ref[...], bd) * v
        for hp in range(2):
            vt_ref[b, hp] = v[:, hp * 128:(hp + 1) * 128].T

    lane = lax.broadcasted_iota(jnp.int32, (1, 128), 1)
    lo_mask = lane < HEAD_DIM
    lane_mod = jnp.bitwise_and(lane, HEAD_DIM - 1)

    def seg_sum(z):
        lo = jnp.sum(jnp.where(lo_mask, z, 0.0), axis=-1, keepdims=True)
        hi = jnp.sum(jnp.where(lo_mask, 0.0, z), axis=-1, keepdims=True)
        return jnp.where(lo_mask, lo, hi)

    for j in range(2):
        n_j = max(0, min(HALF_CHUNK, n_steps - HALF_CHUNK * j))
        if n_j == 0:
            continue

        def body(t8, carry, j=j):
            base = pl.multiple_of(t8 * 8 + HALF_CHUNK * j, 8)
            for b in range(bk):
                for hp in range(2):
                    lanes = slice(hp * 128, (hp + 1) * 128)
                    rows = [ref[b, pl.ds(base, 8), lanes]
                            for ref in (kk_ref, wr_ref, dec_ref, bb_ref, k2_ref, c1_ref, c2_ref)]
                    S = s_ref[b, hp]
                    acc = acc_ref[b, hp, j]
                    vt0 = vt_ref[b, hp, 0:HEAD_DIM, :]
                    vt1 = vt_ref[b, hp, HEAD_DIM:, :]
                    for i in range(8):
                        kk_r, wr_r, dec_r, bb_r, k2_r, c1_r, c2_r = [a[i:i + 1, :] for a in rows]
                        t_lane = lane == base + i
                        vc0 = jnp.sum(jnp.where(t_lane, vt0, 0.0), axis=-1, keepdims=True)
                        vc1 = jnp.sum(jnp.where(t_lane, vt1, 0.0), axis=-1, keepdims=True)
                        vc = jnp.where(lo_mask, vc0, vc1)
                        sa = -seg_sum(S * kk_r)
                        y = seg_sum(S * wr_r) + sa * c1_r + vc * c2_r
                        S = S * dec_r + sa * bb_r + vc * k2_r
                        acc = jnp.where(lane_mod == t8 * 8 + i, y, acc)
                    s_ref[b, hp] = S
                    acc_ref[b, hp, j] = acc
            return carry

        lax.fori_loop(0, n_j // 8, body, 0)

    for b in range(bk):
        halves = []
        for j in range(2):
            tr = jnp.concatenate([acc_ref[b, 0, j], acc_ref[b, 1, j]], axis=0).T
            t0, t1 = tr[:HEAD_DIM], tr[HEAD_DIM:]
            halves.append(jnp.concatenate(
                [t0[:, :HEAD_DIM], t1[:, :HEAD_DIM], t0[:, HEAD_DIM:], t1[:, HEAD_DIM:]], axis=1))
        y = jnp.concatenate(halves, axis=0)
        mean = _dot_hi(y, bd) * (1.0 / HEAD_DIM)
        d = y - mean
        var = _dot_hi(d * d, bd) * (1.0 / HEAD_DIM)
        yn = d * lax.rsqrt(var + GN_EPS) * lng_ref[...] + lnb_ref[...]
        y_ref[b] = (yn + bonus_ref[b]) * g_ref[b]


def rwkv_mixer(p, shift_prev, s0, n_steps, bk, wl):
    bn, tp, _ = p.shape
    G = GROUP_W
    s0p = s0.reshape(bn, 2, 2, HEAD_DIM, HEAD_DIM).transpose(0, 1, 3, 2, 4).reshape(bn, 2, HEAD_DIM, 128)
    head_of = jnp.arange(G) // HEAD_DIM
    bd = (head_of[:, None] == head_of[None, :]).astype(F32)
    row = lambda a: a.reshape(1, -1)
    consts = [bd, row(wl['rw_mu']), row(wl['rw_w0']), wl['rw_w2'], row(wl['rw_a0']), wl['rw_a2'], wl['rw_g2'],
              row(wl['rw_kk']), row(wl['rw_ka']), row(wl['rw_rk']), row(wl['rw_lnx_g']), row(wl['rw_lnx_b'])]
    rows_scratch = [pltpu.VMEM((bk, TIME_CHUNK, G), F32) for _ in range(9)]
    y, s_fin = pl.pallas_call(
        functools.partial(_rwkv_kernel, n_steps, bk),
        grid=(bn // bk, tp // TIME_CHUNK),
        in_specs=[pl.BlockSpec((bk, TIME_CHUNK, RW_COLS), lambda b, c: (b, c, 0)),
                  pl.BlockSpec((bk, 1, RW_COLS), lambda b, c: (b, 0, 0)),
                  pl.BlockSpec((bk, 2, HEAD_DIM, 128), lambda b, c: (b, 0, 0, 0))]
                 + [_full(a.shape) for a in consts],
        out_specs=[pl.BlockSpec((bk, TIME_CHUNK, G), lambda b, c: (b, c, 0)),
                   pl.BlockSpec((bk, 2, HEAD_DIM, 128), lambda b, c: (b, 0, 0, 0))],
        out_shape=[jax.ShapeDtypeStruct((bn, tp, G), F32),
                   jax.ShapeDtypeStruct((bn, 2, HEAD_DIM, 128), F32)],
        scratch_shapes=[pltpu.VMEM((bk, 1, RW_COLS), F32)] + rows_scratch
                       + [pltpu.VMEM((bk, 2, 128, 128), F32),
                          pltpu.VMEM((bk, 2, 2, HEAD_DIM, 128), F32)],
        compiler_params=_params("parallel", "arbitrary"),
        name="rwkv7",
    )(p, shift_prev.reshape(bn, 1, RW_COLS), s0p, *consts)
    s_fin = s_fin.reshape(bn, 2, HEAD_DIM, 2, HEAD_DIM).transpose(0, 1, 3, 2, 4).reshape(
        bn, N_HEADS, HEAD_DIM, HEAD_DIM)
    return y, s_fin


def _local_kernel(t_valid, cv_ref, cm_ref, prev_ref, cw_ref, lg_ref, lb_ref, ws_ref, bias_ref,
                  ycv_ref, ycm_ref, vrow_ref, st_ref, carry_ref):
    ci = pl.program_id(1)
    G = GROUP_W
    TC = TIME_CHUNK

    @pl.when(ci == 0)
    def _():
        carry_ref[...] = prev_ref[0]

    pc = cv_ref[0]
    bg, cg, hv = pc[:, :G], pc[:, G:2 * G], pc[:, 2 * G:]
    u = cg * hv
    row_id = lax.broadcasted_iota(jnp.int32, (TC, 1), 0)
    prev0, prev1 = carry_ref[0:1, :], carry_ref[1:2, :]
    u1 = jnp.where(row_id == 0, prev1, pltpu.roll(u, 1, axis=0))
    u2 = jnp.where(row_id == 0, prev0, jnp.where(row_id == 1, prev1, pltpu.roll(u, 2, axis=0)))
    ycv_ref[0] = bg * (cw_ref[0:1, :] * u2 + cw_ref[1:2, :] * u1 + cw_ref[2:3, :] * u)
    carry_ref[...] = u[TC - 2:, :]
    last = (t_valid - 1) // TC
    r = t_valid - last * TC

    @pl.when(ci == last)
    def _():
        st_ref[0] = u[r - 2:r, :]

    pm = cm_ref[0]
    uu = _gelu(pm[:, :G])
    vv = _ln_rows(_gelu(pm[:, G:]), lg_ref[...], lb_ref[...])
    vrow_ref[0] = vv
    head_of = lax.broadcasted_iota(jnp.int32, (1, G), 1) // HEAD_DIM
    causal = lax.broadcasted_iota(jnp.int32, (TC, TC), 0) >= lax.broadcasted_iota(jnp.int32, (TC, TC), 1)
    mixed = bias_ref[...]
    for h in range(N_HEADS):
        wm = jnp.where(causal, ws_ref[h], 0.0).astype(BF16)
        mixed = mixed + jnp.dot(wm, jnp.where(head_of == h, vv, 0.0).astype(BF16), preferred_element_type=F32)
    ycm_ref[0] = uu * mixed


def local_mixers(p_cv, p_cm, conv_prev, t_valid, wl):
    bn, tp, _ = p_cv.shape
    G = GROUP_W
    bias = jnp.repeat(wl['cm_bs'].T, HEAD_DIM, axis=1)
    row = lambda a: a.reshape(1, -1)
    tok = lambda w: pl.BlockSpec((1, TIME_CHUNK, w), lambda b, c: (b, c, 0))
    return pl.pallas_call(
        functools.partial(_local_kernel, t_valid),
        grid=(bn, tp // TIME_CHUNK),
        in_specs=[tok(CONV_COLS), tok(CMLP_COLS),
                  pl.BlockSpec((1, CONV_W - 1, G), lambda b, c: (b, 0, 0)),
                  _full((CONV_W, G)), _full((1, G)), _full((1, G)),
                  _full((N_HEADS, CHUNK, CHUNK)), _full((CHUNK, G))],
        out_specs=[tok(G), tok(G), tok(G), pl.BlockSpec((1, CONV_W - 1, G), lambda b, c: (b, 0, 0))],
        out_shape=[jax.ShapeDtypeStruct((bn, tp, G), F32)] * 3
                  + [jax.ShapeDtypeStruct((bn, CONV_W - 1, G), F32)],
        scratch_shapes=[pltpu.VMEM((CONV_W - 1, G), F32)],
        compiler_params=_params("parallel", "arbitrary"),
        name="conv_gmlp",
    )(p_cv, p_cm, conv_prev, wl['conv_w'], row(wl['cm_ln_g']), row(wl['cm_ln_b']), wl['cm_ws'], bias)


def _rank_rows(g, n_rows):
    rid = lax.broadcasted_iota(jnp.int32, g.shape, 0)
    rank = jnp.zeros(g.shape, jnp.int32)
    for j in range(n_rows):
        gj = g[j:j + 1, :]
        beats = jnp.where(gj > g, 1, jnp.where(jnp.logical_and(gj == g, rid > j), 1, 0))
        rank = rank + beats
    return rank


def _moba_prompt_kernel(n_blk, n_sel, qT_ref, k_ref, vT_ref, o_ref, kb_ref, vb_ref, kmean_ref, sel_ref):
    qi = pl.program_id(2)
    scale = HEAD_DIM ** -0.5

    @pl.when(qi == 0)
    def _():
        k = k_ref[0, 0]
        kb_ref[...] = k.astype(BF16)
        vb_ref[...] = vT_ref[0, 0].astype(BF16)
        kmean_ref[...] = jnp.mean(k.reshape(n_blk, MOBA_BLOCK, HEAD_DIM), axis=1)

    qT = qT_ref[0, 0]
    qb = qT.astype(BF16)
    own = qi // (MOBA_BLOCK // Q_BLOCK)
    own_start = pl.multiple_of(own * MOBA_BLOCK, MOBA_BLOCK)
    gate = _dot_hi(kmean_ref[...], qT)
    blk = lax.broadcasted_iota(jnp.int32, gate.shape, 0)
    valid = blk < own
    gate = jnp.where(valid, gate, NEG_INF)
    rank = _rank_rows(gate, n_blk)
    sel = jnp.where(jnp.logical_and(valid, rank < n_sel), 1.0, 0.0)
    for j in range(n_blk):
        sel_ref[j] = jnp.broadcast_to(sel[j:j + 1, :], (8, Q_BLOCK))

    def scores(start):
        return jnp.dot(kb_ref[pl.ds(start, MOBA_BLOCK), :], qb, preferred_element_type=F32) * scale

    s = scores(own_start)
    kpos = own_start + lax.broadcasted_iota(jnp.int32, s.shape, 0)
    qpos = qi * Q_BLOCK + lax.broadcasted_iota(jnp.int32, s.shape, 1)
    s = jnp.where(kpos <= qpos, s, NEG_INF)
    m = jnp.max(s, axis=0, keepdims=True)
    p = jnp.exp(s - m)
    l = jnp.sum(p, axis=0, keepdims=True)
    acc = jnp.dot(vb_ref[:, pl.ds(own_start, MOBA_BLOCK)], p.astype(BF16), preferred_element_type=F32)

    def body(j, carry):
        m, l, acc = carry
        start = pl.multiple_of(j * MOBA_BLOCK, MOBA_BLOCK)
        s = jnp.where(sel_ref[j][0:1, :] > 0.0, scores(start), NEG_INF)
        m_new = jnp.maximum(m, jnp.max(s, axis=0, keepdims=True))
        alpha = jnp.exp(m - m_new)
        p = jnp.exp(s - m_new)
        l = l * alpha + jnp.sum(p, axis=0, keepdims=True)
        acc = acc * alpha + jnp.dot(vb_ref[:, pl.ds(start, MOBA_BLOCK)], p.astype(BF16),
                                    preferred_element_type=F32)
        return m_new, l, acc

    m, l, acc = lax.fori_loop(0, own, body, (m, l, acc))
    o_ref[0, 0] = acc / l


def moba_prompt(q, k, v):
    bn, t, _ = q.shape
    n_blk = t // MOBA_BLOCK
    n_sel = min(MOBA_TOPK, n_blk - 1)
    heads = lambda a: a.reshape(bn, t, N_HEADS, HEAD_DIM)
    qT = heads(q).transpose(0, 2, 3, 1)
    kh = heads(k).transpose(0, 2, 1, 3)
    vT = heads(v).transpose(0, 2, 3, 1)
    oT = pl.pallas_call(
        functools.partial(_moba_prompt_kernel, n_blk, n_sel),
        grid=(bn, N_HEADS, t // Q_BLOCK),
        in_specs=[pl.BlockSpec((1, 1, HEAD_DIM, Q_BLOCK), lambda b, h, i: (b, h, 0, i)),
                  pl.BlockSpec((1, 1, t, HEAD_DIM), lambda b, h, i: (b, h, 0, 0)),
                  pl.BlockSpec((1, 1, HEAD_DIM, t), lambda b, h, i: (b, h, 0, 0))],
        out_specs=pl.BlockSpec((1, 1, HEAD_DIM, Q_BLOCK), lambda b, h, i: (b, h, 0, i)),
        out_shape=jax.ShapeDtypeStruct((bn, N_HEADS, HEAD_DIM, t), F32),
        scratch_shapes=[pltpu.VMEM((t, HEAD_DIM), BF16), pltpu.VMEM((HEAD_DIM, t), BF16),
                        pltpu.VMEM((n_blk, HEAD_DIM), F32), pltpu.VMEM((n_blk, 8, Q_BLOCK), F32)],
        compiler_params=_params("parallel", "parallel", "arbitrary"),
        name="moba_prompt",
    )(qT, kh, vT)
    return oT.transpose(0, 3, 1, 2).reshape(bn, t, GROUP_W)


def _page_sum_kernel(pt_ref, k_ref, o_ref):
    o_ref[0, 0] = jnp.sum(k_ref[0, 0], axis=0, keepdims=True)


def _moba_sample_kernel(n_pf, n_sel, t_new, pt_ref, q_ref, kn_ref, vn_ref, psum_ref, k_ref, v_ref, o_ref,
                        qbd_ref, sel_ref, m_ref, l_ref, acc_ref):
    pg = pl.program_id(1)
    n_pages = pl.num_programs(1)
    G = GROUP_W
    R = N_HEADS * t_new
    scale = HEAD_DIM ** -0.5
    pages_per_blk = MOBA_BLOCK // PAGE_SIZE

    @pl.when(pg == 0)
    def _():
        q = q_ref[0]
        row_head = lax.broadcasted_iota(jnp.int32, (R, G), 0) // t_new
        lane_head = lax.broadcasted_iota(jnp.int32, (R, G), 1) // HEAD_DIM
        qbd = jnp.where(row_head == lane_head, jnp.concatenate([q] * N_HEADS, axis=0), 0.0)
        qbd_ref[...] = qbd
        n_pg = n_pf * pages_per_blk
        pair = (lax.broadcasted_iota(jnp.int32, (n_pf, n_pg), 1) // pages_per_blk
                == lax.broadcasted_iota(jnp.int32, (n_pf, n_pg), 0))
        kmean = _dot_hi(jnp.where(pair, 1.0, 0.0), psum_ref[0]) * (1.0 / MOBA_BLOCK)
        gate_t = lax.dot_general(kmean, qbd, (((1,), (1,)), ((), ())), precision=HI,
                                 preferred_element_type=F32)
        rank = _rank_rows(gate_t, n_pf)
        sel_ref[...] = jnp.where(rank < n_sel, 1.0, 0.0)
        s = _dot_nt(qbd.astype(BF16), kn_ref[0].astype(BF16)) * scale
        tq = lax.broadcasted_iota(jnp.int32, s.shape, 0) % t_new
        tk = lax.broadcasted_iota(jnp.int32, s.shape, 1)
        s = jnp.where(tk <= tq, s, NEG_INF)
        m = jnp.max(s, axis=-1, keepdims=True)
        p = jnp.exp(s - m)
        m_ref[...] = m
        l_ref[...] = jnp.sum(p, axis=-1, keepdims=True)
        acc_ref[...] = jnp.dot(p.astype(BF16), vn_ref[0].astype(BF16), preferred_element_type=F32)

    blk = pg // pages_per_blk
    sel_rows = sel_ref[...]
    rid = lax.broadcasted_iota(jnp.int32, sel_rows.shape, 0)
    sel_row = jnp.sum(jnp.where(rid == blk, sel_rows, 0.0), axis=0, keepdims=True)
    eye = lax.broadcasted_iota(jnp.int32, (R, R), 0) == lax.broadcasted_iota(jnp.int32, (R, R), 1)
    sel_col = jnp.sum(jnp.where(eye, sel_row, 0.0), axis=-1, keepdims=True)
    s = _dot_nt(qbd_ref[...].astype(BF16), k_ref[0, 0].astype(BF16)) * scale
    s = jnp.where(sel_col > 0.0, s, NEG_INF)
    m_old = m_ref[...]
    m_new = jnp.maximum(m_old, jnp.max(s, axis=-1, keepdims=True))
    alpha = jnp.exp(m_old - m_new)
    p = jnp.exp(s - m_new)
    m_ref[...] = m_new
    l_ref[...] = l_ref[...] * alpha + jnp.sum(p, axis=-1, keepdims=True)
    acc_ref[...] = acc_ref[...] * alpha + jnp.dot(p.astype(BF16), v_ref[0, 0].astype(BF16),
                                                  preferred_element_type=F32)

    @pl.when(pg == n_pages - 1)
    def _():
        o = acc_ref[...] / l_ref[...]
        row_head = lax.broadcasted_iota(jnp.int32, (R, G), 0) // t_new
        lane_head = lax.broadcasted_iota(jnp.int32, (R, G), 1) // HEAD_DIM
        o = jnp.where(row_head == lane_head, o, 0.0)
        out = o[0:t_new]
        for h in range(1, N_HEADS):
            out = out + o[h * t_new:(h + 1) * t_new]
        o_ref[0] = out


def moba_sample(layer, q, k, v, cache_k, cache_v, page_table):
    db, t_new, G = q.shape
    n_pages = page_table.shape[1]
    past = n_pages * PAGE_SIZE
    assert past % MOBA_BLOCK == 0, "decode kernel expects the past to end on a MoBA block boundary"
    n_pf = past // MOBA_BLOCK
    n_sel = min(MOBA_TOPK, n_pf)
    page_spec = pl.BlockSpec((1, 1, PAGE_SIZE, G), lambda b, p, pt: (layer, pt[b, p], 0, 0))
    psum = pl.pallas_call(
        _page_sum_kernel,
        grid_spec=pltpu.PrefetchScalarGridSpec(
            num_scalar_prefetch=1, grid=(db, n_pages),
            in_specs=[page_spec],
            out_specs=pl.BlockSpec((1, 1, 1, G), lambda b, p, pt: (b, p, 0, 0))),
        out_shape=jax.ShapeDtypeStruct((db, n_pages, 1, G), F32),
        compiler_params=_params("parallel", "arbitrary"),
        name="moba_page_sums",
    )(page_table, cache_k)
    R = N_HEADS * t_new
    tok = pl.BlockSpec((1, t_new, G), lambda b, p, pt: (b, 0, 0))
    return pl.pallas_call(
        functools.partial(_moba_sample_kernel, n_pf, n_sel, t_new),
        grid_spec=pltpu.PrefetchScalarGridSpec(
            num_scalar_prefetch=1, grid=(db, n_pages),
            in_specs=[tok, tok, tok,
                      pl.BlockSpec((1, n_pages, G), lambda b, p, pt: (b, 0, 0)),
                      page_spec, page_spec],
            out_specs=tok,
            scratch_shapes=[pltpu.VMEM((R, G), F32), pltpu.VMEM((n_pf, R), F32),
                            pltpu.VMEM((R, 1), F32), pltpu.VMEM((R, 1), F32), pltpu.VMEM((R, G), F32)]),
        out_shape=jax.ShapeDtypeStruct((db, t_new, G), F32),
        compiler_params=_params("parallel", "arbitrary"),
        name="moba_decode",
    )(page_table, q, k, v, psum.reshape(db, n_pages, G), cache_k, cache_v)


def _out_kernel(alpha, yrw_ref, yat_ref, ycv_ref, ycm_ref, x_ref, g1_ref, w_ref, lg_ref, lb_ref, o_ref):
    G = GROUP_W
    acc = None
    for i, ref in enumerate((yrw_ref, yat_ref, ycv_ref, ycm_ref)):
        part = jnp.dot(ref[0].astype(BF16), w_ref[i * G:(i + 1) * G, :], preferred_element_type=F32)
        acc = part if acc is None else acc + part
    z = alpha * x_ref[0] + (1.0 + g1_ref[0]) * acc
    o_ref[0] = _ln_rows(z, lg_ref[...], lb_ref[...])


def out_projection(ys, x, g1, w_bf16, ln_g, ln_b, alpha):
    bn, t, d = x.shape
    tm = min(256, t)
    rows = g1.shape[1]
    per_row = rows == t and t > 1
    mod_block = (1, tm, d) if per_row else (1, 1, d)
    mod_map = (lambda b, i: (b, i, 0)) if per_row else (lambda b, i: (b, 0, 0))
    tok = lambda w: pl.BlockSpec((1, tm, w), lambda b, i: (b, i, 0))
    return pl.pallas_call(
        functools.partial(_out_kernel, alpha),
        grid=(bn, t // tm),
        in_specs=[tok(GROUP_W)] * 4 + [tok(d), pl.BlockSpec(mod_block, mod_map),
                                       _full(w_bf16.shape), _full((1, d)), _full((1, d))],
        out_specs=tok(d),
        out_shape=jax.ShapeDtypeStruct((bn, t, d), F32),
        compiler_params=_params("parallel", "parallel"),
        name="out_proj_ln",
    )(*ys, x, g1, w_bf16, ln_g.reshape(1, d), ln_b.reshape(1, d))


_N_TOP = PEER_TOPK + 1


def _top_values(x, n, rows_out):
    rid = lax.broadcasted_iota(jnp.int32, (rows_out, x.shape[1]), 0)
    vals = jnp.full((rows_out, x.shape[1]), NEG_INF, F32)
    for i in range(n):
        mx = jnp.max(x, axis=0, keepdims=True)
        vals = jnp.where(rid == i, mx, vals)
        x = jnp.where(x == mx, NEG_INF, x)
    return vals


def _peer_a_kernel(x_ref, sc_ref, sh_ref, wqT_ref, keys_ref, h_ref, s1_ref, s2_ref, thr_ref):
    h = (x_ref[0] * (1.0 + sc_ref[0]) + sh_ref[0]).astype(BF16)
    h_ref[0] = h
    qT = _dot_nt(wqT_ref[...], h).astype(BF16)
    tm = h.shape[0]
    rid8 = lax.broadcasted_iota(jnp.int32, (PEER_HEADS, tm), 0)
    thr_all = jnp.zeros((PEER_HEADS, tm), F32)
    for hh in range(PEER_HEADS):
        r0 = hh * PEER_QDIM
        s1 = jnp.dot(keys_ref[2 * hh], qT[r0:r0 + PEER_HALF], preferred_element_type=F32)
        s2 = jnp.dot(keys_ref[2 * hh + 1], qT[r0 + PEER_HALF:r0 + PEER_QDIM], preferred_element_type=F32)
        v1 = _top_values(s1, _N_TOP, 24)
        v2 = _top_values(s2, _N_TOP, 24)
        cands = [v1[0:1] + v2] + [v1[a:a + 1] + v2[0:8] for a in range(1, _N_TOP)]
        best = _top_values(jnp.concatenate(cands, axis=0), _N_TOP, 24)
        m = best[0:1]
        z = jnp.sum(jnp.exp(best[0:PEER_TOPK] - m), axis=0, keepdims=True)
        off = m + jnp.log(z)
        thr = 0.5 * (best[PEER_TOPK - 1:PEER_TOPK] + best[PEER_TOPK:PEER_TOPK + 1]) - off
        s1_ref[hh * PEER_KEYS:(hh + 1) * PEER_KEYS, :] = s1 - off
        s2_ref[hh * PEER_KEYS:(hh + 1) * PEER_KEYS, :] = s2
        thr_all = jnp.where(rid8 == hh, thr, thr_all)
    thr_ref[...] = thr_all


def peer_stage_a(x, sc, sh, wqT_bf16, keys_bf16):
    bn, t, d = x.shape
    n = bn * t
    tm = min(256, t)
    rows = sc.shape[1]
    per_row = rows == t and t > 1
    mod_block = (1, tm, d) if per_row else (1, 1, d)
    mod_map = (lambda b, i: (b, i, 0)) if per_row else (lambda b, i: (b, 0, 0))
    tpb = t // tm
    col = lambda r: pl.BlockSpec((r, tm), lambda b, i: (0, b * tpb + i))
    nk = PEER_HEADS * PEER_KEYS
    return pl.pallas_call(
        _peer_a_kernel,
        grid=(bn, tpb),
        in_specs=[pl.BlockSpec((1, tm, d), lambda b, i: (b, i, 0)),
                  pl.BlockSpec(mod_block, mod_map), pl.BlockSpec(mod_block, mod_map),
                  _full(wqT_bf16.shape), _full(keys_bf16.shape)],
        out_specs=[pl.BlockSpec((1, tm, d), lambda b, i: (b, i, 0)), col(nk), col(nk), col(PEER_HEADS)],
        out_shape=[jax.ShapeDtypeStruct((bn, t, d), BF16),
                   jax.ShapeDtypeStruct((nk, n), F32), jax.ShapeDtypeStruct((nk, n), F32),
                   jax.ShapeDtypeStruct((PEER_HEADS, n), F32)],
        compiler_params=_params("parallel", "parallel"),
        name="peer_scores",
    )(x, sc, sh, wqT_bf16, keys_bf16)


def _peer_b_kernel(alpha, te, h_ref, u_ref, vT_ref, s1_ref, s2_ref, thr_ref, x_ref, g2_ref, lg_ref, lb_ref,
                   o_ref, acc_ref, w_ref):
    e = pl.program_id(2)
    n_e = pl.num_programs(2)

    @pl.when(e == 0)
    def _():
        acc_ref[...] = jnp.zeros(acc_ref.shape, F32)

    act = _gelu(_dot_nt(u_ref[...], h_ref[0]))
    groups = te // PEER_KEYS
    assert groups == 8
    s1_rows = [s1_ref[pl.ds(pl.multiple_of(hh * PEER_KEYS + e * groups, 8), groups), :]
               for hh in range(PEER_HEADS)]
    for gi in range(groups):
        gate = None
        for hh in range(PEER_HEADS):
            t = s1_rows[hh][gi:gi + 1, :] + s2_ref[hh * PEER_KEYS:(hh + 1) * PEER_KEYS, :]
            part = jnp.where(t >= thr_ref[hh:hh + 1, :], jnp.exp(t), 0.0)
            gate = part if gate is None else gate + part
        w_ref[gi * PEER_KEYS:(gi + 1) * PEER_KEYS, :] = (
            gate * act[gi * PEER_KEYS:(gi + 1) * PEER_KEYS, :]).astype(BF16)
    acc_ref[...] += jnp.dot(vT_ref[...], w_ref[...], preferred_element_type=F32)

    @pl.when(e == n_e - 1)
    def _():
        z = alpha * x_ref[0] + (1.0 + g2_ref[0]) * acc_ref[...].T
        o_ref[0] = _ln_rows(z, lg_ref[...], lb_ref[...])


def peer_stage_b(h_bf16, u_bf16, vT_bf16, s1, s2, thr, x, g2, ln_g, ln_b, alpha):
    bn, t, d = x.shape
    tm = min(512, t)
    te = 8 * PEER_KEYS
    n_exp = u_bf16.shape[0]
    rows = g2.shape[1]
    per_row = rows == t and t > 1
    mod_block = (1, tm, d) if per_row else (1, 1, d)
    mod_map = (lambda b, i, e: (b, i, 0)) if per_row else (lambda b, i, e: (b, 0, 0))
    tpb = t // tm
    tok = lambda: pl.BlockSpec((1, tm, d), lambda b, i, e: (b, i, 0))
    col = lambda r: pl.BlockSpec((r, tm), lambda b, i, e: (0, b * tpb + i))
    nk = PEER_HEADS * PEER_KEYS
    return pl.pallas_call(
        functools.partial(_peer_b_kernel, alpha, te),
        grid=(bn, tpb, n_exp // te),
        in_specs=[tok(),
                  pl.BlockSpec((te, d), lambda b, i, e: (e, 0)),
                  pl.BlockSpec((d, te), lambda b, i, e: (0, e)),
                  col(nk), col(nk), col(PEER_HEADS),
                  tok(), pl.BlockSpec(mod_block, mod_map), _full((1, d)), _full((1, d))],
        out_specs=tok(),
        out_shape=jax.ShapeDtypeStruct((bn, t, d), F32),
        scratch_shapes=[pltpu.VMEM((d, tm), F32), pltpu.VMEM((te, tm), BF16)],
        compiler_params=_params("parallel", "parallel", "arbitrary"),
        name="peer_experts",
    )(h_bf16, u_bf16, vT_bf16, s1, s2, thr, x, g2, ln_g.reshape(1, d), ln_b.reshape(1, d))


def _pad_time(a, tp):
    return a if a.shape[1] == tp else jnp.pad(a, ((0, 0), (0, tp - a.shape[1]), (0, 0)))


def _forward_group(x, mods, rw_s0, shift0, conv0, attend, W, Wc, per_row, rw_bk):
    bn, t, d = x.shape
    depth = W['w_mix'].shape[0]
    alpha = (2.0 * depth) ** 0.25
    tp = -(-t // TIME_CHUNK) * TIME_CHUNK
    x = layer_norm_rows(x.reshape(bn * t, d), W['ln_in_g'], W['ln_in_b']).reshape(bn, t, d)
    if per_row:
        x = x.reshape(1, bn * t, d)
    ks, vs, rws, shs, cvs, cms = [], [], [], [], [], []
    for l in range(depth):
        wl = {name: W[name][l] for name in W if name not in ('ln_in_g', 'ln_in_b')}
        m = mods[l]
        if per_row:
            mv = [jnp.repeat(m[:, i], t, axis=0)[None] for i in range(6)]
        else:
            mv = [m[:, i][:, None, :] for i in range(6)]
        sh1, sc1, g1, sh2, sc2, g2 = mv
        p_rw, q, k, v, p_cv, p_cm = mix_projection(x, sc1, sh1, Wc['w_mix'][l])
        unflat = lambda a: a.reshape(bn, t, a.shape[-1])
        p_rw, q, k, v, p_cv, p_cm = map(unflat, (p_rw, q, k, v, p_cv, p_cm))
        y_rw, s_rw = rwkv_mixer(_pad_time(p_rw, tp), shift0[l], rw_s0[l], min(t, TIME_CHUNK) if tp == TIME_CHUNK
                                else TIME_CHUNK, rw_bk, wl)
        y_at = attend(l, q, k, v)
        y_cv, y_cm, v_rows, cv_st = local_mixers(_pad_time(p_cv, tp), _pad_time(p_cm, tp), conv0[l], t, wl)
        flat = (lambda a: a[:, :t].reshape(1, bn * t, -1)) if per_row else (lambda a: a[:, :t])
        ys = [flat(y_rw), flat(y_at), flat(y_cv), flat(y_cm)]
        x = out_projection(ys, x, g1, Wc['w_out'][l], wl['ln1_g'], wl['ln1_b'], alpha)
        h_bf, s1, s2, thr = peer_stage_a(x, sc2, sh2, Wc['peer_wqT'][l], Wc['peer_keys'][l])
        x = peer_stage_b(h_bf, Wc['peer_u'][l], Wc['peer_vT'][l], s1, s2, thr, x, g2,
                         wl['ln2_g'], wl['ln2_b'], alpha)
        ks.append(k.reshape(bn, t, N_HEADS, HEAD_DIM))
        vs.append(v.reshape(bn, t, N_HEADS, HEAD_DIM))
        rws.append(s_rw)
        shs.append(p_rw[:, t - 1])
        cvs.append(cv_st)
        cms.append(v_rows[:, :t])
    return (x.reshape(bn, t, d), jnp.stack(ks), jnp.stack(vs), jnp.stack(rws), jnp.stack(shs),
            jnp.stack(cvs), jnp.stack(cms))


def kernel(x_prompt, x_sample, cache_k, cache_v, state_rwkv, state_shift, state_conv, page_table, c_prompt, c_sample, ln_in_g, ln_in_b, w_ada, b_ada, w_mix, rw_mu, rw_w0, rw_w2, rw_a0, rw_a2, rw_g2, rw_kk, rw_ka, rw_rk, rw_lnx_g, rw_lnx_b, conv_w, cm_ln_g, cm_ln_b, cm_ws, cm_bs, w_out, ln1_g, ln1_b, ln2_g, ln2_b, peer_wq, peer_keys, peer_u, peer_v):
    W = {'ln_in_g': ln_in_g, 'ln_in_b': ln_in_b, 'w_mix': w_mix,
         'rw_mu': rw_mu, 'rw_w0': rw_w0, 'rw_w2': rw_w2, 'rw_a0': rw_a0, 'rw_a2': rw_a2,
         'rw_g2': rw_g2, 'rw_kk': rw_kk, 'rw_ka': rw_ka, 'rw_rk': rw_rk, 'rw_lnx_g': rw_lnx_g,
         'rw_lnx_b': rw_lnx_b, 'conv_w': conv_w, 'cm_ln_g': cm_ln_g, 'cm_ln_b': cm_ln_b,
         'cm_ws': cm_ws, 'cm_bs': cm_bs, 'ln1_g': ln1_g, 'ln1_b': ln1_b,
         'ln2_g': ln2_g, 'ln2_b': ln2_b}
    depth = w_mix.shape[0]
    Wc = {'w_mix': w_mix.astype(BF16), 'w_out': w_out.astype(BF16),
          'peer_wqT': jnp.swapaxes(peer_wq, 1, 2).astype(BF16),
          'peer_keys': peer_keys.reshape(depth, PEER_HEADS * 2, PEER_KEYS, PEER_HALF).astype(BF16),
          'peer_u': peer_u.astype(BF16),
          'peer_vT': jnp.swapaxes(peer_v, 1, 2).astype(BF16)}
    bp, dbn = x_prompt.shape[0], x_sample.shape[0]
    n_c = bp + dbn
    n_cp = -(-n_c // 8) * 8
    c_all = jnp.pad(jnp.concatenate([c_prompt, c_sample], axis=0), ((0, n_cp - n_c), (0, 0)))
    mods = ada_vectors(c_all, w_ada, b_ada).reshape(depth, n_cp, 6, D_MODEL)
    mods_p, mods_s = mods[:, :bp], mods[:, bp:n_c]
    dt = x_prompt.dtype
    z_rw = jnp.zeros((depth, bp, N_HEADS, HEAD_DIM, HEAD_DIM), dt)
    z_sh = jnp.zeros((depth, bp, RW_COLS), dt)
    z_cv = jnp.zeros((depth, bp, CONV_W - 1, GROUP_W), dt)
    ck = cache_k.reshape(cache_k.shape[0], cache_k.shape[1], PAGE_SIZE, GROUP_W)
    cv = cache_v.reshape(cache_v.shape[0], cache_v.shape[1], PAGE_SIZE, GROUP_W)
    y_p, k_p, v_p, rw_p, sh_p, cv_p, _ = _forward_group(
        x_prompt, mods_p, z_rw, z_sh, z_cv, lambda l, q, k, v: moba_prompt(q, k, v), W, Wc,
        per_row=False, rw_bk=math.gcd(bp, 4))
    y_s, k_s, v_s, rw_s, sh_s, cv_s, cm_s = _forward_group(
        x_sample, mods_s, state_rwkv, state_shift, state_conv,
        lambda l, q, k, v: moba_sample(l, q, k, v, ck, cv, page_table), W, Wc,
        per_row=True, rw_bk=math.gcd(dbn, 8))
    return (y_p, y_s, k_p, v_p, k_s, v_s, rw_p, rw_s, sh_p, sh_s, cv_p, cv_s, cm_s)
```

```python
import functools
import math

import jax
import jax.numpy as jnp
from jax import lax
from jax.experimental import pallas as pl
from jax.experimental.pallas import tpu as pltpu

F32 = jnp.float32
BF16 = jnp.bfloat16
HI = lax.Precision.HIGHEST

D_MODEL = 1024
N_MIXERS = 4
GROUP_W = D_MODEL // N_MIXERS
HEAD_DIM = 64
N_HEADS = GROUP_W // HEAD_DIM
RW_W_RANK = 32
RW_A_RANK = 32
RW_G_RANK = 64
RW_COLS = 3 * GROUP_W + RW_W_RANK + RW_A_RANK + RW_G_RANK
MOBA_COLS = 3 * GROUP_W
CONV_COLS = 3 * GROUP_W
CMLP_COLS = 2 * GROUP_W
N_COLS = RW_COLS + MOBA_COLS + CONV_COLS + CMLP_COLS
MOBA_BLOCK = 256
MOBA_TOPK = 3
Q_BLOCK = 128
PAGE_SIZE = 128
CONV_W = 3
CHUNK = 128
PEER_KEYS = 128
PEER_EXPERTS = PEER_KEYS * PEER_KEYS
PEER_HEADS = 8
PEER_TOPK = 16
PEER_QDIM = 256
PEER_HALF = PEER_QDIM // 2
LN_EPS = 1e-5
GN_EPS = 64e-5
NEG_INF = float("-inf")
LOG2_E = 1.4426950408889634

VMEM_LIMIT = 48 * 1024 * 1024
TIME_CHUNK = 128
PEER_B_PARTS = 4
RW_SUB = 64


def _params(*sem):
    return pltpu.CompilerParams(dimension_semantics=sem, vmem_limit_bytes=VMEM_LIMIT)


def _full(shape):
    n = len(shape)
    return pl.BlockSpec(shape, lambda *_: (0,) * n)


def _sigmoid(x):
    return 1.0 / (1.0 + jnp.exp(-x))


def _gelu(x):
    c = 0.7978845608028654
    return x * (0.5 + 0.5 * jnp.tanh(x * (c + (c * 0.044715) * (x * x))))


def _ln_rows(x, g, b):
    mu = jnp.mean(x, axis=-1, keepdims=True)
    d = x - mu
    var = jnp.mean(d * d, axis=-1, keepdims=True)
    return d * lax.rsqrt(var + LN_EPS) * g + b


def _dot_hi(a, b):
    return jnp.dot(a, b, precision=HI, preferred_element_type=F32)


def _dot_nt(a, b):
    return lax.dot_general(a, b, (((1,), (1,)), ((), ())), preferred_element_type=F32)


def _ln_kernel(x_ref, g_ref, b_ref, o_ref):
    o_ref[...] = _ln_rows(x_ref[...], g_ref[...], b_ref[...])


def layer_norm_rows(x2d, g, b):
    n, d = x2d.shape
    tm = min(512, n)
    return pl.pallas_call(
        _ln_kernel,
        grid=(n // tm,),
        in_specs=[pl.BlockSpec((tm, d), lambda i: (i, 0)), _full((1, d)), _full((1, d))],
        out_specs=pl.BlockSpec((tm, d), lambda i: (i, 0)),
        out_shape=jax.ShapeDtypeStruct((n, d), F32),
        compiler_params=_params("parallel"),
        name="ln_in",
    )(x2d, g.reshape(1, d), b.reshape(1, d))


def _ada_kernel(c_ref, w_ref, b_ref, o_ref):
    c = c_ref[...]
    s = (c * _sigmoid(c)).astype(BF16)
    o_ref[0] = jnp.dot(s, w_ref[0].astype(BF16), preferred_element_type=F32) + b_ref[0]


def ada_vectors(c_all, w_ada, b_ada):
    depth, d, n6 = w_ada.shape
    m = c_all.shape[0]
    tn = 1536
    return pl.pallas_call(
        _ada_kernel,
        grid=(depth, n6 // tn),
        in_specs=[_full((m, d)),
                  pl.BlockSpec((1, d, tn), lambda l, j: (l, 0, j)),
                  pl.BlockSpec((1, 1, tn), lambda l, j: (l, 0, j))],
        out_specs=pl.BlockSpec((1, m, tn), lambda l, j: (l, 0, j)),
        out_shape=jax.ShapeDtypeStruct((depth, m, n6), F32),
        compiler_params=_params("parallel", "parallel"),
        name="ada",
    )(c_all, w_ada, b_ada.reshape(depth, 1, n6))


_MIX_WIDTHS = (RW_COLS, GROUP_W, GROUP_W, GROUP_W, CONV_COLS, CMLP_COLS)


def _mix_kernel(x_ref, sc_ref, sh_ref, w_ref, *out_refs):
    h = (x_ref[0] * (1.0 + sc_ref[0]) + sh_ref[0]).astype(BF16)
    off = 0
    for ref, width in zip(out_refs, _MIX_WIDTHS):
        ref[0] = jnp.dot(h, w_ref[:, off:off + width], preferred_element_type=F32)
        off += width


def mix_projection(x, sc, sh, w_bf16):
    bn, t, d = x.shape
    tm = min(256, t)
    rows = sc.shape[1]
    per_row = rows == t and t > 1
    mod_block = (1, tm, d) if per_row else (1, 1, d)
    mod_map = (lambda b, i: (b, i, 0)) if per_row else (lambda b, i: (b, 0, 0))
    out_shape = [jax.ShapeDtypeStruct((bn, t, w), F32) for w in _MIX_WIDTHS]
    out_specs = [pl.BlockSpec((1, tm, w), lambda b, i: (b, i, 0)) for w in _MIX_WIDTHS]
    return pl.pallas_call(
        _mix_kernel,
        grid=(bn, t // tm),
        in_specs=[pl.BlockSpec((1, tm, d), lambda b, i: (b, i, 0)),
                  pl.BlockSpec(mod_block, mod_map),
                  pl.BlockSpec(mod_block, mod_map),
                  _full(w_bf16.shape)],
        out_specs=out_specs,
        out_shape=out_shape,
        compiler_params=_params("parallel", "parallel"),
        name="mix_proj",
    )(x, sc, sh, w_bf16)


def _softplus(x):
    return jnp.maximum(x, 0.0) + jnp.log(1.0 + jnp.exp(-jnp.abs(x)))


def _mm(a, b):
    return jnp.dot(a.astype(BF16), b.astype(BF16), preferred_element_type=F32)


def _dot_tn(a, b):
    return lax.dot_general(a, b, (((0,), (0,)), ((), ())), preferred_element_type=F32)


def _rwkv_subchunk(state, lw, kk, bb, k2, r, v):
    G = GROUP_W
    C = RW_SUB
    ri = lax.broadcasted_iota(jnp.int32, (C, C), 0)
    cj = lax.broadcasted_iota(jnp.int32, (C, C), 1)
    cum = _dot_hi(jnp.where(cj <= ri, 1.0, 0.0), lw)
    g_in = jnp.exp(cum)
    g_out = jnp.exp(-cum)
    at = -kk * jnp.exp(cum - lw)
    g_end = g_in[C - 1:C, :]
    lane_head = lax.broadcasted_iota(jnp.int32, (1, G), 1) // HEAD_DIM

    def stack(x):
        return jnp.concatenate([jnp.where(lane_head == h, x, 0.0) for h in range(N_HEADS)], axis=0)

    a_s, b_s, k_s, r_s, v_s = [stack(x) for x in (at, bb * g_out, k2 * g_out, r * g_in, v)]
    a_b, b_b, k_b, r_b, v_b = [x.astype(BF16) for x in (a_s, b_s, k_s, r_s, v_s)]
    s_b = state.astype(BF16)
    row = lax.broadcasted_iota(jnp.int32, (G, G), 0)
    col = lax.broadcasted_iota(jnp.int32, (G, G), 1)
    rt, ct = row % C, col % C
    strict = ct < rt
    incl = ct <= rt
    lmat = jnp.where(strict, _dot_nt(a_b, b_b), 0.0)
    a_ak = jnp.where(strict, _dot_nt(a_b, k_b), 0.0)
    a_rb = jnp.where(incl, _dot_nt(r_b, b_b), 0.0)
    a_rk = jnp.where(incl, _dot_nt(r_b, k_b), 0.0)
    rhs = _dot_nt(a_b, s_b) + _mm(a_ak, v_b)
    base = 8
    l1 = jnp.where(row // base == col // base, lmat, 0.0)
    l2 = _mm(l1, l1)
    n = l1 + l2 + _mm(l1, l2)
    l4 = _mm(l2, l2)
    n = n + l4 + _mm(n, l4)
    m = base
    while m < C:
        lower_left = jnp.logical_and(row // (2 * m) == col // (2 * m),
                                     jnp.logical_and((row // m) % 2 == 1, (col // m) % 2 == 0))
        lm = jnp.where(lower_left, lmat, 0.0)
        t1 = lm + _mm(n, lm)
        n = n + t1 + _mm(t1, n)
        m *= 2
    u_s = rhs + _mm(n, rhs)
    u_b = u_s.astype(BF16)
    y_s = _dot_nt(r_b, s_b) + _mm(a_rb, u_b) + _mm(a_rk, v_b)
    y = y_s[0:C] + y_s[C:2 * C] + y_s[2 * C:3 * C] + y_s[3 * C:4 * C]
    new_state = (state * g_end + _dot_tn(u_b, (b_s * g_end).astype(BF16))
                 + _dot_tn(v_b, (k_s * g_end).astype(BF16)))
    return y, new_state


def _rwkv_kernel(n_valid, bk,
                 p_ref, shift_ref, s0_ref, bd_ref, mu_ref, w0_ref, w2_ref, a0_ref, a2_ref, g2_ref,
                 kkp_ref, kap_ref, rk_ref, lng_ref, lnb_ref,
                 y_ref, s_ref, carry_ref):
    ci = pl.program_id(1)
    G = GROUP_W
    bd = bd_ref[...]

    @pl.when(ci == 0)
    def _():
        s_ref[...] = s0_ref[...]
        carry_ref[...] = shift_ref[...]

    row_id = lax.broadcasted_iota(jnp.int32, (TIME_CHUNK, 1), 0)
    sub_row = lax.broadcasted_iota(jnp.int32, (RW_SUB, 1), 0)

    for b in range(bk):
        p = p_ref[b]
        prev = jnp.where(row_id == 0, carry_ref[b], pltpu.roll(p, 1, axis=0))
        carry_ref[b] = p[TIME_CHUNK - 1:TIME_CHUNK, :]
        xm = p + (prev - p) * mu_ref[...]
        r, k, v = xm[:, :G], xm[:, G:2 * G], xm[:, 2 * G:3 * G]
        o = 3 * G
        wl = xm[:, o:o + RW_W_RANK]
        al = xm[:, o + RW_W_RANK:o + RW_W_RANK + RW_A_RANK]
        gl = xm[:, o + RW_W_RANK + RW_A_RANK:]
        w = -_softplus(-(w0_ref[...] + _dot_hi(jnp.tanh(wl), w2_ref[...]))) - 0.5
        lw = -jnp.exp(w)
        a = _sigmoid(a0_ref[...] + _dot_hi(al, a2_ref[...]))
        g = _dot_hi(_sigmoid(gl), g2_ref[...])
        kk = k * kkp_ref[...]
        kk = kk / jnp.maximum(jnp.sqrt(_dot_hi(kk * kk, bd)), 1e-12)
        k2 = k * (1.0 + (a - 1.0) * kap_ref[...])
        bb = kk * a
        bonus = _dot_hi(r * k2 * rk_ref[...], bd) * v

        state = s_ref[b]
        ys = []
        for j in range(TIME_CHUNK // RW_SUB):
            n_j = max(0, min(RW_SUB, n_valid - RW_SUB * j))
            rows = slice(j * RW_SUB, (j + 1) * RW_SUB)
            if n_j == 0:
                ys.append(jnp.zeros((RW_SUB, G), F32))
                continue
            lw_j, kk_j, bb_j, k2_j = lw[rows], kk[rows], bb[rows], k2[rows]
            if n_j < RW_SUB:
                live = sub_row < n_j
                lw_j, kk_j, bb_j, k2_j = [jnp.where(live, x, 0.0) for x in (lw_j, kk_j, bb_j, k2_j)]
            y_j, state = _rwkv_subchunk(state, lw_j, kk_j, bb_j, k2_j, r[rows], v[rows])
            ys.append(y_j)
        s_ref[b] = state
        y = jnp.concatenate(ys, axis=0)

        mean = _dot_hi(y, bd) * (1.0 / HEAD_DIM)
        d = y - mean
        var = _dot_hi(d * d, bd) * (1.0 / HEAD_DIM)
        yn = d * lax.rsqrt(var + GN_EPS) * lng_ref[...] + lnb_ref[...]
        y_ref[b] = (yn + bonus) * g


def rwkv_mixer(p, shift_prev, s0, n_valid, bk, wl):
    bn, tp, _ = p.shape
    G = GROUP_W
    eye_h = jnp.eye(N_HEADS, dtype=s0.dtype)
    s0bd = (s0[:, :, :, None, :] * eye_h[None, :, None, :, None]).reshape(bn, G, G)
    head_of = jnp.arange(G) // HEAD_DIM
    bd = (head_of[:, None] == head_of[None, :]).astype(F32)
    row = lambda a: a.reshape(1, -1)
    consts = [bd, row(wl['rw_mu']), row(wl['rw_w0']), wl['rw_w2'], row(wl['rw_a0']), wl['rw_a2'], wl['rw_g2'],
              row(wl['rw_kk']), row(wl['rw_ka']), row(wl['rw_rk']), row(wl['rw_lnx_g']), row(wl['rw_lnx_b'])]
    y, s_fin = pl.pallas_call(
        functools.partial(_rwkv_kernel, n_valid, bk),
        grid=(bn // bk, tp // TIME_CHUNK),
        in_specs=[pl.BlockSpec((bk, TIME_CHUNK, RW_COLS), lambda b, c: (b, c, 0)),
                  pl.BlockSpec((bk, 1, RW_COLS), lambda b, c: (b, 0, 0)),
                  pl.BlockSpec((bk, G, G), lambda b, c: (b, 0, 0))]
                 + [_full(a.shape) for a in consts],
        out_specs=[pl.BlockSpec((bk, TIME_CHUNK, G), lambda b, c: (b, c, 0)),
                   pl.BlockSpec((bk, G, G), lambda b, c: (b, 0, 0))],
        out_shape=[jax.ShapeDtypeStruct((bn, tp, G), F32),
                   jax.ShapeDtypeStruct((bn, G, G), F32)],
        scratch_shapes=[pltpu.VMEM((bk, 1, RW_COLS), F32)],
        compiler_params=_params("parallel", "arbitrary"),
        name="rwkv7",
    )(p, shift_prev.reshape(bn, 1, RW_COLS), s0bd, *consts)
    s4 = s_fin.reshape(bn, N_HEADS, HEAD_DIM, N_HEADS, HEAD_DIM)
    return y, jnp.stack([s4[:, h, :, h, :] for h in range(N_HEADS)], axis=1)


def _local_kernel(t_valid, cv_ref, cm_ref, prev_ref, cw_ref, lg_ref, lb_ref, ws_ref, bias_ref,
                  ycv_ref, ycm_ref, vrow_ref, st_ref, carry_ref):
    ci = pl.program_id(1)
    G = GROUP_W
    TC = TIME_CHUNK

    @pl.when(ci == 0)
    def _():
        carry_ref[...] = prev_ref[0]

    pc = cv_ref[0]
    bg, cg, hv = pc[:, :G], pc[:, G:2 * G], pc[:, 2 * G:]
    u = cg * hv
    row_id = lax.broadcasted_iota(jnp.int32, (TC, 1), 0)
    prev0, prev1 = carry_ref[0:1, :], carry_ref[1:2, :]
    u1 = jnp.where(row_id == 0, prev1, pltpu.roll(u, 1, axis=0))
    u2 = jnp.where(row_id == 0, prev0, jnp.where(row_id == 1, prev1, pltpu.roll(u, 2, axis=0)))
    ycv_ref[0] = bg * (cw_ref[0:1, :] * u2 + cw_ref[1:2, :] * u1 + cw_ref[2:3, :] * u)
    carry_ref[...] = u[TC - 2:, :]
    last = (t_valid - 1) // TC
    r = t_valid - last * TC

    @pl.when(ci == last)
    def _():
        st_ref[0] = u[r - 2:r, :]

    pm = cm_ref[0]
    uu = _gelu(pm[:, :G])
    vv = _ln_rows(_gelu(pm[:, G:]), lg_ref[...], lb_ref[...])
    vrow_ref[0] = vv
    head_of = lax.broadcasted_iota(jnp.int32, (1, G), 1) // HEAD_DIM
    causal = lax.broadcasted_iota(jnp.int32, (TC, TC), 0) >= lax.broadcasted_iota(jnp.int32, (TC, TC), 1)
    mixed = bias_ref[...]
    for h in range(N_HEADS):
        wm = jnp.where(causal, ws_ref[h], 0.0).astype(BF16)
        mixed = mixed + jnp.dot(wm, jnp.where(head_of == h, vv, 0.0).astype(BF16), preferred_element_type=F32)
    ycm_ref[0] = uu * mixed


def local_mixers(p_cv, p_cm, conv_prev, t_valid, wl):
    bn, tp, _ = p_cv.shape
    G = GROUP_W
    bias = jnp.repeat(wl['cm_bs'].T, HEAD_DIM, axis=1)
    row = lambda a: a.reshape(1, -1)
    tok = lambda w: pl.BlockSpec((1, TIME_CHUNK, w), lambda b, c: (b, c, 0))
    return pl.pallas_call(
        functools.partial(_local_kernel, t_valid),
        grid=(bn, tp // TIME_CHUNK),
        in_specs=[tok(CONV_COLS), tok(CMLP_COLS),
                  pl.BlockSpec((1, CONV_W - 1, G), lambda b, c: (b, 0, 0)),
                  _full((CONV_W, G)), _full((1, G)), _full((1, G)),
                  _full((N_HEADS, CHUNK, CHUNK)), _full((CHUNK, G))],
        out_specs=[tok(G), tok(G), tok(G), pl.BlockSpec((1, CONV_W - 1, G), lambda b, c: (b, 0, 0))],
        out_shape=[jax.ShapeDtypeStruct((bn, tp, G), F32)] * 3
                  + [jax.ShapeDtypeStruct((bn, CONV_W - 1, G), F32)],
        scratch_shapes=[pltpu.VMEM((CONV_W - 1, G), F32)],
        compiler_params=_params("parallel", "arbitrary"),
        name="conv_gmlp",
    )(p_cv, p_cm, conv_prev, wl['conv_w'], row(wl['cm_ln_g']), row(wl['cm_ln_b']), wl['cm_ws'], bias)


def _rank_rows(g, n_rows):
    rid = lax.broadcasted_iota(jnp.int32, g.shape, 0)
    rank = jnp.zeros(g.shape, jnp.int32)
    for j in range(n_rows):
        gj = g[j:j + 1, :]
        beats = jnp.where(gj > g, 1, jnp.where(jnp.logical_and(gj == g, rid > j), 1, 0))
        rank = rank + beats
    return rank


def _moba_prompt_kernel(n_blk, n_sel, qT_ref, k_ref, vT_ref, o_ref, kb_ref, vb_ref, kmean_ref, sel_ref):
    qi = pl.program_id(2)
    scale = HEAD_DIM ** -0.5

    @pl.when(qi == 0)
    def _():
        k = k_ref[0, 0]
        kb_ref[...] = k.astype(BF16)
        vb_ref[...] = vT_ref[0, 0].astype(BF16)
        kmean_ref[...] = jnp.mean(k.reshape(n_blk, MOBA_BLOCK, HEAD_DIM), axis=1)

    qT = qT_ref[0, 0]
    qb = qT.astype(BF16)
    own = qi // (MOBA_BLOCK // Q_BLOCK)
    own_start = pl.multiple_of(own * MOBA_BLOCK, MOBA_BLOCK)
    gate = _dot_hi(kmean_ref[...], qT)
    blk = lax.broadcasted_iota(jnp.int32, gate.shape, 0)
    valid = blk < own
    gate = jnp.where(valid, gate, NEG_INF)
    rank = _rank_rows(gate, n_blk)
    sel = jnp.where(jnp.logical_and(valid, rank < n_sel), 1.0, 0.0)
    for j in range(n_blk):
        sel_ref[j] = jnp.broadcast_to(sel[j:j + 1, :], (8, Q_BLOCK))

    def scores(start):
        return jnp.dot(kb_ref[pl.ds(start, MOBA_BLOCK), :], qb, preferred_element_type=F32) * scale

    s = scores(own_start)
    kpos = own_start + lax.broadcasted_iota(jnp.int32, s.shape, 0)
    qpos = qi * Q_BLOCK + lax.broadcasted_iota(jnp.int32, s.shape, 1)
    s = jnp.where(kpos <= qpos, s, NEG_INF)
    m = jnp.max(s, axis=0, keepdims=True)
    p = jnp.exp(s - m)
    l = jnp.sum(p, axis=0, keepdims=True)
    acc = jnp.dot(vb_ref[:, pl.ds(own_start, MOBA_BLOCK)], p.astype(BF16), preferred_element_type=F32)

    def body(j, carry):
        m, l, acc = carry
        start = pl.multiple_of(j * MOBA_BLOCK, MOBA_BLOCK)
        s = jnp.where(sel_ref[j][0:1, :] > 0.0, scores(start), NEG_INF)
        m_new = jnp.maximum(m, jnp.max(s, axis=0, keepdims=True))
        alpha = jnp.exp(m - m_new)
        p = jnp.exp(s - m_new)
        l = l * alpha + jnp.sum(p, axis=0, keepdims=True)
        acc = acc * alpha + jnp.dot(vb_ref[:, pl.ds(start, MOBA_BLOCK)], p.astype(BF16),
                                    preferred_element_type=F32)
        return m_new, l, acc

    m, l, acc = lax.fori_loop(0, own, body, (m, l, acc))
    o_ref[0, 0] = acc / l


def moba_prompt(q, k, v):
    bn, t, _ = q.shape
    n_blk = t // MOBA_BLOCK
    n_sel = min(MOBA_TOPK, n_blk - 1)
    heads = lambda a: a.reshape(bn, t, N_HEADS, HEAD_DIM)
    qT = heads(q).transpose(0, 2, 3, 1)
    kh = heads(k).transpose(0, 2, 1, 3)
    vT = heads(v).transpose(0, 2, 3, 1)
    oT = pl.pallas_call(
        functools.partial(_moba_prompt_kernel, n_blk, n_sel),
        grid=(bn, N_HEADS, t // Q_BLOCK),
        in_specs=[pl.BlockSpec((1, 1, HEAD_DIM, Q_BLOCK), lambda b, h, i: (b, h, 0, i)),
                  pl.BlockSpec((1, 1, t, HEAD_DIM), lambda b, h, i: (b, h, 0, 0)),
                  pl.BlockSpec((1, 1, HEAD_DIM, t), lambda b, h, i: (b, h, 0, 0))],
        out_specs=pl.BlockSpec((1, 1, HEAD_DIM, Q_BLOCK), lambda b, h, i: (b, h, 0, i)),
        out_shape=jax.ShapeDtypeStruct((bn, N_HEADS, HEAD_DIM, t), F32),
        scratch_shapes=[pltpu.VMEM((t, HEAD_DIM), BF16), pltpu.VMEM((HEAD_DIM, t), BF16),
                        pltpu.VMEM((n_blk, HEAD_DIM), F32), pltpu.VMEM((n_blk, 8, Q_BLOCK), F32)],
        compiler_params=_params("parallel", "parallel", "arbitrary"),
        name="moba_prompt",
    )(qT, kh, vT)
    return oT.transpose(0, 3, 1, 2).reshape(bn, t, GROUP_W)


def _rank_cols(g, n_cols):
    cid = lax.broadcasted_iota(jnp.int32, g.shape, 1)
    rank = jnp.zeros(g.shape, jnp.int32)
    for j in range(n_cols):
        gj = g[:, j:j + 1]
        rank = rank + jnp.where(gj > g, 1, jnp.where(jnp.logical_and(gj == g, cid > j), 1, 0))
    return rank


def _moba_decode_kernel(layer, n_pages, n_sel, t_new, pt_ref, q_ref, kn_ref, vn_ref, ck_hbm, cv_hbm, o_ref,
                        kbuf, vbuf, s_ref, ksem, vsem):
    b = pl.program_id(0)
    nb = pl.num_programs(0)
    slot = lax.rem(b, 2)
    G = GROUP_W
    R = N_HEADS * t_new
    scale = HEAD_DIM ** -0.5
    pages_per_blk = MOBA_BLOCK // PAGE_SIZE
    n_pf = n_pages // pages_per_blk

    def k_copy(row, p, sl):
        return pltpu.make_async_copy(ck_hbm.at[layer, pt_ref[row, p]], kbuf.at[sl, p], ksem.at[sl])

    def v_copy(p):
        return pltpu.make_async_copy(cv_hbm.at[layer, pt_ref[b, p]], vbuf.at[p], vsem.at[0])

    def for_pages(fn):
        def body(p, c):
            fn(p)
            return c
        lax.fori_loop(0, n_pages, body, 0)

    @pl.when(b == 0)
    def _():
        for_pages(lambda p: k_copy(0, p, 0).start())

    for_pages(lambda p: v_copy(p).start())

    @pl.when(b + 1 < nb)
    def _():
        for_pages(lambda p: k_copy(b + 1, p, 1 - slot).start())

    for_pages(lambda p: k_copy(b, p, slot).wait())

    q = q_ref[0]
    row_head = lax.broadcasted_iota(jnp.int32, (R, G), 0) // t_new
    lane_head = lax.broadcasted_iota(jnp.int32, (R, G), 1) // HEAD_DIM
    own_head = row_head == lane_head
    qbd = jnp.where(own_head, jnp.concatenate([q] * N_HEADS, axis=0), 0.0)
    qb = qbd.astype(BF16)
    blk_lane = lax.broadcasted_iota(jnp.int32, (1, n_pf), 1)

    def pass1(p, ksum_t):
        k_t = kbuf[slot, p]
        s_ref[p] = jnp.dot(qb, k_t.astype(BF16), preferred_element_type=F32) * scale
        return ksum_t + jnp.where(blk_lane == p // pages_per_blk, jnp.sum(k_t, axis=-1, keepdims=True), 0.0)

    ksum_t = lax.fori_loop(0, n_pages, pass1, jnp.zeros((G, n_pf), F32))
    gate = _dot_hi(qbd, ksum_t * (1.0 / MOBA_BLOCK))
    sel = jnp.where(_rank_cols(gate, n_pf) < n_sel, 1.0, 0.0)

    s_own = _dot_nt(qb, kn_ref[0].astype(BF16)) * scale
    tq = lax.broadcasted_iota(jnp.int32, s_own.shape, 0) % t_new
    tk = lax.broadcasted_iota(jnp.int32, s_own.shape, 1)
    s_own = jnp.where(tk <= tq, s_own, NEG_INF)

    def pass2(p, m):
        chosen = jnp.sum(jnp.where(blk_lane == p // pages_per_blk, sel, 0.0), axis=-1, keepdims=True)
        s = jnp.where(chosen > 0.0, s_ref[p], NEG_INF)
        s_ref[p] = s
        return jnp.maximum(m, jnp.max(s, axis=-1, keepdims=True))

    m = lax.fori_loop(0, n_pages, pass2, jnp.max(s_own, axis=-1, keepdims=True))

    for_pages(lambda p: v_copy(p).wait())

    def pass3(p, carry):
        l, acc = carry
        pr = jnp.exp(s_ref[p] - m)
        return (l + jnp.sum(pr, axis=-1, keepdims=True),
                acc + _dot_nt(pr.astype(BF16), vbuf[p].astype(BF16)))

    p_own = jnp.exp(s_own - m)
    l, acc = lax.fori_loop(
        0, n_pages, pass3,
        (jnp.sum(p_own, axis=-1, keepdims=True),
         jnp.dot(p_own.astype(BF16), vn_ref[0].astype(BF16), preferred_element_type=F32)))
    o = jnp.where(own_head, acc / l, 0.0)
    out = o[0:t_new]
    for h in range(1, N_HEADS):
        out = out + o[h * t_new:(h + 1) * t_new]
    o_ref[0] = out


def moba_sample(layer, q, k, v, cache_k, cache_v, page_table):
    db, t_new, G = q.shape
    n_pages = page_table.shape[1]
    past = n_pages * PAGE_SIZE
    assert past % MOBA_BLOCK == 0, "decode kernel expects the past to end on a MoBA block boundary"
    n_sel = min(MOBA_TOPK, past // MOBA_BLOCK)
    R = N_HEADS * t_new
    tok = pl.BlockSpec((1, t_new, G), lambda b, pt: (b, 0, 0))
    pool = pl.BlockSpec(memory_space=pl.ANY)
    return pl.pallas_call(
        functools.partial(_moba_decode_kernel, layer, n_pages, n_sel, t_new),
        grid_spec=pltpu.PrefetchScalarGridSpec(
            num_scalar_prefetch=1, grid=(db,),
            in_specs=[tok, tok, tok, pool, pool],
            out_specs=tok,
            scratch_shapes=[pltpu.VMEM((2, n_pages, G, PAGE_SIZE), F32),
                            pltpu.VMEM((n_pages, G, PAGE_SIZE), F32),
                            pltpu.VMEM((n_pages, R, PAGE_SIZE), F32),
                            pltpu.SemaphoreType.DMA((2,)), pltpu.SemaphoreType.DMA((1,))]),
        out_shape=jax.ShapeDtypeStruct((db, t_new, G), F32),
        compiler_params=_params("arbitrary"),
        name="moba_decode",
    )(page_table, q, k, v, cache_k, cache_v)


def _out_kernel(alpha, yrw_ref, yat_ref, ycv_ref, ycm_ref, x_ref, g1_ref, w_ref, lg_ref, lb_ref, o_ref):
    G = GROUP_W
    acc = None
    for i, ref in enumerate((yrw_ref, yat_ref, ycv_ref, ycm_ref)):
        part = jnp.dot(ref[0].astype(BF16), w_ref[i * G:(i + 1) * G, :], preferred_element_type=F32)
        acc = part if acc is None else acc + part
    z = alpha * x_ref[0] + (1.0 + g1_ref[0]) * acc
    o_ref[0] = _ln_rows(z, lg_ref[...], lb_ref[...])


def out_projection(ys, x, g1, w_bf16, ln_g, ln_b, alpha):
    bn, t, d = x.shape
    tm = min(256, t)
    rows = g1.shape[1]
    per_row = rows == t and t > 1
    mod_block = (1, tm, d) if per_row else (1, 1, d)
    mod_map = (lambda b, i: (b, i, 0)) if per_row else (lambda b, i: (b, 0, 0))
    tok = lambda w: pl.BlockSpec((1, tm, w), lambda b, i: (b, i, 0))
    return pl.pallas_call(
        functools.partial(_out_kernel, alpha),
        grid=(bn, t // tm),
        in_specs=[tok(GROUP_W)] * 4 + [tok(d), pl.BlockSpec(mod_block, mod_map),
                                       _full(w_bf16.shape), _full((1, d)), _full((1, d))],
        out_specs=tok(d),
        out_shape=jax.ShapeDtypeStruct((bn, t, d), F32),
        compiler_params=_params("parallel", "parallel"),
        name="out_proj_ln",
    )(*ys, x, g1, w_bf16, ln_g.reshape(1, d), ln_b.reshape(1, d))


_N_TOP = PEER_TOPK + 1


def _top_values(x, n, rows_out):
    rid = lax.broadcasted_iota(jnp.int32, (rows_out, x.shape[1]), 0)
    vals = jnp.full((rows_out, x.shape[1]), NEG_INF, F32)
    for i in range(n):
        mx = jnp.max(x, axis=0, keepdims=True)
        vals = jnp.where(rid == i, mx, vals)
        x = jnp.where(x == mx, NEG_INF, x)
    return vals


def _peer_a_kernel(x_ref, sc_ref, sh_ref, wqT_ref, keys_ref, h_ref, s1_ref, s2_ref, thr_ref):
    h = (x_ref[0] * (1.0 + sc_ref[0]) + sh_ref[0]).astype(BF16)
    h_ref[0] = h
    qT = _dot_nt(wqT_ref[...], h).astype(BF16)
    tm = h.shape[0]
    rid8 = lax.broadcasted_iota(jnp.int32, (PEER_HEADS, tm), 0)
    thr_all = jnp.zeros((PEER_HEADS, tm), F32)
    for hh in range(PEER_HEADS):
        r0 = hh * PEER_QDIM
        s1 = jnp.dot(keys_ref[2 * hh], qT[r0:r0 + PEER_HALF], preferred_element_type=F32)
        s2 = jnp.dot(keys_ref[2 * hh + 1], qT[r0 + PEER_HALF:r0 + PEER_QDIM], preferred_element_type=F32)
        v1 = _top_values(s1, _N_TOP, 24)
        v2 = _top_values(s2, _N_TOP, 24)
        cands = [v1[0:1] + v2] + [v1[a:a + 1] + v2[0:8] for a in range(1, _N_TOP)]
        best = _top_values(jnp.concatenate(cands, axis=0), _N_TOP, 24)
        m = best[0:1]
        z = jnp.sum(jnp.exp(best[0:PEER_TOPK] - m), axis=0, keepdims=True)
        off = m + jnp.log(z)
        thr = 0.5 * (best[PEER_TOPK - 1:PEER_TOPK] + best[PEER_TOPK:PEER_TOPK + 1]) - off
        thr = thr * LOG2_E
        s1_ref[hh * PEER_KEYS:(hh + 1) * PEER_KEYS, :] = (s1 - off) * LOG2_E
        s2_ref[hh * PEER_KEYS:(hh + 1) * PEER_KEYS, :] = s2 * LOG2_E
        thr_all = jnp.where(rid8 == hh, thr, thr_all)
    thr_ref[...] = thr_all


def peer_stage_a(x, sc, sh, wqT_bf16, keys_bf16):
    bn, t, d = x.shape
    n = bn * t
    tm = min(256, t)
    rows = sc.shape[1]
    per_row = rows == t and t > 1
    mod_block = (1, tm, d) if per_row else (1, 1, d)
    mod_map = (lambda b, i: (b, i, 0)) if per_row else (lambda b, i: (b, 0, 0))
    tpb = t // tm
    col = lambda r: pl.BlockSpec((r, tm), lambda b, i: (0, b * tpb + i))
    nk = PEER_HEADS * PEER_KEYS
    return pl.pallas_call(
        _peer_a_kernel,
        grid=(bn, tpb),
        in_specs=[pl.BlockSpec((1, tm, d), lambda b, i: (b, i, 0)),
                  pl.BlockSpec(mod_block, mod_map), pl.BlockSpec(mod_block, mod_map),
                  _full(wqT_bf16.shape), _full(keys_bf16.shape)],
        out_specs=[pl.BlockSpec((1, tm, d), lambda b, i: (b, i, 0)), col(nk), col(nk), col(PEER_HEADS)],
        out_shape=[jax.ShapeDtypeStruct((bn, t, d), BF16),
                   jax.ShapeDtypeStruct((nk, n), F32), jax.ShapeDtypeStruct((nk, n), F32),
                   jax.ShapeDtypeStruct((PEER_HEADS, n), F32)],
        compiler_params=_params("parallel", "parallel"),
        name="peer_scores",
    )(x, sc, sh, wqT_bf16, keys_bf16)


def _peer_b_kernel(alpha, te, h_ref, u_ref, vT_ref, s1_ref, s2_ref, thr_ref, x_ref, g2_ref, lg_ref, lb_ref,
                   o_ref, acc_ref, w_ref, *a_refs):
    e = pl.program_id(2)
    n_e = pl.num_programs(2)
    tm = w_ref.shape[1]

    @pl.when(e == 0)
    def _():
        acc_ref[...] = jnp.zeros(acc_ref.shape, F32)

    groups = te // PEER_KEYS
    assert groups == 8
    part_groups = groups // len(a_refs)
    part_rows = part_groups * PEER_KEYS
    row0 = [pl.multiple_of(hh * PEER_KEYS + e * groups, 8) for hh in range(PEER_HEADS)]
    for q, a_ref in enumerate(a_refs):
        a_ref[...] = _dot_nt(u_ref[q * part_rows:(q + 1) * part_rows, :], h_ref[0])
        tl = min(128, tm)
        for c in range(tm // tl):
            lanes = slice(c * tl, (c + 1) * tl)
            s1_rows = [s1_ref[pl.ds(row0[hh], groups), lanes] for hh in range(PEER_HEADS)]
            thr = thr_ref[:, lanes]
            for gl in range(part_groups):
                gi = q * part_groups + gl
                gate = None
                for hh in range(PEER_HEADS):
                    t = s1_rows[hh][gi:gi + 1, :] + s2_ref[hh * PEER_KEYS:(hh + 1) * PEER_KEYS, lanes]
                    part = jnp.where(t >= thr[hh:hh + 1, :], jnp.exp2(t), 0.0)
                    gate = part if gate is None else gate + part
                act = _gelu(a_ref[gl * PEER_KEYS:(gl + 1) * PEER_KEYS, lanes])
                w_ref[gi * PEER_KEYS:(gi + 1) * PEER_KEYS, lanes] = (gate * act).astype(BF16)
    acc_ref[...] += jnp.dot(vT_ref[...], w_ref[...], preferred_element_type=F32)

    @pl.when(e == n_e - 1)
    def _():
        z = alpha * x_ref[0] + (1.0 + g2_ref[0]) * acc_ref[...].T
        o_ref[0] = _ln_rows(z, lg_ref[...], lb_ref[...])


def peer_stage_b(h_bf16, u_bf16, vT_bf16, s1, s2, thr, x, g2, ln_g, ln_b, alpha):
    bn, t, d = x.shape
    tm = min(512, t)
    te = 8 * PEER_KEYS
    n_exp = u_bf16.shape[0]
    rows = g2.shape[1]
    per_row = rows == t and t > 1
    mod_block = (1, tm, d) if per_row else (1, 1, d)
    mod_map = (lambda b, i, e: (b, i, 0)) if per_row else (lambda b, i, e: (b, 0, 0))
    tpb = t // tm
    tok = lambda: pl.BlockSpec((1, tm, d), lambda b, i, e: (b, i, 0))
    col = lambda r: pl.BlockSpec((r, tm), lambda b, i, e: (0, b * tpb + i))
    nk = PEER_HEADS * PEER_KEYS
    return pl.pallas_call(
        functools.partial(_peer_b_kernel, alpha, te),
        grid=(bn, tpb, n_exp // te),
        in_specs=[tok(),
                  pl.BlockSpec((te, d), lambda b, i, e: (e, 0)),
                  pl.BlockSpec((d, te), lambda b, i, e: (0, e)),
                  col(nk), col(nk), col(PEER_HEADS),
                  tok(), pl.BlockSpec(mod_block, mod_map), _full((1, d)), _full((1, d))],
        out_specs=tok(),
        out_shape=jax.ShapeDtypeStruct((bn, t, d), F32),
        scratch_shapes=[pltpu.VMEM((d, tm), F32), pltpu.VMEM((te, tm), BF16)]
                       + [pltpu.VMEM((te // PEER_B_PARTS, tm), F32) for _ in range(PEER_B_PARTS)],
        compiler_params=_params("parallel", "parallel", "arbitrary"),
        name="peer_experts",
    )(h_bf16, u_bf16, vT_bf16, s1, s2, thr, x, g2, ln_g.reshape(1, d), ln_b.reshape(1, d))


def _pad_time(a, tp):
    return a if a.shape[1] == tp else jnp.pad(a, ((0, 0), (0, tp - a.shape[1]), (0, 0)))


def _forward_group(x, mods, rw_s0, shift0, conv0, attend, W, Wc, per_row, rw_bk):
    bn, t, d = x.shape
    depth = W['w_mix'].shape[0]
    alpha = (2.0 * depth) ** 0.25
    tp = -(-t // TIME_CHUNK) * TIME_CHUNK
    x = layer_norm_rows(x.reshape(bn * t, d), W['ln_in_g'], W['ln_in_b']).reshape(bn, t, d)
    if per_row:
        x = x.reshape(1, bn * t, d)
    ks, vs, rws, shs, cvs, cms = [], [], [], [], [], []
    for l in range(depth):
        wl = {name: W[name][l] for name in W if name not in ('ln_in_g', 'ln_in_b')}
        m = mods[l]
        if per_row:
            mv = [jnp.repeat(m[:, i], t, axis=0)[None] for i in range(6)]
        else:
            mv = [m[:, i][:, None, :] for i in range(6)]
        sh1, sc1, g1, sh2, sc2, g2 = mv
        p_rw, q, k, v, p_cv, p_cm = mix_projection(x, sc1, sh1, Wc['w_mix'][l])
        unflat = lambda a: a.reshape(bn, t, a.shape[-1])
        p_rw, q, k, v, p_cv, p_cm = map(unflat, (p_rw, q, k, v, p_cv, p_cm))
        y_rw, s_rw = rwkv_mixer(_pad_time(p_rw, tp), shift0[l], rw_s0[l], min(t, TIME_CHUNK) if tp == TIME_CHUNK
                                else TIME_CHUNK, rw_bk, wl)
        y_at = attend(l, q, k, v)
        y_cv, y_cm, v_rows, cv_st = local_mixers(_pad_time(p_cv, tp), _pad_time(p_cm, tp), conv0[l], t, wl)
        flat = (lambda a: a[:, :t].reshape(1, bn * t, -1)) if per_row else (lambda a: a[:, :t])
        ys = [flat(y_rw), flat(y_at), flat(y_cv), flat(y_cm)]
        x = out_projection(ys, x, g1, Wc['w_out'][l], wl['ln1_g'], wl['ln1_b'], alpha)
        h_bf, s1, s2, thr = peer_stage_a(x, sc2, sh2, Wc['peer_wqT'][l], Wc['peer_keys'][l])
        x = peer_stage_b(h_bf, Wc['peer_u'][l], Wc['peer_vT'][l], s1, s2, thr, x, g2,
                         wl['ln2_g'], wl['ln2_b'], alpha)
        ks.append(k.reshape(bn, t, N_HEADS, HEAD_DIM))
        vs.append(v.reshape(bn, t, N_HEADS, HEAD_DIM))
        rws.append(s_rw)
        shs.append(p_rw[:, t - 1])
        cvs.append(cv_st)
        cms.append(v_rows[:, :t])
    return (x.reshape(bn, t, d), jnp.stack(ks), jnp.stack(vs), jnp.stack(rws), jnp.stack(shs),
            jnp.stack(cvs), jnp.stack(cms))


def kernel(x_prompt, x_sample, cache_k, cache_v, state_rwkv, state_shift, state_conv, page_table, c_prompt, c_sample, ln_in_g, ln_in_b, w_ada, b_ada, w_mix, rw_mu, rw_w0, rw_w2, rw_a0, rw_a2, rw_g2, rw_kk, rw_ka, rw_rk, rw_lnx_g, rw_lnx_b, conv_w, cm_ln_g, cm_ln_b, cm_ws, cm_bs, w_out, ln1_g, ln1_b, ln2_g, ln2_b, peer_wq, peer_keys, peer_u, peer_v):
    W = {'ln_in_g': ln_in_g, 'ln_in_b': ln_in_b, 'w_mix': w_mix,
         'rw_mu': rw_mu, 'rw_w0': rw_w0, 'rw_w2': rw_w2, 'rw_a0': rw_a0, 'rw_a2': rw_a2,
         'rw_g2': rw_g2, 'rw_kk': rw_kk, 'rw_ka': rw_ka, 'rw_rk': rw_rk, 'rw_lnx_g': rw_lnx_g,
         'rw_lnx_b': rw_lnx_b, 'conv_w': conv_w, 'cm_ln_g': cm_ln_g, 'cm_ln_b': cm_ln_b,
         'cm_ws': cm_ws, 'cm_bs': cm_bs, 'ln1_g': ln1_g, 'ln1_b': ln1_b,
         'ln2_g': ln2_g, 'ln2_b': ln2_b}
    depth = w_mix.shape[0]
    Wc = {'w_mix': w_mix.astype(BF16), 'w_out': w_out.astype(BF16),
          'peer_wqT': jnp.swapaxes(peer_wq, 1, 2).astype(BF16),
          'peer_keys': peer_keys.reshape(depth, PEER_HEADS * 2, PEER_KEYS, PEER_HALF).astype(BF16),
          'peer_u': peer_u.astype(BF16),
          'peer_vT': jnp.swapaxes(peer_v, 1, 2).astype(BF16)}
    bp, dbn = x_prompt.shape[0], x_sample.shape[0]
    n_c = bp + dbn
    n_cp = -(-n_c // 8) * 8
    c_all = jnp.pad(jnp.concatenate([c_prompt, c_sample], axis=0), ((0, n_cp - n_c), (0, 0)))
    mods = ada_vectors(c_all, w_ada, b_ada).reshape(depth, n_cp, 6, D_MODEL)
    mods_p, mods_s = mods[:, :bp], mods[:, bp:n_c]
    dt = x_prompt.dtype
    z_rw = jnp.zeros((depth, bp, N_HEADS, HEAD_DIM, HEAD_DIM), dt)
    z_sh = jnp.zeros((depth, bp, RW_COLS), dt)
    z_cv = jnp.zeros((depth, bp, CONV_W - 1, GROUP_W), dt)
    pages = lambda c: c.transpose(0, 1, 3, 4, 2).reshape(c.shape[0], c.shape[1], GROUP_W, PAGE_SIZE)
    ck, cv = pages(cache_k), pages(cache_v)
    y_p, k_p, v_p, rw_p, sh_p, cv_p, _ = _forward_group(
        x_prompt, mods_p, z_rw, z_sh, z_cv, lambda l, q, k, v: moba_prompt(q, k, v), W, Wc,
        per_row=False, rw_bk=math.gcd(bp, 4))
    y_s, k_s, v_s, rw_s, sh_s, cv_s, cm_s = _forward_group(
        x_sample, mods_s, state_rwkv, state_shift, state_conv,
        lambda l, q, k, v: moba_sample(l, q, k, v, ck, cv, page_table), W, Wc,
        per_row=True, rw_bk=math.gcd(dbn, 4))
    return (y_p, y_s, k_p, v_p, k_s, v_s, rw_p, rw_s, sh_p, sh_s, cv_p, cv_s, cm_s)
```

```python
import functools
import math

import jax
import jax.numpy as jnp
from jax import lax
from jax.experimental import pallas as pl
from jax.experimental.pallas import tpu as pltpu

F32 = jnp.float32
BF16 = jnp.bfloat16
HI = lax.Precision.HIGHEST

D_MODEL = 1024
N_MIXERS = 4
GROUP_W = D_MODEL // N_MIXERS
HEAD_DIM = 64
N_HEADS = GROUP_W // HEAD_DIM
RW_W_RANK = 32
RW_A_RANK = 32
RW_G_RANK = 64
RW_COLS = 3 * GROUP_W + RW_W_RANK + RW_A_RANK + RW_G_RANK
MOBA_COLS = 3 * GROUP_W
CONV_COLS = 3 * GROUP_W
CMLP_COLS = 2 * GROUP_W
N_COLS = RW_COLS + MOBA_COLS + CONV_COLS + CMLP_COLS
MOBA_BLOCK = 256
MOBA_TOPK = 3
Q_BLOCK = 128
PAGE_SIZE = 128
CONV_W = 3
CHUNK = 128
PEER_KEYS = 128
PEER_EXPERTS = PEER_KEYS * PEER_KEYS
PEER_HEADS = 8
PEER_TOPK = 16
PEER_QDIM = 256
PEER_HALF = PEER_QDIM // 2
LN_EPS = 1e-5
GN_EPS = 64e-5
NEG_INF = float("-inf")
LOG2_E = 1.4426950408889634

VMEM_LIMIT = 48 * 1024 * 1024
TIME_CHUNK = 128
PAGE_UNROLL = 4
PEER_B_PARTS = 4
RW_SUB = 64


def _params(*sem):
    return pltpu.CompilerParams(dimension_semantics=sem, vmem_limit_bytes=VMEM_LIMIT)


def _full(shape):
    n = len(shape)
    return pl.BlockSpec(shape, lambda *_: (0,) * n)


def _sigmoid(x):
    return 1.0 / (1.0 + jnp.exp(-x))


def _gelu(x):
    c = 0.7978845608028654
    return x * (0.5 + 0.5 * jnp.tanh(x * (c + (c * 0.044715) * (x * x))))


def _ln_rows(x, g, b):
    mu = jnp.mean(x, axis=-1, keepdims=True)
    d = x - mu
    var = jnp.mean(d * d, axis=-1, keepdims=True)
    return d * lax.rsqrt(var + LN_EPS) * g + b


def _dot_hi(a, b):
    return jnp.dot(a, b, precision=HI, preferred_element_type=F32)


def _dot_nt(a, b):
    return lax.dot_general(a, b, (((1,), (1,)), ((), ())), preferred_element_type=F32)


def _ln_kernel(x_ref, g_ref, b_ref, o_ref):
    o_ref[...] = _ln_rows(x_ref[...], g_ref[...], b_ref[...])


def layer_norm_rows(x2d, g, b):
    n, d = x2d.shape
    tm = min(512, n)
    return pl.pallas_call(
        _ln_kernel,
        grid=(n // tm,),
        in_specs=[pl.BlockSpec((tm, d), lambda i: (i, 0)), _full((1, d)), _full((1, d))],
        out_specs=pl.BlockSpec((tm, d), lambda i: (i, 0)),
        out_shape=jax.ShapeDtypeStruct((n, d), F32),
        compiler_params=_params("parallel"),
        name="ln_in",
    )(x2d, g.reshape(1, d), b.reshape(1, d))


def _ada_kernel(c_ref, w_ref, b_ref, o_ref):
    c = c_ref[...]
    s = (c * _sigmoid(c)).astype(BF16)
    o_ref[0] = jnp.dot(s, w_ref[0].astype(BF16), preferred_element_type=F32) + b_ref[0]


def ada_vectors(c_all, w_ada, b_ada):
    depth, d, n6 = w_ada.shape
    m = c_all.shape[0]
    tn = 1536
    return pl.pallas_call(
        _ada_kernel,
        grid=(depth, n6 // tn),
        in_specs=[_full((m, d)),
                  pl.BlockSpec((1, d, tn), lambda l, j: (l, 0, j)),
                  pl.BlockSpec((1, 1, tn), lambda l, j: (l, 0, j))],
        out_specs=pl.BlockSpec((1, m, tn), lambda l, j: (l, 0, j)),
        out_shape=jax.ShapeDtypeStruct((depth, m, n6), F32),
        compiler_params=_params("parallel", "parallel"),
        name="ada",
    )(c_all, w_ada, b_ada.reshape(depth, 1, n6))


_MIX_WIDTHS = (RW_COLS, GROUP_W, GROUP_W, GROUP_W, CONV_COLS, CMLP_COLS)


def _mix_kernel(x_ref, sc_ref, sh_ref, w_ref, *out_refs):
    h = (x_ref[0] * (1.0 + sc_ref[0]) + sh_ref[0]).astype(BF16)
    off = 0
    for ref, width in zip(out_refs, _MIX_WIDTHS):
        ref[0] = jnp.dot(h, w_ref[:, off:off + width], preferred_element_type=F32)
        off += width


def mix_projection(x, sc, sh, w_bf16):
    bn, t, d = x.shape
    tm = min(256, t)
    rows = sc.shape[1]
    per_row = rows == t and t > 1
    mod_block = (1, tm, d) if per_row else (1, 1, d)
    mod_map = (lambda b, i: (b, i, 0)) if per_row else (lambda b, i: (b, 0, 0))
    out_shape = [jax.ShapeDtypeStruct((bn, t, w), F32) for w in _MIX_WIDTHS]
    out_specs = [pl.BlockSpec((1, tm, w), lambda b, i: (b, i, 0)) for w in _MIX_WIDTHS]
    return pl.pallas_call(
        _mix_kernel,
        grid=(bn, t // tm),
        in_specs=[pl.BlockSpec((1, tm, d), lambda b, i: (b, i, 0)),
                  pl.BlockSpec(mod_block, mod_map),
                  pl.BlockSpec(mod_block, mod_map),
                  _full(w_bf16.shape)],
        out_specs=out_specs,
        out_shape=out_shape,
        compiler_params=_params("parallel", "parallel"),
        name="mix_proj",
    )(x, sc, sh, w_bf16)


def _softplus(x):
    return jnp.maximum(x, 0.0) + jnp.log(1.0 + jnp.exp(-jnp.abs(x)))


def _mm(a, b):
    return jnp.dot(a.astype(BF16), b.astype(BF16), preferred_element_type=F32)


def _dot_tn(a, b):
    return lax.dot_general(a, b, (((0,), (0,)), ((), ())), preferred_element_type=F32)


def _rwkv_subchunk(state, lw, kk, bb, k2, r, v):
    G = GROUP_W
    C = RW_SUB
    ri = lax.broadcasted_iota(jnp.int32, (C, C), 0)
    cj = lax.broadcasted_iota(jnp.int32, (C, C), 1)
    cum = _dot_hi(jnp.where(cj <= ri, 1.0, 0.0), lw)
    g_in = jnp.exp(cum)
    g_out = jnp.exp(-cum)
    at = -kk * jnp.exp(cum - lw)
    g_end = g_in[C - 1:C, :]
    lane_head = lax.broadcasted_iota(jnp.int32, (1, G), 1) // HEAD_DIM

    def stack(x):
        return jnp.concatenate([jnp.where(lane_head == h, x, 0.0) for h in range(N_HEADS)], axis=0)

    a_s, b_s, k_s, r_s, v_s = [stack(x) for x in (at, bb * g_out, k2 * g_out, r * g_in, v)]
    a_b, b_b, k_b, r_b, v_b = [x.astype(BF16) for x in (a_s, b_s, k_s, r_s, v_s)]
    s_b = state.astype(BF16)
    row = lax.broadcasted_iota(jnp.int32, (G, G), 0)
    col = lax.broadcasted_iota(jnp.int32, (G, G), 1)
    rt, ct = row % C, col % C
    strict = ct < rt
    incl = ct <= rt
    lmat = jnp.where(strict, _dot_nt(a_b, b_b), 0.0)
    a_ak = jnp.where(strict, _dot_nt(a_b, k_b), 0.0)
    a_rb = jnp.where(incl, _dot_nt(r_b, b_b), 0.0)
    a_rk = jnp.where(incl, _dot_nt(r_b, k_b), 0.0)
    rhs = _dot_nt(a_b, s_b) + _mm(a_ak, v_b)
    base = 8
    l1 = jnp.where(row // base == col // base, lmat, 0.0)
    l2 = _mm(l1, l1)
    n = l1 + l2 + _mm(l1, l2)
    l4 = _mm(l2, l2)
    n = n + l4 + _mm(n, l4)
    m = base
    while m < C:
        lower_left = jnp.logical_and(row // (2 * m) == col // (2 * m),
                                     jnp.logical_and((row // m) % 2 == 1, (col // m) % 2 == 0))
        lm = jnp.where(lower_left, lmat, 0.0)
        t1 = lm + _mm(n, lm)
        n = n + t1 + _mm(t1, n)
        m *= 2
    u_s = rhs + _mm(n, rhs)
    u_b = u_s.astype(BF16)
    y_s = _dot_nt(r_b, s_b) + _mm(a_rb, u_b) + _mm(a_rk, v_b)
    y = y_s[0:C] + y_s[C:2 * C] + y_s[2 * C:3 * C] + y_s[3 * C:4 * C]
    new_state = (state * g_end + _dot_tn(u_b, (b_s * g_end).astype(BF16))
                 + _dot_tn(v_b, (k_s * g_end).astype(BF16)))
    return y, new_state


def _rwkv_kernel(n_valid, bk,
                 p_ref, shift_ref, s0_ref, bd_ref, mu_ref, w0_ref, w2_ref, a0_ref, a2_ref, g2_ref,
                 kkp_ref, kap_ref, rk_ref, lng_ref, lnb_ref,
                 y_ref, s_ref, carry_ref):
    ci = pl.program_id(1)
    G = GROUP_W
    bd = bd_ref[...]

    @pl.when(ci == 0)
    def _():
        s_ref[...] = s0_ref[...]
        carry_ref[...] = shift_ref[...]

    row_id = lax.broadcasted_iota(jnp.int32, (TIME_CHUNK, 1), 0)
    sub_row = lax.broadcasted_iota(jnp.int32, (RW_SUB, 1), 0)

    for b in range(bk):
        p = p_ref[b]
        prev = jnp.where(row_id == 0, carry_ref[b], pltpu.roll(p, 1, axis=0))
        carry_ref[b] = p[TIME_CHUNK - 1:TIME_CHUNK, :]
        xm = p + (prev - p) * mu_ref[...]
        r, k, v = xm[:, :G], xm[:, G:2 * G], xm[:, 2 * G:3 * G]
        o = 3 * G
        wl = xm[:, o:o + RW_W_RANK]
        al = xm[:, o + RW_W_RANK:o + RW_W_RANK + RW_A_RANK]
        gl = xm[:, o + RW_W_RANK + RW_A_RANK:]
        w = -_softplus(-(w0_ref[...] + _dot_hi(jnp.tanh(wl), w2_ref[...]))) - 0.5
        lw = -jnp.exp(w)
        a = _sigmoid(a0_ref[...] + _dot_hi(al, a2_ref[...]))
        g = _dot_hi(_sigmoid(gl), g2_ref[...])
        kk = k * kkp_ref[...]
        kk = kk / jnp.maximum(jnp.sqrt(_dot_hi(kk * kk, bd)), 1e-12)
        k2 = k * (1.0 + (a - 1.0) * kap_ref[...])
        bb = kk * a
        bonus = _dot_hi(r * k2 * rk_ref[...], bd) * v

        state = s_ref[b]
        ys = []
        for j in range(TIME_CHUNK // RW_SUB):
            n_j = max(0, min(RW_SUB, n_valid - RW_SUB * j))
            rows = slice(j * RW_SUB, (j + 1) * RW_SUB)
            if n_j == 0:
                ys.append(jnp.zeros((RW_SUB, G), F32))
                continue
            lw_j, kk_j, bb_j, k2_j = lw[rows], kk[rows], bb[rows], k2[rows]
            if n_j < RW_SUB:
                live = sub_row < n_j
                lw_j, kk_j, bb_j, k2_j = [jnp.where(live, x, 0.0) for x in (lw_j, kk_j, bb_j, k2_j)]
            y_j, state = _rwkv_subchunk(state, lw_j, kk_j, bb_j, k2_j, r[rows], v[rows])
            ys.append(y_j)
        s_ref[b] = state
        y = jnp.concatenate(ys, axis=0)

        mean = _dot_hi(y, bd) * (1.0 / HEAD_DIM)
        d = y - mean
        var = _dot_hi(d * d, bd) * (1.0 / HEAD_DIM)
        yn = d * lax.rsqrt(var + GN_EPS) * lng_ref[...] + lnb_ref[...]
        y_ref[b] = (yn + bonus) * g


def rwkv_mixer(p, shift_prev, s0, n_valid, bk, wl):
    bn, tp, _ = p.shape
    G = GROUP_W
    eye_h = jnp.eye(N_HEADS, dtype=s0.dtype)
    s0bd = (s0[:, :, :, None, :] * eye_h[None, :, None, :, None]).reshape(bn, G, G)
    head_of = jnp.arange(G) // HEAD_DIM
    bd = (head_of[:, None] == head_of[None, :]).astype(F32)
    row = lambda a: a.reshape(1, -1)
    consts = [bd, row(wl['rw_mu']), row(wl['rw_w0']), wl['rw_w2'], row(wl['rw_a0']), wl['rw_a2'], wl['rw_g2'],
              row(wl['rw_kk']), row(wl['rw_ka']), row(wl['rw_rk']), row(wl['rw_lnx_g']), row(wl['rw_lnx_b'])]
    y, s_fin = pl.pallas_call(
        functools.partial(_rwkv_kernel, n_valid, bk),
        grid=(bn // bk, tp // TIME_CHUNK),
        in_specs=[pl.BlockSpec((bk, TIME_CHUNK, RW_COLS), lambda b, c: (b, c, 0)),
                  pl.BlockSpec((bk, 1, RW_COLS), lambda b, c: (b, 0, 0)),
                  pl.BlockSpec((bk, G, G), lambda b, c: (b, 0, 0))]
                 + [_full(a.shape) for a in consts],
        out_specs=[pl.BlockSpec((bk, TIME_CHUNK, G), lambda b, c: (b, c, 0)),
                   pl.BlockSpec((bk, G, G), lambda b, c: (b, 0, 0))],
        out_shape=[jax.ShapeDtypeStruct((bn, tp, G), F32),
                   jax.ShapeDtypeStruct((bn, G, G), F32)],
        scratch_shapes=[pltpu.VMEM((bk, 1, RW_COLS), F32)],
        compiler_params=_params("parallel", "arbitrary"),
        name="rwkv7",
    )(p, shift_prev.reshape(bn, 1, RW_COLS), s0bd, *consts)
    s4 = s_fin.reshape(bn, N_HEADS, HEAD_DIM, N_HEADS, HEAD_DIM)
    return y, jnp.stack([s4[:, h, :, h, :] for h in range(N_HEADS)], axis=1)


def _local_kernel(t_valid, cv_ref, cm_ref, prev_ref, cw_ref, lg_ref, lb_ref, ws_ref, bias_ref,
                  ycv_ref, ycm_ref, vrow_ref, st_ref, carry_ref):
    ci = pl.program_id(1)
    G = GROUP_W
    TC = TIME_CHUNK

    @pl.when(ci == 0)
    def _():
        carry_ref[...] = prev_ref[0]

    pc = cv_ref[0]
    bg, cg, hv = pc[:, :G], pc[:, G:2 * G], pc[:, 2 * G:]
    u = cg * hv
    row_id = lax.broadcasted_iota(jnp.int32, (TC, 1), 0)
    prev0, prev1 = carry_ref[0:1, :], carry_ref[1:2, :]
    u1 = jnp.where(row_id == 0, prev1, pltpu.roll(u, 1, axis=0))
    u2 = jnp.where(row_id == 0, prev0, jnp.where(row_id == 1, prev1, pltpu.roll(u, 2, axis=0)))
    ycv_ref[0] = bg * (cw_ref[0:1, :] * u2 + cw_ref[1:2, :] * u1 + cw_ref[2:3, :] * u)
    carry_ref[...] = u[TC - 2:, :]
    last = (t_valid - 1) // TC
    r = t_valid - last * TC

    @pl.when(ci == last)
    def _():
        st_ref[0] = u[r - 2:r, :]

    pm = cm_ref[0]
    uu = _gelu(pm[:, :G])
    vv = _ln_rows(_gelu(pm[:, G:]), lg_ref[...], lb_ref[...])
    vrow_ref[0] = vv
    head_of = lax.broadcasted_iota(jnp.int32, (1, G), 1) // HEAD_DIM
    causal = lax.broadcasted_iota(jnp.int32, (TC, TC), 0) >= lax.broadcasted_iota(jnp.int32, (TC, TC), 1)
    mixed = bias_ref[...]
    for h in range(N_HEADS):
        wm = jnp.where(causal, ws_ref[h], 0.0).astype(BF16)
        mixed = mixed + jnp.dot(wm, jnp.where(head_of == h, vv, 0.0).astype(BF16), preferred_element_type=F32)
    ycm_ref[0] = uu * mixed


def local_mixers(p_cv, p_cm, conv_prev, t_valid, wl):
    bn, tp, _ = p_cv.shape
    G = GROUP_W
    bias = jnp.repeat(wl['cm_bs'].T, HEAD_DIM, axis=1)
    row = lambda a: a.reshape(1, -1)
    tok = lambda w: pl.BlockSpec((1, TIME_CHUNK, w), lambda b, c: (b, c, 0))
    return pl.pallas_call(
        functools.partial(_local_kernel, t_valid),
        grid=(bn, tp // TIME_CHUNK),
        in_specs=[tok(CONV_COLS), tok(CMLP_COLS),
                  pl.BlockSpec((1, CONV_W - 1, G), lambda b, c: (b, 0, 0)),
                  _full((CONV_W, G)), _full((1, G)), _full((1, G)),
                  _full((N_HEADS, CHUNK, CHUNK)), _full((CHUNK, G))],
        out_specs=[tok(G), tok(G), tok(G), pl.BlockSpec((1, CONV_W - 1, G), lambda b, c: (b, 0, 0))],
        out_shape=[jax.ShapeDtypeStruct((bn, tp, G), F32)] * 3
                  + [jax.ShapeDtypeStruct((bn, CONV_W - 1, G), F32)],
        scratch_shapes=[pltpu.VMEM((CONV_W - 1, G), F32)],
        compiler_params=_params("parallel", "arbitrary"),
        name="conv_gmlp",
    )(p_cv, p_cm, conv_prev, wl['conv_w'], row(wl['cm_ln_g']), row(wl['cm_ln_b']), wl['cm_ws'], bias)


def _rank_rows(g, n_rows):
    rid = lax.broadcasted_iota(jnp.int32, g.shape, 0)
    rank = jnp.zeros(g.shape, jnp.int32)
    for j in range(n_rows):
        gj = g[j:j + 1, :]
        beats = jnp.where(gj > g, 1, jnp.where(jnp.logical_and(gj == g, rid > j), 1, 0))
        rank = rank + beats
    return rank


def _block_mean_kernel(k_ref, o_ref):
    o_ref[0, 0] = jnp.mean(k_ref[0], axis=0, keepdims=True)


def _moba_prompt_kernel(n_blk, n_sel, qT_ref, k_ref, vT_ref, kmean_ref, o_ref, sel_ref, s_ref):
    qi = pl.program_id(1)
    own = qi // (MOBA_BLOCK // Q_BLOCK)
    own_start = pl.multiple_of(own * MOBA_BLOCK, MOBA_BLOCK)
    kpos = own_start + lax.broadcasted_iota(jnp.int32, (MOBA_BLOCK, Q_BLOCK), 0)
    qpos = qi * Q_BLOCK + lax.broadcasted_iota(jnp.int32, (MOBA_BLOCK, Q_BLOCK), 1)
    causal = kpos <= qpos
    blk = lax.broadcasted_iota(jnp.int32, (n_blk, Q_BLOCK), 0)
    valid = blk < own

    def attend(h, start, qb, keep):
        s = jnp.dot(k_ref[0, h, pl.ds(start, MOBA_BLOCK), :], qb, preferred_element_type=F32)
        return jnp.where(keep, s, NEG_INF)

    def values(h, start, p):
        return jnp.dot(vT_ref[0, h, :, pl.ds(start, MOBA_BLOCK)], p.astype(BF16), preferred_element_type=F32)

    def fold8(s):
        return jnp.max(s.reshape(MOBA_BLOCK // 8, 8, Q_BLOCK), axis=0)

    qbs, peaks = [], []
    for h in range(N_HEADS):
        qT = qT_ref[0, h]
        gate = jnp.where(valid, _dot_hi(kmean_ref[0, h], qT), NEG_INF)
        sel = jnp.where(jnp.logical_and(valid, _rank_rows(gate, n_blk) < n_sel), 1.0, 0.0)
        for j in range(n_blk):
            sel_ref[h, j] = jnp.broadcast_to(sel[j:j + 1, :], (8, Q_BLOCK))
        qb = (qT * (HEAD_DIM ** -0.5 * LOG2_E)).astype(BF16)
        s = attend(h, own_start, qb, causal)
        s_ref[h, own] = s
        qbs.append(qb)
        peaks.append(fold8(s))

    def pass_a(j, peaks):
        start = pl.multiple_of(j * MOBA_BLOCK, MOBA_BLOCK)
        out = []
        for h in range(N_HEADS):
            s = attend(h, start, qbs[h], sel_ref[h, j][0:1, :] > 0.0)
            s_ref[h, j] = s
            out.append(jnp.maximum(peaks[h], fold8(s)))
        return tuple(out)

    peaks = lax.fori_loop(0, own, pass_a, tuple(peaks))
    ms = [jnp.max(pk, axis=0, keepdims=True) for pk in peaks]

    def pass_b(j, accs):
        start = pl.multiple_of(j * MOBA_BLOCK, MOBA_BLOCK)
        return tuple(accs[h] + values(h, start, jnp.exp2(s_ref[h, j] - ms[h])) for h in range(N_HEADS))

    rows = vT_ref.shape[2]
    accs = lax.fori_loop(0, own + 1, pass_b, tuple(jnp.zeros((rows, Q_BLOCK), F32) for _ in range(N_HEADS)))
    for h in range(N_HEADS):
        o_ref[0, h] = accs[h][:HEAD_DIM] / accs[h][HEAD_DIM:HEAD_DIM + 1]


def moba_prompt(q, k, v):
    bn, t, G = q.shape
    n_blk = t // MOBA_BLOCK
    n_sel = min(MOBA_TOPK, n_blk - 1)
    kmean = pl.pallas_call(
        _block_mean_kernel,
        grid=(bn, n_blk),
        in_specs=[pl.BlockSpec((1, MOBA_BLOCK, G), lambda b, j: (b, j, 0))],
        out_specs=pl.BlockSpec((1, 1, 1, G), lambda b, j: (b, j, 0, 0)),
        out_shape=jax.ShapeDtypeStruct((bn, n_blk, 1, G), F32),
        compiler_params=_params("parallel", "parallel"),
        name="moba_block_means",
    )(k)
    heads = lambda a: a.reshape(bn, -1, N_HEADS, HEAD_DIM)
    kmean = heads(kmean).transpose(0, 2, 1, 3)
    qT = heads(q).transpose(0, 2, 3, 1)
    kh = heads(k).transpose(0, 2, 1, 3).astype(BF16)
    vT = heads(v).transpose(0, 2, 3, 1).astype(BF16)
    v_rows = HEAD_DIM + 16
    vT = jnp.concatenate([vT, jnp.ones((bn, N_HEADS, 1, t), BF16),
                          jnp.zeros((bn, N_HEADS, v_rows - HEAD_DIM - 1, t), BF16)], axis=2)
    hq = lambda rows, cols: pl.BlockSpec((1, N_HEADS, rows, cols), lambda b, i: (b, 0, 0, 0))
    tile = pl.BlockSpec((1, N_HEADS, HEAD_DIM, Q_BLOCK), lambda b, i: (b, 0, 0, i))
    oT = pl.pallas_call(
        functools.partial(_moba_prompt_kernel, n_blk, n_sel),
        grid=(bn, t // Q_BLOCK),
        in_specs=[tile, hq(t, HEAD_DIM), hq(v_rows, t), hq(n_blk, HEAD_DIM)],
        out_specs=tile,
        out_shape=jax.ShapeDtypeStruct((bn, N_HEADS, HEAD_DIM, t), F32),
        scratch_shapes=[pltpu.VMEM((N_HEADS, n_blk, 8, Q_BLOCK), F32),
                        pltpu.VMEM((N_HEADS, n_blk, MOBA_BLOCK, Q_BLOCK), F32)],
        compiler_params=_params("parallel", "arbitrary"),
        name="moba_prompt",
    )(qT, kh, vT, kmean)
    return oT.transpose(0, 3, 1, 2).reshape(bn, t, GROUP_W)


def _rank_cols(g, n_cols):
    cid = lax.broadcasted_iota(jnp.int32, g.shape, 1)
    rank = jnp.zeros(g.shape, jnp.int32)
    for j in range(n_cols):
        gj = g[:, j:j + 1]
        rank = rank + jnp.where(gj > g, 1, jnp.where(jnp.logical_and(gj == g, cid > j), 1, 0))
    return rank


def _moba_decode_kernel(layer, n_pages, n_sel, t_new, pt_ref, q_ref, kn_ref, vn_ref, ck_hbm, cv_hbm, o_ref,
                        kbuf, vbuf, s_ref, ksem, vsem):
    b = pl.program_id(0)
    nb = pl.num_programs(0)
    slot = lax.rem(b, 2)
    G = GROUP_W
    R = N_HEADS * t_new
    scale = HEAD_DIM ** -0.5
    pages_per_blk = MOBA_BLOCK // PAGE_SIZE
    n_pf = n_pages // pages_per_blk

    def k_copy(row, p, sl):
        return pltpu.make_async_copy(ck_hbm.at[layer, pt_ref[row, p]], kbuf.at[sl, p], ksem.at[sl])

    def v_copy(p):
        return pltpu.make_async_copy(cv_hbm.at[layer, pt_ref[b, p]], vbuf.at[p], vsem.at[0])

    def for_pages(fn):
        def body(p, c):
            fn(p)
            return c
        lax.fori_loop(0, n_pages, body, 0, unroll=PAGE_UNROLL)

    @pl.when(b == 0)
    def _():
        for_pages(lambda p: k_copy(0, p, 0).start())

    for_pages(lambda p: v_copy(p).start())

    @pl.when(b + 1 < nb)
    def _():
        for_pages(lambda p: k_copy(b + 1, p, 1 - slot).start())

    for_pages(lambda p: k_copy(b, p, slot).wait())

    q = q_ref[0]
    row_head = lax.broadcasted_iota(jnp.int32, (R, G), 0) // t_new
    lane_head = lax.broadcasted_iota(jnp.int32, (R, G), 1) // HEAD_DIM
    own_head = row_head == lane_head
    qbd = jnp.where(own_head, jnp.concatenate([q] * N_HEADS, axis=0), 0.0)
    qb = qbd.astype(BF16)
    blk_lane = lax.broadcasted_iota(jnp.int32, (1, n_pf), 1)

    def pass1(p, ksum_t):
        k_t = kbuf[slot, p]
        s_ref[p] = jnp.dot(qb, k_t.astype(BF16), preferred_element_type=F32) * scale
        return ksum_t + jnp.where(blk_lane == p // pages_per_blk, jnp.sum(k_t, axis=-1, keepdims=True), 0.0)

    ksum_t = lax.fori_loop(0, n_pages, pass1, jnp.zeros((G, n_pf), F32), unroll=PAGE_UNROLL)
    gate = _dot_hi(qbd, ksum_t * (1.0 / MOBA_BLOCK))
    sel = jnp.where(_rank_cols(gate, n_pf) < n_sel, 1.0, 0.0)

    s_own = _dot_nt(qb, kn_ref[0].astype(BF16)) * scale
    tq = lax.broadcasted_iota(jnp.int32, s_own.shape, 0) % t_new
    tk = lax.broadcasted_iota(jnp.int32, s_own.shape, 1)
    s_own = jnp.where(tk <= tq, s_own, NEG_INF)

    def pass2(p, m):
        chosen = jnp.sum(jnp.where(blk_lane == p // pages_per_blk, sel, 0.0), axis=-1, keepdims=True)
        s = jnp.where(chosen > 0.0, s_ref[p], NEG_INF)
        s_ref[p] = s
        return jnp.maximum(m, jnp.max(s, axis=-1, keepdims=True))

    m = lax.fori_loop(0, n_pages, pass2, jnp.max(s_own, axis=-1, keepdims=True), unroll=PAGE_UNROLL)

    for_pages(lambda p: v_copy(p).wait())

    def pass3(p, carry):
        l, acc = carry
        pr = jnp.exp(s_ref[p] - m)
        return (l + jnp.sum(pr, axis=-1, keepdims=True),
                acc + _dot_nt(pr.astype(BF16), vbuf[p].astype(BF16)))

    p_own = jnp.exp(s_own - m)
    l, acc = lax.fori_loop(
        0, n_pages, pass3,
        (jnp.sum(p_own, axis=-1, keepdims=True),
         jnp.dot(p_own.astype(BF16), vn_ref[0].astype(BF16), preferred_element_type=F32)),
        unroll=PAGE_UNROLL)
    o = jnp.where(own_head, acc / l, 0.0)
    out = o[0:t_new]
    for h in range(1, N_HEADS):
        out = out + o[h * t_new:(h + 1) * t_new]
    o_ref[0] = out


def moba_sample(layer, q, k, v, cache_k, cache_v, page_table):
    db, t_new, G = q.shape
    n_pages = page_table.shape[1]
    past = n_pages * PAGE_SIZE
    assert past % MOBA_BLOCK == 0, "decode kernel expects the past to end on a MoBA block boundary"
    n_sel = min(MOBA_TOPK, past // MOBA_BLOCK)
    R = N_HEADS * t_new
    tok = pl.BlockSpec((1, t_new, G), lambda b, pt: (b, 0, 0))
    pool = pl.BlockSpec(memory_space=pl.ANY)
    return pl.pallas_call(
        functools.partial(_moba_decode_kernel, layer, n_pages, n_sel, t_new),
        grid_spec=pltpu.PrefetchScalarGridSpec(
            num_scalar_prefetch=1, grid=(db,),
            in_specs=[tok, tok, tok, pool, pool],
            out_specs=tok,
            scratch_shapes=[pltpu.VMEM((2, n_pages, G, PAGE_SIZE), F32),
                            pltpu.VMEM((n_pages, G, PAGE_SIZE), F32),
                            pltpu.VMEM((n_pages, R, PAGE_SIZE), F32),
                            pltpu.SemaphoreType.DMA((2,)), pltpu.SemaphoreType.DMA((1,))]),
        out_shape=jax.ShapeDtypeStruct((db, t_new, G), F32),
        compiler_params=_params("arbitrary"),
        name="moba_decode",
    )(page_table, q, k, v, cache_k, cache_v)


def _out_kernel(alpha, yrw_ref, yat_ref, ycv_ref, ycm_ref, x_ref, g1_ref, w_ref, lg_ref, lb_ref, o_ref):
    G = GROUP_W
    acc = None
    for i, ref in enumerate((yrw_ref, yat_ref, ycv_ref, ycm_ref)):
        part = jnp.dot(ref[0].astype(BF16), w_ref[i * G:(i + 1) * G, :], preferred_element_type=F32)
        acc = part if acc is None else acc + part
    z = alpha * x_ref[0] + (1.0 + g1_ref[0]) * acc
    o_ref[0] = _ln_rows(z, lg_ref[...], lb_ref[...])


def out_projection(ys, x, g1, w_bf16, ln_g, ln_b, alpha):
    bn, t, d = x.shape
    tm = min(256, t)
    rows = g1.shape[1]
    per_row = rows == t and t > 1
    mod_block = (1, tm, d) if per_row else (1, 1, d)
    mod_map = (lambda b, i: (b, i, 0)) if per_row else (lambda b, i: (b, 0, 0))
    tok = lambda w: pl.BlockSpec((1, tm, w), lambda b, i: (b, i, 0))
    return pl.pallas_call(
        functools.partial(_out_kernel, alpha),
        grid=(bn, t // tm),
        in_specs=[tok(GROUP_W)] * 4 + [tok(d), pl.BlockSpec(mod_block, mod_map),
                                       _full(w_bf16.shape), _full((1, d)), _full((1, d))],
        out_specs=tok(d),
        out_shape=jax.ShapeDtypeStruct((bn, t, d), F32),
        compiler_params=_params("parallel", "parallel"),
        name="out_proj_ln",
    )(*ys, x, g1, w_bf16, ln_g.reshape(1, d), ln_b.reshape(1, d))


_N_TOP = PEER_TOPK + 1


def _top_values(x, n, rows_out):
    rid = lax.broadcasted_iota(jnp.int32, (rows_out, x.shape[1]), 0)
    vals = jnp.full((rows_out, x.shape[1]), NEG_INF, F32)
    for i in range(n):
        mx = jnp.max(x, axis=0, keepdims=True)
        vals = jnp.where(rid == i, mx, vals)
        x = jnp.where(x == mx, NEG_INF, x)
    return vals


def _peer_a_kernel(x_ref, sc_ref, sh_ref, wqT_ref, keys_ref, h_ref, s1_ref, s2_ref, thr_ref):
    h = (x_ref[0] * (1.0 + sc_ref[0]) + sh_ref[0]).astype(BF16)
    h_ref[0] = h
    qT = _dot_nt(wqT_ref[...], h).astype(BF16)
    tm = h.shape[0]
    rid8 = lax.broadcasted_iota(jnp.int32, (PEER_HEADS, tm), 0)
    thr_all = jnp.zeros((PEER_HEADS, tm), F32)
    for hh in range(PEER_HEADS):
        r0 = hh * PEER_QDIM
        s1 = jnp.dot(keys_ref[2 * hh], qT[r0:r0 + PEER_HALF], preferred_element_type=F32)
        s2 = jnp.dot(keys_ref[2 * hh + 1], qT[r0 + PEER_HALF:r0 + PEER_QDIM], preferred_element_type=F32)
        v1 = _top_values(s1, _N_TOP, 24)
        v2 = _top_values(s2, _N_TOP, 24)
        cands = [v1[0:1] + v2] + [v1[a:a + 1] + v2[0:8] for a in range(1, _N_TOP)]
        best = _top_values(jnp.concatenate(cands, axis=0), _N_TOP, 24)
        m = best[0:1]
        z = jnp.sum(jnp.exp(best[0:PEER_TOPK] - m), axis=0, keepdims=True)
        off = m + jnp.log(z)
        thr = 0.5 * (best[PEER_TOPK - 1:PEER_TOPK] + best[PEER_TOPK:PEER_TOPK + 1]) - off
        thr = thr * LOG2_E
        s1_ref[hh * PEER_KEYS:(hh + 1) * PEER_KEYS, :] = (s1 - off) * LOG2_E
        s2_ref[hh * PEER_KEYS:(hh + 1) * PEER_KEYS, :] = s2 * LOG2_E
        thr_all = jnp.where(rid8 == hh, thr, thr_all)
    thr_ref[...] = thr_all


def peer_stage_a(x, sc, sh, wqT_bf16, keys_bf16):
    bn, t, d = x.shape
    n = bn * t
    tm = min(256, t)
    rows = sc.shape[1]
    per_row = rows == t and t > 1
    mod_block = (1, tm, d) if per_row else (1, 1, d)
    mod_map = (lambda b, i: (b, i, 0)) if per_row else (lambda b, i: (b, 0, 0))
    tpb = t // tm
    col = lambda r: pl.BlockSpec((r, tm), lambda b, i: (0, b * tpb + i))
    nk = PEER_HEADS * PEER_KEYS
    return pl.pallas_call(
        _peer_a_kernel,
        grid=(bn, tpb),
        in_specs=[pl.BlockSpec((1, tm, d), lambda b, i: (b, i, 0)),
                  pl.BlockSpec(mod_block, mod_map), pl.BlockSpec(mod_block, mod_map),
                  _full(wqT_bf16.shape), _full(keys_bf16.shape)],
        out_specs=[pl.BlockSpec((1, tm, d), lambda b, i: (b, i, 0)), col(nk), col(nk), col(PEER_HEADS)],
        out_shape=[jax.ShapeDtypeStruct((bn, t, d), BF16),
                   jax.ShapeDtypeStruct((nk, n), F32), jax.ShapeDtypeStruct((nk, n), F32),
                   jax.ShapeDtypeStruct((PEER_HEADS, n), F32)],
        compiler_params=_params("parallel", "parallel"),
        name="peer_scores",
    )(x, sc, sh, wqT_bf16, keys_bf16)


def _peer_b_kernel(alpha, te, h_ref, u_ref, vT_ref, s1_ref, s2_ref, thr_ref, x_ref, g2_ref, lg_ref, lb_ref,
                   o_ref, acc_ref, w_ref, *a_refs):
    e = pl.program_id(2)
    n_tiles = pl.num_programs(2) - 1
    tm = w_ref.shape[2]
    slot = lax.rem(e, 2)

    @pl.when(e == 0)
    def _():
        acc_ref[...] = jnp.zeros(acc_ref.shape, F32)
        w_ref[1] = jnp.zeros(w_ref.shape[1:], BF16)

    def drain():
        acc_ref[...] += jnp.dot(vT_ref[...], w_ref[1 - slot], preferred_element_type=F32)

    groups = te // PEER_KEYS
    assert groups == 8
    part_groups = groups // len(a_refs)
    part_rows = part_groups * PEER_KEYS

    @pl.when(e < n_tiles)
    def _():
        row0 = [pl.multiple_of(hh * PEER_KEYS + e * groups, 8) for hh in range(PEER_HEADS)]
        for q, a_ref in enumerate(a_refs):
            a_ref[...] = _dot_nt(u_ref[q * part_rows:(q + 1) * part_rows, :], h_ref[0])
            tl = min(128, tm)
            for c in range(tm // tl):
                lanes = slice(c * tl, (c + 1) * tl)
                s1_rows = [s1_ref[pl.ds(row0[hh], groups), lanes] for hh in range(PEER_HEADS)]
                thr = thr_ref[:, lanes]
                for gl in range(part_groups):
                    gi = q * part_groups + gl
                    gate = None
                    for hh in range(PEER_HEADS):
                        t = s1_rows[hh][gi:gi + 1, :] + s2_ref[hh * PEER_KEYS:(hh + 1) * PEER_KEYS, lanes]
                        part = jnp.where(t >= thr[hh:hh + 1, :], jnp.exp2(t), 0.0)
                        gate = part if gate is None else gate + part
                    act = _gelu(a_ref[gl * PEER_KEYS:(gl + 1) * PEER_KEYS, lanes])
                    w_ref[slot, gi * PEER_KEYS:(gi + 1) * PEER_KEYS, lanes] = (gate * act).astype(BF16)
        drain()

    @pl.when(e == n_tiles)
    def _():
        drain()
        z = alpha * x_ref[0] + (1.0 + g2_ref[0]) * acc_ref[...].T
        o_ref[0] = _ln_rows(z, lg_ref[...], lb_ref[...])


def peer_stage_b(h_bf16, u_bf16, vT_bf16, s1, s2, thr, x, g2, ln_g, ln_b, alpha):
    bn, t, d = x.shape
    tm = min(512, t)
    te = 8 * PEER_KEYS
    n_tiles = u_bf16.shape[0] // te
    rows = g2.shape[1]
    per_row = rows == t and t > 1
    mod_block = (1, tm, d) if per_row else (1, 1, d)
    mod_map = (lambda b, i, e: (b, i, 0)) if per_row else (lambda b, i, e: (b, 0, 0))
    tpb = t // tm
    tok = lambda: pl.BlockSpec((1, tm, d), lambda b, i, e: (b, i, 0))
    col = lambda r: pl.BlockSpec((r, tm), lambda b, i, e: (0, b * tpb + i))
    nk = PEER_HEADS * PEER_KEYS
    return pl.pallas_call(
        functools.partial(_peer_b_kernel, alpha, te),
        grid=(bn, tpb, n_tiles + 1),
        in_specs=[tok(),
                  pl.BlockSpec((te, d), lambda b, i, e: (jnp.minimum(e, n_tiles - 1), 0)),
                  pl.BlockSpec((d, te), lambda b, i, e: (0, jnp.maximum(e - 1, 0))),
                  col(nk), col(nk), col(PEER_HEADS),
                  tok(), pl.BlockSpec(mod_block, mod_map), _full((1, d)), _full((1, d))],
        out_specs=tok(),
        out_shape=jax.ShapeDtypeStruct((bn, t, d), F32),
        scratch_shapes=[pltpu.VMEM((d, tm), F32), pltpu.VMEM((2, te, tm), BF16)]
                       + [pltpu.VMEM((te // PEER_B_PARTS, tm), F32) for _ in range(PEER_B_PARTS)],
        compiler_params=_params("parallel", "parallel", "arbitrary"),
        name="peer_experts",
    )(h_bf16, u_bf16, vT_bf16, s1, s2, thr, x, g2, ln_g.reshape(1, d), ln_b.reshape(1, d))


def _pad_time(a, tp):
    return a if a.shape[1] == tp else jnp.pad(a, ((0, 0), (0, tp - a.shape[1]), (0, 0)))


def _forward_group(x, mods, rw_s0, shift0, conv0, attend, W, Wc, per_row, rw_bk):
    bn, t, d = x.shape
    depth = W['w_mix'].shape[0]
    alpha = (2.0 * depth) ** 0.25
    tp = -(-t // TIME_CHUNK) * TIME_CHUNK
    x = layer_norm_rows(x.reshape(bn * t, d), W['ln_in_g'], W['ln_in_b']).reshape(bn, t, d)
    if per_row:
        x = x.reshape(1, bn * t, d)
    ks, vs, rws, shs, cvs, cms = [], [], [], [], [], []
    for l in range(depth):
        wl = {name: W[name][l] for name in W if name not in ('ln_in_g', 'ln_in_b')}
        m = mods[l]
        if per_row:
            mv = [jnp.repeat(m[:, i], t, axis=0)[None] for i in range(6)]
        else:
            mv = [m[:, i][:, None, :] for i in range(6)]
        sh1, sc1, g1, sh2, sc2, g2 = mv
        p_rw, q, k, v, p_cv, p_cm = mix_projection(x, sc1, sh1, Wc['w_mix'][l])
        unflat = lambda a: a.reshape(bn, t, a.shape[-1])
        p_rw, q, k, v, p_cv, p_cm = map(unflat, (p_rw, q, k, v, p_cv, p_cm))
        y_rw, s_rw = rwkv_mixer(_pad_time(p_rw, tp), shift0[l], rw_s0[l], min(t, TIME_CHUNK) if tp == TIME_CHUNK
                                else TIME_CHUNK, rw_bk, wl)
        y_at = attend(l, q, k, v)
        y_cv, y_cm, v_rows, cv_st = local_mixers(_pad_time(p_cv, tp), _pad_time(p_cm, tp), conv0[l], t, wl)
        flat = (lambda a: a[:, :t].reshape(1, bn * t, -1)) if per_row else (lambda a: a[:, :t])
        ys = [flat(y_rw), flat(y_at), flat(y_cv), flat(y_cm)]
        x = out_projection(ys, x, g1, Wc['w_out'][l], wl['ln1_g'], wl['ln1_b'], alpha)
        h_bf, s1, s2, thr = peer_stage_a(x, sc2, sh2, Wc['peer_wqT'][l], Wc['peer_keys'][l])
        x = peer_stage_b(h_bf, Wc['peer_u'][l], Wc['peer_vT'][l], s1, s2, thr, x, g2,
                         wl['ln2_g'], wl['ln2_b'], alpha)
        ks.append(k.reshape(bn, t, N_HEADS, HEAD_DIM))
        vs.append(v.reshape(bn, t, N_HEADS, HEAD_DIM))
        rws.append(s_rw)
        shs.append(p_rw[:, t - 1])
        cvs.append(cv_st)
        cms.append(v_rows[:, :t])
    return (x.reshape(bn, t, d), jnp.stack(ks), jnp.stack(vs), jnp.stack(rws), jnp.stack(shs),
            jnp.stack(cvs), jnp.stack(cms))


def kernel(x_prompt, x_sample, cache_k, cache_v, state_rwkv, state_shift, state_conv, page_table, c_prompt, c_sample, ln_in_g, ln_in_b, w_ada, b_ada, w_mix, rw_mu, rw_w0, rw_w2, rw_a0, rw_a2, rw_g2, rw_kk, rw_ka, rw_rk, rw_lnx_g, rw_lnx_b, conv_w, cm_ln_g, cm_ln_b, cm_ws, cm_bs, w_out, ln1_g, ln1_b, ln2_g, ln2_b, peer_wq, peer_keys, peer_u, peer_v):
    W = {'ln_in_g': ln_in_g, 'ln_in_b': ln_in_b, 'w_mix': w_mix,
         'rw_mu': rw_mu, 'rw_w0': rw_w0, 'rw_w2': rw_w2, 'rw_a0': rw_a0, 'rw_a2': rw_a2,
         'rw_g2': rw_g2, 'rw_kk': rw_kk, 'rw_ka': rw_ka, 'rw_rk': rw_rk, 'rw_lnx_g': rw_lnx_g,
         'rw_lnx_b': rw_lnx_b, 'conv_w': conv_w, 'cm_ln_g': cm_ln_g, 'cm_ln_b': cm_ln_b,
         'cm_ws': cm_ws, 'cm_bs': cm_bs, 'ln1_g': ln1_g, 'ln1_b': ln1_b,
         'ln2_g': ln2_g, 'ln2_b': ln2_b}
    depth = w_mix.shape[0]
    Wc = {'w_mix': w_mix.astype(BF16), 'w_out': w_out.astype(BF16),
          'peer_wqT': jnp.swapaxes(peer_wq, 1, 2).astype(BF16),
          'peer_keys': peer_keys.reshape(depth, PEER_HEADS * 2, PEER_KEYS, PEER_HALF).astype(BF16),
          'peer_u': peer_u.astype(BF16),
          'peer_vT': jnp.swapaxes(peer_v, 1, 2).astype(BF16)}
    bp, dbn = x_prompt.shape[0], x_sample.shape[0]
    n_c = bp + dbn
    n_cp = -(-n_c // 8) * 8
    c_all = jnp.pad(jnp.concatenate([c_prompt, c_sample], axis=0), ((0, n_cp - n_c), (0, 0)))
    mods = ada_vectors(c_all, w_ada, b_ada).reshape(depth, n_cp, 6, D_MODEL)
    mods_p, mods_s = mods[:, :bp], mods[:, bp:n_c]
    dt = x_prompt.dtype
    z_rw = jnp.zeros((depth, bp, N_HEADS, HEAD_DIM, HEAD_DIM), dt)
    z_sh = jnp.zeros((depth, bp, RW_COLS), dt)
    z_cv = jnp.zeros((depth, bp, CONV_W - 1, GROUP_W), dt)
    pages = lambda c: c.transpose(0, 1, 3, 4, 2).reshape(c.shape[0], c.shape[1], GROUP_W, PAGE_SIZE)
    ck, cv = pages(cache_k), pages(cache_v)
    y_p, k_p, v_p, rw_p, sh_p, cv_p, _ = _forward_group(
        x_prompt, mods_p, z_rw, z_sh, z_cv, lambda l, q, k, v: moba_prompt(q, k, v), W, Wc,
        per_row=False, rw_bk=math.gcd(bp, 4))
    y_s, k_s, v_s, rw_s, sh_s, cv_s, cm_s = _forward_group(
        x_sample, mods_s, state_rwkv, state_shift, state_conv,
        lambda l, q, k, v: moba_sample(l, q, k, v, ck, cv, page_table), W, Wc,
        per_row=True, rw_bk=math.gcd(dbn, 4))
    return (y_p, y_s, k_p, v_p, k_s, v_s, rw_p, rw_s, sh_p, sh_s, cv_p, cv_s, cm_s)
```

```python
import functools
import math

import jax
import jax.numpy as jnp
from jax import lax
from jax.experimental import pallas as pl
from jax.experimental.pallas import tpu as pltpu

F32 = jnp.float32
BF16 = jnp.bfloat16
HI = lax.Precision.HIGHEST

D_MODEL = 1024
N_MIXERS = 4
GROUP_W = D_MODEL // N_MIXERS
HEAD_DIM = 64
N_HEADS = GROUP_W // HEAD_DIM
RW_W_RANK = 32
RW_A_RANK = 32
RW_G_RANK = 64
RW_COLS = 3 * GROUP_W + RW_W_RANK + RW_A_RANK + RW_G_RANK
MOBA_COLS = 3 * GROUP_W
CONV_COLS = 3 * GROUP_W
CMLP_COLS = 2 * GROUP_W
N_COLS = RW_COLS + MOBA_COLS + CONV_COLS + CMLP_COLS
MOBA_BLOCK = 256
MOBA_TOPK = 3
Q_BLOCK = 128
PAGE_SIZE = 128
CONV_W = 3
CHUNK = 128
PEER_KEYS = 128
PEER_EXPERTS = PEER_KEYS * PEER_KEYS
PEER_HEADS = 8
PEER_TOPK = 16
PEER_QDIM = 256
PEER_HALF = PEER_QDIM // 2
LN_EPS = 1e-5
GN_EPS = 64e-5
NEG_INF = float("-inf")
LOG2_E = 1.4426950408889634

VMEM_LIMIT = 48 * 1024 * 1024
TIME_CHUNK = 128
PAGE_UNROLL = 4
PEER_TILE = 8 * PEER_KEYS
PEER_B_PARTS = 4
RW_SUB = 64


def _params(*sem):
    return pltpu.CompilerParams(dimension_semantics=sem, vmem_limit_bytes=VMEM_LIMIT)


def _full(shape):
    n = len(shape)
    return pl.BlockSpec(shape, lambda *_: (0,) * n)


def _sigmoid(x):
    return 1.0 / (1.0 + jnp.exp(-x))


def _gelu(x):
    c = 0.7978845608028654
    return x * (0.5 + 0.5 * jnp.tanh(x * (c + (c * 0.044715) * (x * x))))


def _ln_rows(x, g, b):
    mu = jnp.mean(x, axis=-1, keepdims=True)
    d = x - mu
    var = jnp.mean(d * d, axis=-1, keepdims=True)
    return d * lax.rsqrt(var + LN_EPS) * g + b


def _dot_hi(a, b):
    return jnp.dot(a, b, precision=HI, preferred_element_type=F32)


def _dot_nt(a, b):
    return lax.dot_general(a, b, (((1,), (1,)), ((), ())), preferred_element_type=F32)


def _ln_kernel(x_ref, g_ref, b_ref, o_ref):
    o_ref[...] = _ln_rows(x_ref[...], g_ref[...], b_ref[...])


def layer_norm_rows(x2d, g, b):
    n, d = x2d.shape
    tm = min(512, n)
    return pl.pallas_call(
        _ln_kernel,
        grid=(n // tm,),
        in_specs=[pl.BlockSpec((tm, d), lambda i: (i, 0)), _full((1, d)), _full((1, d))],
        out_specs=pl.BlockSpec((tm, d), lambda i: (i, 0)),
        out_shape=jax.ShapeDtypeStruct((n, d), F32),
        compiler_params=_params("parallel"),
        name="ln_in",
    )(x2d, g.reshape(1, d), b.reshape(1, d))


def _ada_kernel(c_ref, w_ref, b_ref, o_ref):
    c = c_ref[...]
    s = (c * _sigmoid(c)).astype(BF16)
    o_ref[0] = jnp.dot(s, w_ref[0].astype(BF16), preferred_element_type=F32) + b_ref[0]


def ada_vectors(c_all, w_ada, b_ada):
    depth, d, n6 = w_ada.shape
    m = c_all.shape[0]
    tn = 1536
    return pl.pallas_call(
        _ada_kernel,
        grid=(depth, n6 // tn),
        in_specs=[_full((m, d)),
                  pl.BlockSpec((1, d, tn), lambda l, j: (l, 0, j)),
                  pl.BlockSpec((1, 1, tn), lambda l, j: (l, 0, j))],
        out_specs=pl.BlockSpec((1, m, tn), lambda l, j: (l, 0, j)),
        out_shape=jax.ShapeDtypeStruct((depth, m, n6), F32),
        compiler_params=_params("parallel", "parallel"),
        name="ada",
    )(c_all, w_ada, b_ada.reshape(depth, 1, n6))


_MIX_WIDTHS = (RW_COLS, GROUP_W, GROUP_W, GROUP_W, CONV_COLS, CMLP_COLS)


def _mix_kernel(x_ref, sc_ref, sh_ref, w_ref, *out_refs):
    h = (x_ref[0] * (1.0 + sc_ref[0]) + sh_ref[0]).astype(BF16)
    off = 0
    for ref, width in zip(out_refs, _MIX_WIDTHS):
        ref[0] = jnp.dot(h, w_ref[:, off:off + width], preferred_element_type=F32)
        off += width


def mix_projection(x, sc, sh, w_bf16):
    bn, t, d = x.shape
    tm = min(256, t)
    rows = sc.shape[1]
    per_row = rows == t and t > 1
    mod_block = (1, tm, d) if per_row else (1, 1, d)
    mod_map = (lambda b, i: (b, i, 0)) if per_row else (lambda b, i: (b, 0, 0))
    out_shape = [jax.ShapeDtypeStruct((bn, t, w), F32) for w in _MIX_WIDTHS]
    out_specs = [pl.BlockSpec((1, tm, w), lambda b, i: (b, i, 0)) for w in _MIX_WIDTHS]
    return pl.pallas_call(
        _mix_kernel,
        grid=(bn, t // tm),
        in_specs=[pl.BlockSpec((1, tm, d), lambda b, i: (b, i, 0)),
                  pl.BlockSpec(mod_block, mod_map),
                  pl.BlockSpec(mod_block, mod_map),
                  _full(w_bf16.shape)],
        out_specs=out_specs,
        out_shape=out_shape,
        compiler_params=_params("parallel", "parallel"),
        name="mix_proj",
    )(x, sc, sh, w_bf16)


def _softplus(x):
    return jnp.maximum(x, 0.0) + jnp.log(1.0 + jnp.exp(-jnp.abs(x)))


def _mm(a, b):
    return jnp.dot(a.astype(BF16), b.astype(BF16), preferred_element_type=F32)


def _dot_tn(a, b):
    return lax.dot_general(a, b, (((0,), (0,)), ((), ())), preferred_element_type=F32)


def _rwkv_subchunk(state, lw, kk, bb, k2, r, v):
    G = GROUP_W
    C = RW_SUB
    ri = lax.broadcasted_iota(jnp.int32, (C, C), 0)
    cj = lax.broadcasted_iota(jnp.int32, (C, C), 1)
    cum = _dot_hi(jnp.where(cj <= ri, 1.0, 0.0), lw)
    g_in = jnp.exp(cum)
    g_out = jnp.exp(-cum)
    at = -kk * jnp.exp(cum - lw)
    g_end = g_in[C - 1:C, :]
    lane_head = lax.broadcasted_iota(jnp.int32, (1, G), 1) // HEAD_DIM

    def stack(x):
        return jnp.concatenate([jnp.where(lane_head == h, x, 0.0) for h in range(N_HEADS)], axis=0)

    a_s, b_s, k_s, r_s, v_s = [stack(x) for x in (at, bb * g_out, k2 * g_out, r * g_in, v)]
    a_b, b_b, k_b, r_b, v_b = [x.astype(BF16) for x in (a_s, b_s, k_s, r_s, v_s)]
    s_b = state.astype(BF16)
    row = lax.broadcasted_iota(jnp.int32, (G, G), 0)
    col = lax.broadcasted_iota(jnp.int32, (G, G), 1)
    rt, ct = row % C, col % C
    strict = ct < rt
    incl = ct <= rt
    lmat = jnp.where(strict, _dot_nt(a_b, b_b), 0.0)
    a_ak = jnp.where(strict, _dot_nt(a_b, k_b), 0.0)
    a_rb = jnp.where(incl, _dot_nt(r_b, b_b), 0.0)
    a_rk = jnp.where(incl, _dot_nt(r_b, k_b), 0.0)
    rhs = _dot_nt(a_b, s_b) + _mm(a_ak, v_b)
    base = 8
    l1 = jnp.where(row // base == col // base, lmat, 0.0)
    l2 = _mm(l1, l1)
    n = l1 + l2 + _mm(l1, l2)
    l4 = _mm(l2, l2)
    n = n + l4 + _mm(n, l4)
    m = base
    while m < C:
        lower_left = jnp.logical_and(row // (2 * m) == col // (2 * m),
                                     jnp.logical_and((row // m) % 2 == 1, (col // m) % 2 == 0))
        lm = jnp.where(lower_left, lmat, 0.0)
        t1 = lm + _mm(n, lm)
        n = n + t1 + _mm(t1, n)
        m *= 2
    u_s = rhs + _mm(n, rhs)
    u_b = u_s.astype(BF16)
    y_s = _dot_nt(r_b, s_b) + _mm(a_rb, u_b) + _mm(a_rk, v_b)
    y = y_s[0:C] + y_s[C:2 * C] + y_s[2 * C:3 * C] + y_s[3 * C:4 * C]
    new_state = (state * g_end + _dot_tn(u_b, (b_s * g_end).astype(BF16))
                 + _dot_tn(v_b, (k_s * g_end).astype(BF16)))
    return y, new_state


def _rwkv_kernel(n_valid, bk,
                 p_ref, shift_ref, s0_ref, bd_ref, mu_ref, w0_ref, w2_ref, a0_ref, a2_ref, g2_ref,
                 kkp_ref, kap_ref, rk_ref, lng_ref, lnb_ref,
                 y_ref, s_ref, carry_ref):
    ci = pl.program_id(1)
    G = GROUP_W
    bd = bd_ref[...]

    @pl.when(ci == 0)
    def _():
        s_ref[...] = s0_ref[...]
        carry_ref[...] = shift_ref[...]

    row_id = lax.broadcasted_iota(jnp.int32, (TIME_CHUNK, 1), 0)
    sub_row = lax.broadcasted_iota(jnp.int32, (RW_SUB, 1), 0)

    for b in range(bk):
        p = p_ref[b]
        prev = jnp.where(row_id == 0, carry_ref[b], pltpu.roll(p, 1, axis=0))
        carry_ref[b] = p[TIME_CHUNK - 1:TIME_CHUNK, :]
        xm = p + (prev - p) * mu_ref[...]
        r, k, v = xm[:, :G], xm[:, G:2 * G], xm[:, 2 * G:3 * G]
        o = 3 * G
        wl = xm[:, o:o + RW_W_RANK]
        al = xm[:, o + RW_W_RANK:o + RW_W_RANK + RW_A_RANK]
        gl = xm[:, o + RW_W_RANK + RW_A_RANK:]
        w = -_softplus(-(w0_ref[...] + _dot_hi(jnp.tanh(wl), w2_ref[...]))) - 0.5
        lw = -jnp.exp(w)
        a = _sigmoid(a0_ref[...] + _dot_hi(al, a2_ref[...]))
        g = _dot_hi(_sigmoid(gl), g2_ref[...])
        kk = k * kkp_ref[...]
        kk = kk / jnp.maximum(jnp.sqrt(_dot_hi(kk * kk, bd)), 1e-12)
        k2 = k * (1.0 + (a - 1.0) * kap_ref[...])
        bb = kk * a
        bonus = _dot_hi(r * k2 * rk_ref[...], bd) * v

        state = s_ref[b]
        ys = []
        for j in range(TIME_CHUNK // RW_SUB):
            n_j = max(0, min(RW_SUB, n_valid - RW_SUB * j))
            rows = slice(j * RW_SUB, (j + 1) * RW_SUB)
            if n_j == 0:
                ys.append(jnp.zeros((RW_SUB, G), F32))
                continue
            lw_j, kk_j, bb_j, k2_j = lw[rows], kk[rows], bb[rows], k2[rows]
            if n_j < RW_SUB:
                live = sub_row < n_j
                lw_j, kk_j, bb_j, k2_j = [jnp.where(live, x, 0.0) for x in (lw_j, kk_j, bb_j, k2_j)]
            y_j, state = _rwkv_subchunk(state, lw_j, kk_j, bb_j, k2_j, r[rows], v[rows])
            ys.append(y_j)
        s_ref[b] = state
        y = jnp.concatenate(ys, axis=0)

        mean = _dot_hi(y, bd) * (1.0 / HEAD_DIM)
        d = y - mean
        var = _dot_hi(d * d, bd) * (1.0 / HEAD_DIM)
        yn = d * lax.rsqrt(var + GN_EPS) * lng_ref[...] + lnb_ref[...]
        y_ref[b] = (yn + bonus) * g


def rwkv_mixer(p, shift_prev, s0, n_valid, bk, wl):
    bn, tp, _ = p.shape
    G = GROUP_W
    eye_h = jnp.eye(N_HEADS, dtype=s0.dtype)
    s0bd = (s0[:, :, :, None, :] * eye_h[None, :, None, :, None]).reshape(bn, G, G)
    head_of = jnp.arange(G) // HEAD_DIM
    bd = (head_of[:, None] == head_of[None, :]).astype(F32)
    row = lambda a: a.reshape(1, -1)
    consts = [bd, row(wl['rw_mu']), row(wl['rw_w0']), wl['rw_w2'], row(wl['rw_a0']), wl['rw_a2'], wl['rw_g2'],
              row(wl['rw_kk']), row(wl['rw_ka']), row(wl['rw_rk']), row(wl['rw_lnx_g']), row(wl['rw_lnx_b'])]
    y, s_fin = pl.pallas_call(
        functools.partial(_rwkv_kernel, n_valid, bk),
        grid=(bn // bk, tp // TIME_CHUNK),
        in_specs=[pl.BlockSpec((bk, TIME_CHUNK, RW_COLS), lambda b, c: (b, c, 0)),
                  pl.BlockSpec((bk, 1, RW_COLS), lambda b, c: (b, 0, 0)),
                  pl.BlockSpec((bk, G, G), lambda b, c: (b, 0, 0))]
                 + [_full(a.shape) for a in consts],
        out_specs=[pl.BlockSpec((bk, TIME_CHUNK, G), lambda b, c: (b, c, 0)),
                   pl.BlockSpec((bk, G, G), lambda b, c: (b, 0, 0))],
        out_shape=[jax.ShapeDtypeStruct((bn, tp, G), F32),
                   jax.ShapeDtypeStruct((bn, G, G), F32)],
        scratch_shapes=[pltpu.VMEM((bk, 1, RW_COLS), F32)],
        compiler_params=_params("parallel", "arbitrary"),
        name="rwkv7",
    )(p, shift_prev.reshape(bn, 1, RW_COLS), s0bd, *consts)
    s4 = s_fin.reshape(bn, N_HEADS, HEAD_DIM, N_HEADS, HEAD_DIM)
    return y, jnp.stack([s4[:, h, :, h, :] for h in range(N_HEADS)], axis=1)


def _local_kernel(t_valid, cv_ref, cm_ref, prev_ref, cw_ref, lg_ref, lb_ref, ws_ref, bias_ref,
                  ycv_ref, ycm_ref, vrow_ref, st_ref, carry_ref):
    ci = pl.program_id(1)
    G = GROUP_W
    TC = TIME_CHUNK

    @pl.when(ci == 0)
    def _():
        carry_ref[...] = prev_ref[0]

    pc = cv_ref[0]
    bg, cg, hv = pc[:, :G], pc[:, G:2 * G], pc[:, 2 * G:]
    u = cg * hv
    row_id = lax.broadcasted_iota(jnp.int32, (TC, 1), 0)
    prev0, prev1 = carry_ref[0:1, :], carry_ref[1:2, :]
    u1 = jnp.where(row_id == 0, prev1, pltpu.roll(u, 1, axis=0))
    u2 = jnp.where(row_id == 0, prev0, jnp.where(row_id == 1, prev1, pltpu.roll(u, 2, axis=0)))
    ycv_ref[0] = bg * (cw_ref[0:1, :] * u2 + cw_ref[1:2, :] * u1 + cw_ref[2:3, :] * u)
    carry_ref[...] = u[TC - 2:, :]
    last = (t_valid - 1) // TC
    r = t_valid - last * TC

    @pl.when(ci == last)
    def _():
        st_ref[0] = u[r - 2:r, :]

    pm = cm_ref[0]
    uu = _gelu(pm[:, :G])
    vv = _ln_rows(_gelu(pm[:, G:]), lg_ref[...], lb_ref[...])
    vrow_ref[0] = vv
    head_of = lax.broadcasted_iota(jnp.int32, (1, G), 1) // HEAD_DIM
    causal = lax.broadcasted_iota(jnp.int32, (TC, TC), 0) >= lax.broadcasted_iota(jnp.int32, (TC, TC), 1)
    mixed = bias_ref[...]
    for h in range(N_HEADS):
        wm = jnp.where(causal, ws_ref[h], 0.0).astype(BF16)
        mixed = mixed + jnp.dot(wm, jnp.where(head_of == h, vv, 0.0).astype(BF16), preferred_element_type=F32)
    ycm_ref[0] = uu * mixed


def local_mixers(p_cv, p_cm, conv_prev, t_valid, wl):
    bn, tp, _ = p_cv.shape
    G = GROUP_W
    bias = jnp.repeat(wl['cm_bs'].T, HEAD_DIM, axis=1)
    row = lambda a: a.reshape(1, -1)
    tok = lambda w: pl.BlockSpec((1, TIME_CHUNK, w), lambda b, c: (b, c, 0))
    return pl.pallas_call(
        functools.partial(_local_kernel, t_valid),
        grid=(bn, tp // TIME_CHUNK),
        in_specs=[tok(CONV_COLS), tok(CMLP_COLS),
                  pl.BlockSpec((1, CONV_W - 1, G), lambda b, c: (b, 0, 0)),
                  _full((CONV_W, G)), _full((1, G)), _full((1, G)),
                  _full((N_HEADS, CHUNK, CHUNK)), _full((CHUNK, G))],
        out_specs=[tok(G), tok(G), tok(G), pl.BlockSpec((1, CONV_W - 1, G), lambda b, c: (b, 0, 0))],
        out_shape=[jax.ShapeDtypeStruct((bn, tp, G), F32)] * 3
                  + [jax.ShapeDtypeStruct((bn, CONV_W - 1, G), F32)],
        scratch_shapes=[pltpu.VMEM((CONV_W - 1, G), F32)],
        compiler_params=_params("parallel", "arbitrary"),
        name="conv_gmlp",
    )(p_cv, p_cm, conv_prev, wl['conv_w'], row(wl['cm_ln_g']), row(wl['cm_ln_b']), wl['cm_ws'], bias)


def _rank_rows(g, n_rows):
    rid = lax.broadcasted_iota(jnp.int32, g.shape, 0)
    rank = jnp.zeros(g.shape, jnp.int32)
    for j in range(n_rows):
        gj = g[j:j + 1, :]
        beats = jnp.where(gj > g, 1, jnp.where(jnp.logical_and(gj == g, rid > j), 1, 0))
        rank = rank + beats
    return rank


def _block_mean_kernel(k_ref, o_ref):
    o_ref[0, 0] = jnp.mean(k_ref[0], axis=0, keepdims=True)


def _moba_prompt_kernel(n_blk, n_sel, qT_ref, k_ref, vT_ref, kmean_ref, o_ref, sel_ref, s_ref):
    qi = pl.program_id(1)
    own = qi // (MOBA_BLOCK // Q_BLOCK)
    own_start = pl.multiple_of(own * MOBA_BLOCK, MOBA_BLOCK)
    kpos = own_start + lax.broadcasted_iota(jnp.int32, (MOBA_BLOCK, Q_BLOCK), 0)
    qpos = qi * Q_BLOCK + lax.broadcasted_iota(jnp.int32, (MOBA_BLOCK, Q_BLOCK), 1)
    causal = kpos <= qpos
    blk = lax.broadcasted_iota(jnp.int32, (n_blk, Q_BLOCK), 0)
    valid = blk < own

    def attend(h, start, qb, keep):
        s = jnp.dot(k_ref[0, h, pl.ds(start, MOBA_BLOCK), :], qb, preferred_element_type=F32)
        return jnp.where(keep, s, NEG_INF)

    def values(h, start, p):
        return jnp.dot(vT_ref[0, h, :, pl.ds(start, MOBA_BLOCK)], p.astype(BF16), preferred_element_type=F32)

    def fold8(s):
        return jnp.max(s.reshape(MOBA_BLOCK // 8, 8, Q_BLOCK), axis=0)

    qbs, peaks = [], []
    for h in range(N_HEADS):
        qT = qT_ref[0, h]
        gate = jnp.where(valid, _dot_hi(kmean_ref[0, h], qT), NEG_INF)
        sel = jnp.where(jnp.logical_and(valid, _rank_rows(gate, n_blk) < n_sel), 1.0, 0.0)
        for j in range(n_blk):
            sel_ref[h, j] = jnp.broadcast_to(sel[j:j + 1, :], (8, Q_BLOCK))
        qb = (qT * (HEAD_DIM ** -0.5 * LOG2_E)).astype(BF16)
        s = attend(h, own_start, qb, causal)
        s_ref[h, own] = s
        qbs.append(qb)
        peaks.append(fold8(s))

    def pass_a(j, peaks):
        start = pl.multiple_of(j * MOBA_BLOCK, MOBA_BLOCK)
        out = []
        for h in range(N_HEADS):
            s = attend(h, start, qbs[h], sel_ref[h, j][0:1, :] > 0.0)
            s_ref[h, j] = s
            out.append(jnp.maximum(peaks[h], fold8(s)))
        return tuple(out)

    peaks = lax.fori_loop(0, own, pass_a, tuple(peaks))
    ms = [jnp.max(pk, axis=0, keepdims=True) for pk in peaks]

    def pass_b(j, accs):
        start = pl.multiple_of(j * MOBA_BLOCK, MOBA_BLOCK)
        return tuple(accs[h] + values(h, start, jnp.exp2(s_ref[h, j] - ms[h])) for h in range(N_HEADS))

    rows = vT_ref.shape[2]
    accs = lax.fori_loop(0, own + 1, pass_b, tuple(jnp.zeros((rows, Q_BLOCK), F32) for _ in range(N_HEADS)))
    for h in range(N_HEADS):
        o_ref[0, h] = accs[h][:HEAD_DIM] / accs[h][HEAD_DIM:HEAD_DIM + 1]


def moba_prompt(q, k, v):
    bn, t, G = q.shape
    n_blk = t // MOBA_BLOCK
    n_sel = min(MOBA_TOPK, n_blk - 1)
    kmean = pl.pallas_call(
        _block_mean_kernel,
        grid=(bn, n_blk),
        in_specs=[pl.BlockSpec((1, MOBA_BLOCK, G), lambda b, j: (b, j, 0))],
        out_specs=pl.BlockSpec((1, 1, 1, G), lambda b, j: (b, j, 0, 0)),
        out_shape=jax.ShapeDtypeStruct((bn, n_blk, 1, G), F32),
        compiler_params=_params("parallel", "parallel"),
        name="moba_block_means",
    )(k)
    heads = lambda a: a.reshape(bn, -1, N_HEADS, HEAD_DIM)
    kmean = heads(kmean).transpose(0, 2, 1, 3)
    qT = heads(q).transpose(0, 2, 3, 1)
    kh = heads(k).transpose(0, 2, 1, 3).astype(BF16)
    vT = heads(v).transpose(0, 2, 3, 1).astype(BF16)
    v_rows = HEAD_DIM + 16
    vT = jnp.concatenate([vT, jnp.ones((bn, N_HEADS, 1, t), BF16),
                          jnp.zeros((bn, N_HEADS, v_rows - HEAD_DIM - 1, t), BF16)], axis=2)
    hq = lambda rows, cols: pl.BlockSpec((1, N_HEADS, rows, cols), lambda b, i: (b, 0, 0, 0))
    tile = pl.BlockSpec((1, N_HEADS, HEAD_DIM, Q_BLOCK), lambda b, i: (b, 0, 0, i))
    oT = pl.pallas_call(
        functools.partial(_moba_prompt_kernel, n_blk, n_sel),
        grid=(bn, t // Q_BLOCK),
        in_specs=[tile, hq(t, HEAD_DIM), hq(v_rows, t), hq(n_blk, HEAD_DIM)],
        out_specs=tile,
        out_shape=jax.ShapeDtypeStruct((bn, N_HEADS, HEAD_DIM, t), F32),
        scratch_shapes=[pltpu.VMEM((N_HEADS, n_blk, 8, Q_BLOCK), F32),
                        pltpu.VMEM((N_HEADS, n_blk, MOBA_BLOCK, Q_BLOCK), F32)],
        compiler_params=_params("parallel", "arbitrary"),
        name="moba_prompt",
    )(qT, kh, vT, kmean)
    return oT.transpose(0, 3, 1, 2).reshape(bn, t, GROUP_W)


def _rank_cols(g, n_cols):
    cid = lax.broadcasted_iota(jnp.int32, g.shape, 1)
    rank = jnp.zeros(g.shape, jnp.int32)
    for j in range(n_cols):
        gj = g[:, j:j + 1]
        rank = rank + jnp.where(gj > g, 1, jnp.where(jnp.logical_and(gj == g, cid > j), 1, 0))
    return rank


def _moba_decode_kernel(layer, n_pages, n_sel, t_new, pt_ref, q_ref, kn_ref, vn_ref, ck_hbm, cv_hbm, o_ref,
                        kbuf, vbuf, s_ref, ksem, vsem):
    b = pl.program_id(0)
    nb = pl.num_programs(0)
    slot = lax.rem(b, 2)
    G = GROUP_W
    R = N_HEADS * t_new
    scale = HEAD_DIM ** -0.5
    pages_per_blk = MOBA_BLOCK // PAGE_SIZE
    n_pf = n_pages // pages_per_blk

    def k_copy(row, p, sl):
        return pltpu.make_async_copy(ck_hbm.at[layer, pt_ref[row, p]], kbuf.at[sl, p], ksem.at[sl])

    def v_copy(p):
        return pltpu.make_async_copy(cv_hbm.at[layer, pt_ref[b, p]], vbuf.at[p], vsem.at[0])

    def for_pages(fn):
        def body(p, c):
            fn(p)
            return c
        lax.fori_loop(0, n_pages, body, 0, unroll=PAGE_UNROLL)

    @pl.when(b == 0)
    def _():
        for_pages(lambda p: k_copy(0, p, 0).start())

    for_pages(lambda p: v_copy(p).start())

    @pl.when(b + 1 < nb)
    def _():
        for_pages(lambda p: k_copy(b + 1, p, 1 - slot).start())

    for_pages(lambda p: k_copy(b, p, slot).wait())

    q = q_ref[0]
    row_head = lax.broadcasted_iota(jnp.int32, (R, G), 0) // t_new
    lane_head = lax.broadcasted_iota(jnp.int32, (R, G), 1) // HEAD_DIM
    own_head = row_head == lane_head
    qbd = jnp.where(own_head, jnp.concatenate([q] * N_HEADS, axis=0), 0.0)
    qb = qbd.astype(BF16)
    blk_lane = lax.broadcasted_iota(jnp.int32, (1, n_pf), 1)

    def pass1(p, ksum_t):
        k_t = kbuf[slot, p]
        s_ref[p] = jnp.dot(qb, k_t.astype(BF16), preferred_element_type=F32) * scale
        return ksum_t + jnp.where(blk_lane == p // pages_per_blk, jnp.sum(k_t, axis=-1, keepdims=True), 0.0)

    ksum_t = lax.fori_loop(0, n_pages, pass1, jnp.zeros((G, n_pf), F32), unroll=PAGE_UNROLL)
    gate = _dot_hi(qbd, ksum_t * (1.0 / MOBA_BLOCK))
    sel = jnp.where(_rank_cols(gate, n_pf) < n_sel, 1.0, 0.0)

    s_own = _dot_nt(qb, kn_ref[0].astype(BF16)) * scale
    tq = lax.broadcasted_iota(jnp.int32, s_own.shape, 0) % t_new
    tk = lax.broadcasted_iota(jnp.int32, s_own.shape, 1)
    s_own = jnp.where(tk <= tq, s_own, NEG_INF)

    def pass2(p, m):
        chosen = jnp.sum(jnp.where(blk_lane == p // pages_per_blk, sel, 0.0), axis=-1, keepdims=True)
        s = jnp.where(chosen > 0.0, s_ref[p], NEG_INF)
        s_ref[p] = s
        return jnp.maximum(m, jnp.max(s, axis=-1, keepdims=True))

    m = lax.fori_loop(0, n_pages, pass2, jnp.max(s_own, axis=-1, keepdims=True), unroll=PAGE_UNROLL)

    for_pages(lambda p: v_copy(p).wait())

    def pass3(p, carry):
        l, acc = carry
        pr = jnp.exp(s_ref[p] - m)
        return (l + jnp.sum(pr, axis=-1, keepdims=True),
                acc + _dot_nt(pr.astype(BF16), vbuf[p].astype(BF16)))

    p_own = jnp.exp(s_own - m)
    l, acc = lax.fori_loop(
        0, n_pages, pass3,
        (jnp.sum(p_own, axis=-1, keepdims=True),
         jnp.dot(p_own.astype(BF16), vn_ref[0].astype(BF16), preferred_element_type=F32)),
        unroll=PAGE_UNROLL)
    o = jnp.where(own_head, acc / l, 0.0)
    out = o[0:t_new]
    for h in range(1, N_HEADS):
        out = out + o[h * t_new:(h + 1) * t_new]
    o_ref[0] = out


def moba_sample(layer, q, k, v, cache_k, cache_v, page_table):
    db, t_new, G = q.shape
    n_pages = page_table.shape[1]
    past = n_pages * PAGE_SIZE
    assert past % MOBA_BLOCK == 0, "decode kernel expects the past to end on a MoBA block boundary"
    n_sel = min(MOBA_TOPK, past // MOBA_BLOCK)
    R = N_HEADS * t_new
    tok = pl.BlockSpec((1, t_new, G), lambda b, pt: (b, 0, 0))
    pool = pl.BlockSpec(memory_space=pl.ANY)
    return pl.pallas_call(
        functools.partial(_moba_decode_kernel, layer, n_pages, n_sel, t_new),
        grid_spec=pltpu.PrefetchScalarGridSpec(
            num_scalar_prefetch=1, grid=(db,),
            in_specs=[tok, tok, tok, pool, pool],
            out_specs=tok,
            scratch_shapes=[pltpu.VMEM((2, n_pages, G, PAGE_SIZE), F32),
                            pltpu.VMEM((n_pages, G, PAGE_SIZE), F32),
                            pltpu.VMEM((n_pages, R, PAGE_SIZE), F32),
                            pltpu.SemaphoreType.DMA((2,)), pltpu.SemaphoreType.DMA((1,))]),
        out_shape=jax.ShapeDtypeStruct((db, t_new, G), F32),
        compiler_params=_params("arbitrary"),
        name="moba_decode",
    )(page_table, q, k, v, cache_k, cache_v)


def _out_kernel(alpha, yrw_ref, yat_ref, ycv_ref, ycm_ref, x_ref, g1_ref, w_ref, lg_ref, lb_ref, o_ref):
    G = GROUP_W
    acc = None
    for i, ref in enumerate((yrw_ref, yat_ref, ycv_ref, ycm_ref)):
        part = jnp.dot(ref[0].astype(BF16), w_ref[i * G:(i + 1) * G, :], preferred_element_type=F32)
        acc = part if acc is None else acc + part
    z = alpha * x_ref[0] + (1.0 + g1_ref[0]) * acc
    o_ref[0] = _ln_rows(z, lg_ref[...], lb_ref[...])


def out_projection(ys, x, g1, w_bf16, ln_g, ln_b, alpha):
    bn, t, d = x.shape
    tm = min(256, t)
    rows = g1.shape[1]
    per_row = rows == t and t > 1
    mod_block = (1, tm, d) if per_row else (1, 1, d)
    mod_map = (lambda b, i: (b, i, 0)) if per_row else (lambda b, i: (b, 0, 0))
    tok = lambda w: pl.BlockSpec((1, tm, w), lambda b, i: (b, i, 0))
    return pl.pallas_call(
        functools.partial(_out_kernel, alpha),
        grid=(bn, t // tm),
        in_specs=[tok(GROUP_W)] * 4 + [tok(d), pl.BlockSpec(mod_block, mod_map),
                                       _full(w_bf16.shape), _full((1, d)), _full((1, d))],
        out_specs=tok(d),
        out_shape=jax.ShapeDtypeStruct((bn, t, d), F32),
        compiler_params=_params("parallel", "parallel"),
        name="out_proj_ln",
    )(*ys, x, g1, w_bf16, ln_g.reshape(1, d), ln_b.reshape(1, d))


_N_TOP = PEER_TOPK + 1


def _top_values(x, n, rows_out):
    rid = lax.broadcasted_iota(jnp.int32, (rows_out, x.shape[1]), 0)
    vals = jnp.full((rows_out, x.shape[1]), NEG_INF, F32)
    for i in range(n):
        mx = jnp.max(x, axis=0, keepdims=True)
        vals = jnp.where(rid == i, mx, vals)
        x = jnp.where(x == mx, NEG_INF, x)
    return vals


def _peer_a_kernel(x_ref, sc_ref, sh_ref, wqT_ref, keys_ref, h_ref, s1_ref, s2_ref, thr_ref):
    h = (x_ref[0] * (1.0 + sc_ref[0]) + sh_ref[0]).astype(BF16)
    h_ref[0] = h
    qT = _dot_nt(wqT_ref[...], h).astype(BF16)
    tm = h.shape[0]
    rid8 = lax.broadcasted_iota(jnp.int32, (PEER_HEADS, tm), 0)
    thr_all = jnp.zeros((PEER_HEADS, tm), F32)
    for hh in range(PEER_HEADS):
        r0 = hh * PEER_QDIM
        s1 = jnp.dot(keys_ref[2 * hh], qT[r0:r0 + PEER_HALF], preferred_element_type=F32)
        s2 = jnp.dot(keys_ref[2 * hh + 1], qT[r0 + PEER_HALF:r0 + PEER_QDIM], preferred_element_type=F32)
        v1 = _top_values(s1, _N_TOP, 24)
        v2 = _top_values(s2, _N_TOP, 24)
        cands = ([v1[0:1] + v2, v2[0:1] + v1] + [v1[a:a + 1] + v2[0:8] for a in (1, 2, 3)]
                 + [v1[4:8] + v2[1:2], v1[4:8] + v2[2:3]])
        best = _top_values(jnp.concatenate(cands, axis=0), _N_TOP, 24)
        m = best[0:1]
        z = jnp.sum(jnp.exp(best[0:PEER_TOPK] - m), axis=0, keepdims=True)
        off = m + jnp.log(z)
        thr = 0.5 * (best[PEER_TOPK - 1:PEER_TOPK] + best[PEER_TOPK:PEER_TOPK + 1]) - off
        thr = thr * LOG2_E
        s1_ref[hh * PEER_KEYS:(hh + 1) * PEER_KEYS, :] = (s1 - off) * LOG2_E
        s2_ref[hh * PEER_KEYS:(hh + 1) * PEER_KEYS, :] = s2 * LOG2_E
        thr_all = jnp.where(rid8 == hh, thr, thr_all)
    thr_ref[...] = thr_all


def peer_stage_a(x, sc, sh, wqT_bf16, keys_bf16):
    bn, t, d = x.shape
    n = bn * t
    tm = min(256, t)
    rows = sc.shape[1]
    per_row = rows == t and t > 1
    mod_block = (1, tm, d) if per_row else (1, 1, d)
    mod_map = (lambda b, i: (b, i, 0)) if per_row else (lambda b, i: (b, 0, 0))
    tpb = t // tm
    col = lambda r: pl.BlockSpec((r, tm), lambda b, i: (0, b * tpb + i))
    nk = PEER_HEADS * PEER_KEYS
    return pl.pallas_call(
        _peer_a_kernel,
        grid=(bn, tpb),
        in_specs=[pl.BlockSpec((1, tm, d), lambda b, i: (b, i, 0)),
                  pl.BlockSpec(mod_block, mod_map), pl.BlockSpec(mod_block, mod_map),
                  _full(wqT_bf16.shape), _full(keys_bf16.shape)],
        out_specs=[pl.BlockSpec((1, tm, d), lambda b, i: (b, i, 0)), col(nk), col(nk), col(PEER_HEADS)],
        out_shape=[jax.ShapeDtypeStruct((bn, t, d), BF16),
                   jax.ShapeDtypeStruct((nk, n), F32), jax.ShapeDtypeStruct((nk, n), F32),
                   jax.ShapeDtypeStruct((PEER_HEADS, n), F32)],
        compiler_params=_params("parallel", "parallel"),
        name="peer_scores",
    )(x, sc, sh, wqT_bf16, keys_bf16)


def _peer_b_kernel(alpha, te, h_ref, u_ref, vT_ref, s1_ref, s2_ref, thr_ref, x_ref, g2_ref, lg_ref, lb_ref,
                   o_ref, acc_ref, w_ref, *a_refs):
    e = pl.program_id(2)
    n_tiles = pl.num_programs(2) - 1
    tm = w_ref.shape[2]
    slot = lax.rem(e, 2)

    @pl.when(e == 0)
    def _():
        acc_ref[...] = jnp.zeros(acc_ref.shape, F32)
        w_ref[1] = jnp.zeros(w_ref.shape[1:], BF16)

    def drain():
        acc_ref[...] += jnp.dot(vT_ref[0], w_ref[1 - slot], preferred_element_type=F32)

    groups = te // PEER_KEYS
    assert groups == 8
    part_groups = groups // len(a_refs)
    part_rows = part_groups * PEER_KEYS

    @pl.when(e < n_tiles)
    def _():
        row0 = [pl.multiple_of(hh * PEER_KEYS + e * groups, 8) for hh in range(PEER_HEADS)]
        for q, a_ref in enumerate(a_refs):
            a_ref[...] = _dot_nt(u_ref[q * part_rows:(q + 1) * part_rows, :], h_ref[0])
            tl = min(128, tm)
            for c in range(tm // tl):
                lanes = slice(c * tl, (c + 1) * tl)
                s1_rows = [s1_ref[pl.ds(row0[hh], groups), lanes] for hh in range(PEER_HEADS)]
                thr = thr_ref[:, lanes]
                for gl in range(part_groups):
                    gi = q * part_groups + gl
                    gate = None
                    for hh in range(PEER_HEADS):
                        t = s1_rows[hh][gi:gi + 1, :] + s2_ref[hh * PEER_KEYS:(hh + 1) * PEER_KEYS, lanes]
                        part = jnp.where(t >= thr[hh:hh + 1, :], jnp.exp2(t), 0.0)
                        gate = part if gate is None else gate + part
                    act = _gelu(a_ref[gl * PEER_KEYS:(gl + 1) * PEER_KEYS, lanes])
                    w_ref[slot, gi * PEER_KEYS:(gi + 1) * PEER_KEYS, lanes] = (gate * act).astype(BF16)
        drain()

    @pl.when(e == n_tiles)
    def _():
        drain()
        z = alpha * x_ref[0] + (1.0 + g2_ref[0]) * acc_ref[...].T
        o_ref[0] = _ln_rows(z, lg_ref[...], lb_ref[...])


def peer_stage_b(h_bf16, u_bf16, vT_bf16, s1, s2, thr, x, g2, ln_g, ln_b, alpha):
    bn, t, d = x.shape
    tm = min(512, t)
    te = PEER_TILE
    n_tiles = u_bf16.shape[0] // te
    rows = g2.shape[1]
    per_row = rows == t and t > 1
    mod_block = (1, tm, d) if per_row else (1, 1, d)
    mod_map = (lambda b, i, e: (b, i, 0)) if per_row else (lambda b, i, e: (b, 0, 0))
    tpb = t // tm
    tok = lambda: pl.BlockSpec((1, tm, d), lambda b, i, e: (b, i, 0))
    col = lambda r: pl.BlockSpec((r, tm), lambda b, i, e: (0, b * tpb + i))
    nk = PEER_HEADS * PEER_KEYS
    return pl.pallas_call(
        functools.partial(_peer_b_kernel, alpha, te),
        grid=(bn, tpb, n_tiles + 1),
        in_specs=[tok(),
                  pl.BlockSpec((te, d), lambda b, i, e: (jnp.minimum(e, n_tiles - 1), 0)),
                  pl.BlockSpec((1, d, te), lambda b, i, e: (jnp.maximum(e - 1, 0), 0, 0)),
                  col(nk), col(nk), col(PEER_HEADS),
                  tok(), pl.BlockSpec(mod_block, mod_map), _full((1, d)), _full((1, d))],
        out_specs=tok(),
        out_shape=jax.ShapeDtypeStruct((bn, t, d), F32),
        scratch_shapes=[pltpu.VMEM((d, tm), F32), pltpu.VMEM((2, te, tm), BF16)]
                       + [pltpu.VMEM((te // PEER_B_PARTS, tm), F32) for _ in range(PEER_B_PARTS)],
        compiler_params=_params("parallel", "parallel", "arbitrary"),
        name="peer_experts",
    )(h_bf16, u_bf16, vT_bf16, s1, s2, thr, x, g2, ln_g.reshape(1, d), ln_b.reshape(1, d))


def _pad_time(a, tp):
    return a if a.shape[1] == tp else jnp.pad(a, ((0, 0), (0, tp - a.shape[1]), (0, 0)))


def _forward_group(x, mods, rw_s0, shift0, conv0, attend, W, Wc, per_row, rw_bk):
    bn, t, d = x.shape
    depth = W['w_mix'].shape[0]
    alpha = (2.0 * depth) ** 0.25
    tp = -(-t // TIME_CHUNK) * TIME_CHUNK
    x = layer_norm_rows(x.reshape(bn * t, d), W['ln_in_g'], W['ln_in_b']).reshape(bn, t, d)
    if per_row:
        x = x.reshape(1, bn * t, d)
    ks, vs, rws, shs, cvs, cms = [], [], [], [], [], []
    for l in range(depth):
        wl = {name: W[name][l] for name in W if name not in ('ln_in_g', 'ln_in_b')}
        m = mods[l]
        if per_row:
            mv = [jnp.repeat(m[:, i], t, axis=0)[None] for i in range(6)]
        else:
            mv = [m[:, i][:, None, :] for i in range(6)]
        sh1, sc1, g1, sh2, sc2, g2 = mv
        p_rw, q, k, v, p_cv, p_cm = mix_projection(x, sc1, sh1, Wc['w_mix'][l])
        unflat = lambda a: a.reshape(bn, t, a.shape[-1])
        p_rw, q, k, v, p_cv, p_cm = map(unflat, (p_rw, q, k, v, p_cv, p_cm))
        y_rw, s_rw = rwkv_mixer(_pad_time(p_rw, tp), shift0[l], rw_s0[l], min(t, TIME_CHUNK) if tp == TIME_CHUNK
                                else TIME_CHUNK, rw_bk, wl)
        y_at = attend(l, q, k, v)
        y_cv, y_cm, v_rows, cv_st = local_mixers(_pad_time(p_cv, tp), _pad_time(p_cm, tp), conv0[l], t, wl)
        flat = (lambda a: a[:, :t].reshape(1, bn * t, -1)) if per_row else (lambda a: a[:, :t])
        ys = [flat(y_rw), flat(y_at), flat(y_cv), flat(y_cm)]
        x = out_projection(ys, x, g1, Wc['w_out'][l], wl['ln1_g'], wl['ln1_b'], alpha)
        h_bf, s1, s2, thr = peer_stage_a(x, sc2, sh2, Wc['peer_wqT'][l], Wc['peer_keys'][l])
        x = peer_stage_b(h_bf, Wc['peer_u'][l], Wc['peer_vT'][l], s1, s2, thr, x, g2,
                         wl['ln2_g'], wl['ln2_b'], alpha)
        ks.append(k.reshape(bn, t, N_HEADS, HEAD_DIM))
        vs.append(v.reshape(bn, t, N_HEADS, HEAD_DIM))
        rws.append(s_rw)
        shs.append(p_rw[:, t - 1])
        cvs.append(cv_st)
        cms.append(v_rows[:, :t])
    return (x.reshape(bn, t, d), jnp.stack(ks), jnp.stack(vs), jnp.stack(rws), jnp.stack(shs),
            jnp.stack(cvs), jnp.stack(cms))


def kernel(x_prompt, x_sample, cache_k, cache_v, state_rwkv, state_shift, state_conv, page_table, c_prompt, c_sample, ln_in_g, ln_in_b, w_ada, b_ada, w_mix, rw_mu, rw_w0, rw_w2, rw_a0, rw_a2, rw_g2, rw_kk, rw_ka, rw_rk, rw_lnx_g, rw_lnx_b, conv_w, cm_ln_g, cm_ln_b, cm_ws, cm_bs, w_out, ln1_g, ln1_b, ln2_g, ln2_b, peer_wq, peer_keys, peer_u, peer_v):
    W = {'ln_in_g': ln_in_g, 'ln_in_b': ln_in_b, 'w_mix': w_mix,
         'rw_mu': rw_mu, 'rw_w0': rw_w0, 'rw_w2': rw_w2, 'rw_a0': rw_a0, 'rw_a2': rw_a2,
         'rw_g2': rw_g2, 'rw_kk': rw_kk, 'rw_ka': rw_ka, 'rw_rk': rw_rk, 'rw_lnx_g': rw_lnx_g,
         'rw_lnx_b': rw_lnx_b, 'conv_w': conv_w, 'cm_ln_g': cm_ln_g, 'cm_ln_b': cm_ln_b,
         'cm_ws': cm_ws, 'cm_bs': cm_bs, 'ln1_g': ln1_g, 'ln1_b': ln1_b,
         'ln2_g': ln2_g, 'ln2_b': ln2_b}
    depth = w_mix.shape[0]
    Wc = {'w_mix': w_mix.astype(BF16), 'w_out': w_out.astype(BF16),
          'peer_wqT': jnp.swapaxes(peer_wq, 1, 2).astype(BF16),
          'peer_keys': peer_keys.reshape(depth, PEER_HEADS * 2, PEER_KEYS, PEER_HALF).astype(BF16),
          'peer_u': peer_u.astype(BF16),
          'peer_vT': peer_v.reshape(depth, -1, PEER_TILE, D_MODEL).transpose(0, 1, 3, 2).astype(BF16)}
    bp, dbn = x_prompt.shape[0], x_sample.shape[0]
    n_c = bp + dbn
    n_cp = -(-n_c // 8) * 8
    c_all = jnp.pad(jnp.concatenate([c_prompt, c_sample], axis=0), ((0, n_cp - n_c), (0, 0)))
    mods = ada_vectors(c_all, w_ada, b_ada).reshape(depth, n_cp, 6, D_MODEL)
    mods_p, mods_s = mods[:, :bp], mods[:, bp:n_c]
    dt = x_prompt.dtype
    z_rw = jnp.zeros((depth, bp, N_HEADS, HEAD_DIM, HEAD_DIM), dt)
    z_sh = jnp.zeros((depth, bp, RW_COLS), dt)
    z_cv = jnp.zeros((depth, bp, CONV_W - 1, GROUP_W), dt)
    pages = lambda c: c.transpose(0, 1, 3, 4, 2).reshape(c.shape[0], c.shape[1], GROUP_W, PAGE_SIZE)
    ck, cv = pages(cache_k), pages(cache_v)
    y_p, k_p, v_p, rw_p, sh_p, cv_p, _ = _forward_group(
        x_prompt, mods_p, z_rw, z_sh, z_cv, lambda l, q, k, v: moba_prompt(q, k, v), W, Wc,
        per_row=False, rw_bk=math.gcd(bp, 4))
    y_s, k_s, v_s, rw_s, sh_s, cv_s, cm_s = _forward_group(
        x_sample, mods_s, state_rwkv, state_shift, state_conv,
        lambda l, q, k, v: moba_sample(l, q, k, v, ck, cv, page_table), W, Wc,
        per_row=True, rw_bk=math.gcd(dbn, 4))
    return (y_p, y_s, k_p, v_p, k_s, v_s, rw_p, rw_s, sh_p, sh_s, cv_p, cv_s, cm_s)
```

```python
import functools
import math

import jax
import jax.numpy as jnp
from jax import lax
from jax.experimental import pallas as pl
from jax.experimental.pallas import tpu as pltpu

F32 = jnp.float32
BF16 = jnp.bfloat16
HI = lax.Precision.HIGHEST

D_MODEL = 1024
N_MIXERS = 4
GROUP_W = D_MODEL // N_MIXERS
HEAD_DIM = 64
N_HEADS = GROUP_W // HEAD_DIM
RW_W_RANK = 32
RW_A_RANK = 32
RW_G_RANK = 64
RW_COLS = 3 * GROUP_W + RW_W_RANK + RW_A_RANK + RW_G_RANK
MOBA_COLS = 3 * GROUP_W
CONV_COLS = 3 * GROUP_W
CMLP_COLS = 2 * GROUP_W
N_COLS = RW_COLS + MOBA_COLS + CONV_COLS + CMLP_COLS
MOBA_BLOCK = 256
MOBA_TOPK = 3
Q_BLOCK = 128
PAGE_SIZE = 128
CONV_W = 3
CHUNK = 128
PEER_KEYS = 128
PEER_EXPERTS = PEER_KEYS * PEER_KEYS
PEER_HEADS = 8
PEER_TOPK = 16
PEER_QDIM = 256
PEER_HALF = PEER_QDIM // 2
LN_EPS = 1e-5
GN_EPS = 64e-5
NEG_INF = float("-inf")
LOG2_E = 1.4426950408889634

VMEM_LIMIT = 48 * 1024 * 1024
TIME_CHUNK = 128
PAGE_UNROLL = 4
PEER_TILE = 8 * PEER_KEYS
PEER_B_PARTS = 4
RW_SUB = 64


def _params(*sem):
    return pltpu.CompilerParams(dimension_semantics=sem, vmem_limit_bytes=VMEM_LIMIT)


def _full(shape):
    n = len(shape)
    return pl.BlockSpec(shape, lambda *_: (0,) * n)


def _sigmoid(x):
    return 1.0 / (1.0 + jnp.exp(-x))


def _gelu(x):
    c = 0.7978845608028654
    return x * (0.5 + 0.5 * jnp.tanh(x * (c + (c * 0.044715) * (x * x))))


def _ln_rows(x, g, b):
    mu = jnp.mean(x, axis=-1, keepdims=True)
    d = x - mu
    var = jnp.mean(d * d, axis=-1, keepdims=True)
    return d * lax.rsqrt(var + LN_EPS) * g + b


def _dot_hi(a, b):
    return jnp.dot(a, b, precision=HI, preferred_element_type=F32)


def _dot_nt(a, b):
    return lax.dot_general(a, b, (((1,), (1,)), ((), ())), preferred_element_type=F32)


def _ln_kernel(x_ref, g_ref, b_ref, o_ref):
    o_ref[...] = _ln_rows(x_ref[...], g_ref[...], b_ref[...])


def layer_norm_rows(x2d, g, b):
    n, d = x2d.shape
    tm = min(512, n)
    return pl.pallas_call(
        _ln_kernel,
        grid=(n // tm,),
        in_specs=[pl.BlockSpec((tm, d), lambda i: (i, 0)), _full((1, d)), _full((1, d))],
        out_specs=pl.BlockSpec((tm, d), lambda i: (i, 0)),
        out_shape=jax.ShapeDtypeStruct((n, d), F32),
        compiler_params=_params("parallel"),
        name="ln_in",
    )(x2d, g.reshape(1, d), b.reshape(1, d))


def _ada_kernel(c_ref, w_ref, b_ref, o_ref):
    c = c_ref[...]
    s = (c * _sigmoid(c)).astype(BF16)
    o_ref[0] = jnp.dot(s, w_ref[0].astype(BF16), preferred_element_type=F32) + b_ref[0]


def ada_vectors(c_all, w_ada, b_ada):
    depth, d, n6 = w_ada.shape
    m = c_all.shape[0]
    tn = 1536
    return pl.pallas_call(
        _ada_kernel,
        grid=(depth, n6 // tn),
        in_specs=[_full((m, d)),
                  pl.BlockSpec((1, d, tn), lambda l, j: (l, 0, j)),
                  pl.BlockSpec((1, 1, tn), lambda l, j: (l, 0, j))],
        out_specs=pl.BlockSpec((1, m, tn), lambda l, j: (l, 0, j)),
        out_shape=jax.ShapeDtypeStruct((depth, m, n6), F32),
        compiler_params=_params("parallel", "parallel"),
        name="ada",
    )(c_all, w_ada, b_ada.reshape(depth, 1, n6))


_MIX_WIDTHS = (RW_COLS, GROUP_W, GROUP_W, GROUP_W, CONV_COLS, CMLP_COLS)


def _mix_kernel(x_ref, sc_ref, sh_ref, w_ref, *out_refs):
    h = (x_ref[0] * (1.0 + sc_ref[0]) + sh_ref[0]).astype(BF16)
    off = 0
    for ref, width in zip(out_refs, _MIX_WIDTHS):
        ref[0] = jnp.dot(h, w_ref[:, off:off + width], preferred_element_type=F32)
        off += width


def mix_projection(x, sc, sh, w_bf16):
    bn, t, d = x.shape
    tm = min(256, t)
    rows = sc.shape[1]
    per_row = rows == t and t > 1
    mod_block = (1, tm, d) if per_row else (1, 1, d)
    mod_map = (lambda b, i: (b, i, 0)) if per_row else (lambda b, i: (b, 0, 0))
    out_shape = [jax.ShapeDtypeStruct((bn, t, w), F32) for w in _MIX_WIDTHS]
    out_specs = [pl.BlockSpec((1, tm, w), lambda b, i: (b, i, 0)) for w in _MIX_WIDTHS]
    return pl.pallas_call(
        _mix_kernel,
        grid=(bn, t // tm),
        in_specs=[pl.BlockSpec((1, tm, d), lambda b, i: (b, i, 0)),
                  pl.BlockSpec(mod_block, mod_map),
                  pl.BlockSpec(mod_block, mod_map),
                  _full(w_bf16.shape)],
        out_specs=out_specs,
        out_shape=out_shape,
        compiler_params=_params("parallel", "parallel"),
        name="mix_proj",
    )(x, sc, sh, w_bf16)


def _softplus(x):
    return jnp.maximum(x, 0.0) + jnp.log(1.0 + jnp.exp(-jnp.abs(x)))


def _mm(a, b):
    return jnp.dot(a.astype(BF16), b.astype(BF16), preferred_element_type=F32)


def _dot_tn(a, b):
    return lax.dot_general(a, b, (((0,), (0,)), ((), ())), preferred_element_type=F32)


def _rwkv_subchunk(state, lw, kk, bb, k2, r, v):
    G = GROUP_W
    C = RW_SUB
    ri = lax.broadcasted_iota(jnp.int32, (C, C), 0)
    cj = lax.broadcasted_iota(jnp.int32, (C, C), 1)
    cum = _dot_hi(jnp.where(cj <= ri, 1.0, 0.0), lw)
    g_in = jnp.exp(cum)
    g_out = jnp.exp(-cum)
    at = -kk * jnp.exp(cum - lw)
    g_end = g_in[C - 1:C, :]
    lane_head = lax.broadcasted_iota(jnp.int32, (1, G), 1) // HEAD_DIM

    def stack(x):
        return jnp.concatenate([jnp.where(lane_head == h, x, 0.0) for h in range(N_HEADS)], axis=0)

    a_s, b_s, k_s, r_s, v_s = [stack(x) for x in (at, bb * g_out, k2 * g_out, r * g_in, v)]
    a_b, b_b, k_b, r_b, v_b = [x.astype(BF16) for x in (a_s, b_s, k_s, r_s, v_s)]
    s_b = state.astype(BF16)
    row = lax.broadcasted_iota(jnp.int32, (G, G), 0)
    col = lax.broadcasted_iota(jnp.int32, (G, G), 1)
    rt, ct = row % C, col % C
    strict = ct < rt
    incl = ct <= rt
    lmat = jnp.where(strict, _dot_nt(a_b, b_b), 0.0)
    a_ak = jnp.where(strict, _dot_nt(a_b, k_b), 0.0)
    a_rb = jnp.where(incl, _dot_nt(r_b, b_b), 0.0)
    a_rk = jnp.where(incl, _dot_nt(r_b, k_b), 0.0)
    rhs = _dot_nt(a_b, s_b) + _mm(a_ak, v_b)
    base = 8
    l1 = jnp.where(row // base == col // base, lmat, 0.0)
    l2 = _mm(l1, l1)
    n = l1 + l2 + _mm(l1, l2)
    l4 = _mm(l2, l2)
    n = n + l4 + _mm(n, l4)
    m = base
    while m < C:
        lower_left = jnp.logical_and(row // (2 * m) == col // (2 * m),
                                     jnp.logical_and((row // m) % 2 == 1, (col // m) % 2 == 0))
        lm = jnp.where(lower_left, lmat, 0.0)
        t1 = lm + _mm(n, lm)
        n = n + t1 + _mm(t1, n)
        m *= 2
    u_s = rhs + _mm(n, rhs)
    u_b = u_s.astype(BF16)
    y_s = _dot_nt(r_b, s_b) + _mm(a_rb, u_b) + _mm(a_rk, v_b)
    y = y_s[0:C] + y_s[C:2 * C] + y_s[2 * C:3 * C] + y_s[3 * C:4 * C]
    new_state = (state * g_end + _dot_tn(u_b, (b_s * g_end).astype(BF16))
                 + _dot_tn(v_b, (k_s * g_end).astype(BF16)))
    return y, new_state


def _rwkv_kernel(n_valid, bk,
                 p_ref, shift_ref, s0_ref, bd_ref, mu_ref, w0_ref, w2_ref, a0_ref, a2_ref, g2_ref,
                 kkp_ref, kap_ref, rk_ref, lng_ref, lnb_ref,
                 y_ref, s_ref, carry_ref):
    ci = pl.program_id(1)
    G = GROUP_W
    bd = bd_ref[...]

    @pl.when(ci == 0)
    def _():
        s_ref[...] = s0_ref[...]
        carry_ref[...] = shift_ref[...]

    row_id = lax.broadcasted_iota(jnp.int32, (TIME_CHUNK, 1), 0)
    sub_row = lax.broadcasted_iota(jnp.int32, (RW_SUB, 1), 0)

    for b in range(bk):
        p = p_ref[b]
        prev = jnp.where(row_id == 0, carry_ref[b], pltpu.roll(p, 1, axis=0))
        carry_ref[b] = p[TIME_CHUNK - 1:TIME_CHUNK, :]
        xm = p + (prev - p) * mu_ref[...]
        r, k, v = xm[:, :G], xm[:, G:2 * G], xm[:, 2 * G:3 * G]
        o = 3 * G
        wl = xm[:, o:o + RW_W_RANK]
        al = xm[:, o + RW_W_RANK:o + RW_W_RANK + RW_A_RANK]
        gl = xm[:, o + RW_W_RANK + RW_A_RANK:]
        w = -_softplus(-(w0_ref[...] + _dot_hi(jnp.tanh(wl), w2_ref[...]))) - 0.5
        lw = -jnp.exp(w)
        a = _sigmoid(a0_ref[...] + _dot_hi(al, a2_ref[...]))
        g = _dot_hi(_sigmoid(gl), g2_ref[...])
        kk = k * kkp_ref[...]
        kk = kk / jnp.maximum(jnp.sqrt(_dot_hi(kk * kk, bd)), 1e-12)
        k2 = k * (1.0 + (a - 1.0) * kap_ref[...])
        bb = kk * a
        bonus = _dot_hi(r * k2 * rk_ref[...], bd) * v

        state = s_ref[b]
        ys = []
        for j in range(TIME_CHUNK // RW_SUB):
            n_j = max(0, min(RW_SUB, n_valid - RW_SUB * j))
            rows = slice(j * RW_SUB, (j + 1) * RW_SUB)
            if n_j == 0:
                ys.append(jnp.zeros((RW_SUB, G), F32))
                continue
            lw_j, kk_j, bb_j, k2_j = lw[rows], kk[rows], bb[rows], k2[rows]
            if n_j < RW_SUB:
                live = sub_row < n_j
                lw_j, kk_j, bb_j, k2_j = [jnp.where(live, x, 0.0) for x in (lw_j, kk_j, bb_j, k2_j)]
            y_j, state = _rwkv_subchunk(state, lw_j, kk_j, bb_j, k2_j, r[rows], v[rows])
            ys.append(y_j)
        s_ref[b] = state
        y = jnp.concatenate(ys, axis=0)

        mean = _dot_hi(y, bd) * (1.0 / HEAD_DIM)
        d = y - mean
        var = _dot_hi(d * d, bd) * (1.0 / HEAD_DIM)
        yn = d * lax.rsqrt(var + GN_EPS) * lng_ref[...] + lnb_ref[...]
        y_ref[b] = (yn + bonus) * g


def rwkv_mixer(p, shift_prev, s0, n_valid, bk, wl):
    bn, tp, _ = p.shape
    G = GROUP_W
    eye_h = jnp.eye(N_HEADS, dtype=s0.dtype)
    s0bd = (s0[:, :, :, None, :] * eye_h[None, :, None, :, None]).reshape(bn, G, G)
    head_of = jnp.arange(G) // HEAD_DIM
    bd = (head_of[:, None] == head_of[None, :]).astype(F32)
    row = lambda a: a.reshape(1, -1)
    consts = [bd, row(wl['rw_mu']), row(wl['rw_w0']), wl['rw_w2'], row(wl['rw_a0']), wl['rw_a2'], wl['rw_g2'],
              row(wl['rw_kk']), row(wl['rw_ka']), row(wl['rw_rk']), row(wl['rw_lnx_g']), row(wl['rw_lnx_b'])]
    y, s_fin = pl.pallas_call(
        functools.partial(_rwkv_kernel, n_valid, bk),
        grid=(bn // bk, tp // TIME_CHUNK),
        in_specs=[pl.BlockSpec((bk, TIME_CHUNK, RW_COLS), lambda b, c: (b, c, 0)),
                  pl.BlockSpec((bk, 1, RW_COLS), lambda b, c: (b, 0, 0)),
                  pl.BlockSpec((bk, G, G), lambda b, c: (b, 0, 0))]
                 + [_full(a.shape) for a in consts],
        out_specs=[pl.BlockSpec((bk, TIME_CHUNK, G), lambda b, c: (b, c, 0)),
                   pl.BlockSpec((bk, G, G), lambda b, c: (b, 0, 0))],
        out_shape=[jax.ShapeDtypeStruct((bn, tp, G), F32),
                   jax.ShapeDtypeStruct((bn, G, G), F32)],
        scratch_shapes=[pltpu.VMEM((bk, 1, RW_COLS), F32)],
        compiler_params=_params("parallel", "arbitrary"),
        name="rwkv7",
    )(p, shift_prev.reshape(bn, 1, RW_COLS), s0bd, *consts)
    s4 = s_fin.reshape(bn, N_HEADS, HEAD_DIM, N_HEADS, HEAD_DIM)
    return y, jnp.stack([s4[:, h, :, h, :] for h in range(N_HEADS)], axis=1)


def _local_kernel(t_valid, cv_ref, cm_ref, prev_ref, cw_ref, lg_ref, lb_ref, ws_ref, bias_ref,
                  ycv_ref, ycm_ref, vrow_ref, st_ref, carry_ref):
    ci = pl.program_id(1)
    G = GROUP_W
    TC = TIME_CHUNK

    @pl.when(ci == 0)
    def _():
        carry_ref[...] = prev_ref[0]

    pc = cv_ref[0]
    bg, cg, hv = pc[:, :G], pc[:, G:2 * G], pc[:, 2 * G:]
    u = cg * hv
    row_id = lax.broadcasted_iota(jnp.int32, (TC, 1), 0)
    prev0, prev1 = carry_ref[0:1, :], carry_ref[1:2, :]
    u1 = jnp.where(row_id == 0, prev1, pltpu.roll(u, 1, axis=0))
    u2 = jnp.where(row_id == 0, prev0, jnp.where(row_id == 1, prev1, pltpu.roll(u, 2, axis=0)))
    ycv_ref[0] = bg * (cw_ref[0:1, :] * u2 + cw_ref[1:2, :] * u1 + cw_ref[2:3, :] * u)
    carry_ref[...] = u[TC - 2:, :]
    last = (t_valid - 1) // TC
    r = t_valid - last * TC

    @pl.when(ci == last)
    def _():
        st_ref[0] = u[r - 2:r, :]

    pm = cm_ref[0]
    uu = _gelu(pm[:, :G])
    vv = _ln_rows(_gelu(pm[:, G:]), lg_ref[...], lb_ref[...])
    vrow_ref[0] = vv
    head_of = lax.broadcasted_iota(jnp.int32, (1, G), 1) // HEAD_DIM
    causal = lax.broadcasted_iota(jnp.int32, (TC, TC), 0) >= lax.broadcasted_iota(jnp.int32, (TC, TC), 1)
    mixed = bias_ref[...]
    for h in range(N_HEADS):
        wm = jnp.where(causal, ws_ref[h], 0.0).astype(BF16)
        mixed = mixed + jnp.dot(wm, jnp.where(head_of == h, vv, 0.0).astype(BF16), preferred_element_type=F32)
    ycm_ref[0] = uu * mixed


def local_mixers(p_cv, p_cm, conv_prev, t_valid, wl):
    bn, tp, _ = p_cv.shape
    G = GROUP_W
    bias = jnp.repeat(wl['cm_bs'].T, HEAD_DIM, axis=1)
    row = lambda a: a.reshape(1, -1)
    tok = lambda w: pl.BlockSpec((1, TIME_CHUNK, w), lambda b, c: (b, c, 0))
    return pl.pallas_call(
        functools.partial(_local_kernel, t_valid),
        grid=(bn, tp // TIME_CHUNK),
        in_specs=[tok(CONV_COLS), tok(CMLP_COLS),
                  pl.BlockSpec((1, CONV_W - 1, G), lambda b, c: (b, 0, 0)),
                  _full((CONV_W, G)), _full((1, G)), _full((1, G)),
                  _full((N_HEADS, CHUNK, CHUNK)), _full((CHUNK, G))],
        out_specs=[tok(G), tok(G), tok(G), pl.BlockSpec((1, CONV_W - 1, G), lambda b, c: (b, 0, 0))],
        out_shape=[jax.ShapeDtypeStruct((bn, tp, G), F32)] * 3
                  + [jax.ShapeDtypeStruct((bn, CONV_W - 1, G), F32)],
        scratch_shapes=[pltpu.VMEM((CONV_W - 1, G), F32)],
        compiler_params=_params("parallel", "arbitrary"),
        name="conv_gmlp",
    )(p_cv, p_cm, conv_prev, wl['conv_w'], row(wl['cm_ln_g']), row(wl['cm_ln_b']), wl['cm_ws'], bias)


def _rank_rows(g, n_rows):
    rid = lax.broadcasted_iota(jnp.int32, g.shape, 0)
    rank = jnp.zeros(g.shape, jnp.int32)
    for j in range(n_rows):
        gj = g[j:j + 1, :]
        beats = jnp.where(gj > g, 1, jnp.where(jnp.logical_and(gj == g, rid > j), 1, 0))
        rank = rank + beats
    return rank


def _block_mean_kernel(k_ref, o_ref):
    o_ref[0, 0] = jnp.mean(k_ref[0], axis=0, keepdims=True)


def _moba_prompt_kernel(n_blk, n_sel, qT_ref, k_ref, vT_ref, kmean_ref, o_ref, sel_ref, s_ref):
    qi = pl.program_id(1)
    own = qi // (MOBA_BLOCK // Q_BLOCK)
    own_start = pl.multiple_of(own * MOBA_BLOCK, MOBA_BLOCK)
    kpos = own_start + lax.broadcasted_iota(jnp.int32, (MOBA_BLOCK, Q_BLOCK), 0)
    qpos = qi * Q_BLOCK + lax.broadcasted_iota(jnp.int32, (MOBA_BLOCK, Q_BLOCK), 1)
    causal = kpos <= qpos
    blk = lax.broadcasted_iota(jnp.int32, (n_blk, Q_BLOCK), 0)
    valid = blk < own

    def attend(h, start, qb, keep):
        s = jnp.dot(k_ref[0, h, pl.ds(start, MOBA_BLOCK), :], qb, preferred_element_type=F32)
        return jnp.where(keep, s, NEG_INF)

    def values(h, start, p):
        return jnp.dot(vT_ref[0, h, :, pl.ds(start, MOBA_BLOCK)], p.astype(BF16), preferred_element_type=F32)

    def fold8(s):
        return jnp.max(s.reshape(MOBA_BLOCK // 8, 8, Q_BLOCK), axis=0)

    qbs, peaks = [], []
    for h in range(N_HEADS):
        qT = qT_ref[0, h]
        gate = jnp.where(valid, _dot_hi(kmean_ref[0, h], qT), NEG_INF)
        sel = jnp.where(jnp.logical_and(valid, _rank_rows(gate, n_blk) < n_sel), 1.0, 0.0)
        for j in range(n_blk):
            sel_ref[h, j] = jnp.broadcast_to(sel[j:j + 1, :], (8, Q_BLOCK))
        qb = (qT * (HEAD_DIM ** -0.5 * LOG2_E)).astype(BF16)
        s = attend(h, own_start, qb, causal)
        s_ref[h, own] = s
        qbs.append(qb)
        peaks.append(fold8(s))

    def pass_a(j, peaks):
        start = pl.multiple_of(j * MOBA_BLOCK, MOBA_BLOCK)
        out = []
        for h in range(N_HEADS):
            s = attend(h, start, qbs[h], sel_ref[h, j][0:1, :] > 0.0)
            s_ref[h, j] = s
            out.append(jnp.maximum(peaks[h], fold8(s)))
        return tuple(out)

    peaks = lax.fori_loop(0, own, pass_a, tuple(peaks))
    ms = [jnp.max(pk, axis=0, keepdims=True) for pk in peaks]

    def pass_b(j, accs):
        start = pl.multiple_of(j * MOBA_BLOCK, MOBA_BLOCK)
        return tuple(accs[h] + values(h, start, jnp.exp2(s_ref[h, j] - ms[h])) for h in range(N_HEADS))

    rows = vT_ref.shape[2]
    accs = lax.fori_loop(0, own + 1, pass_b, tuple(jnp.zeros((rows, Q_BLOCK), F32) for _ in range(N_HEADS)))
    for h in range(N_HEADS):
        o_ref[0, h] = accs[h][:HEAD_DIM] / accs[h][HEAD_DIM:HEAD_DIM + 1]


def moba_prompt(q, k, v):
    bn, t, G = q.shape
    n_blk = t // MOBA_BLOCK
    n_sel = min(MOBA_TOPK, n_blk - 1)
    kmean = pl.pallas_call(
        _block_mean_kernel,
        grid=(bn, n_blk),
        in_specs=[pl.BlockSpec((1, MOBA_BLOCK, G), lambda b, j: (b, j, 0))],
        out_specs=pl.BlockSpec((1, 1, 1, G), lambda b, j: (b, j, 0, 0)),
        out_shape=jax.ShapeDtypeStruct((bn, n_blk, 1, G), F32),
        compiler_params=_params("parallel", "parallel"),
        name="moba_block_means",
    )(k)
    heads = lambda a: a.reshape(bn, -1, N_HEADS, HEAD_DIM)
    kmean = heads(kmean).transpose(0, 2, 1, 3)
    qT = heads(q).transpose(0, 2, 3, 1)
    kh = heads(k).transpose(0, 2, 1, 3).astype(BF16)
    vT = heads(v).transpose(0, 2, 3, 1).astype(BF16)
    v_rows = HEAD_DIM + 16
    vT = jnp.concatenate([vT, jnp.ones((bn, N_HEADS, 1, t), BF16),
                          jnp.zeros((bn, N_HEADS, v_rows - HEAD_DIM - 1, t), BF16)], axis=2)
    hq = lambda rows, cols: pl.BlockSpec((1, N_HEADS, rows, cols), lambda b, i: (b, 0, 0, 0))
    tile = pl.BlockSpec((1, N_HEADS, HEAD_DIM, Q_BLOCK), lambda b, i: (b, 0, 0, i))
    oT = pl.pallas_call(
        functools.partial(_moba_prompt_kernel, n_blk, n_sel),
        grid=(bn, t // Q_BLOCK),
        in_specs=[tile, hq(t, HEAD_DIM), hq(v_rows, t), hq(n_blk, HEAD_DIM)],
        out_specs=tile,
        out_shape=jax.ShapeDtypeStruct((bn, N_HEADS, HEAD_DIM, t), F32),
        scratch_shapes=[pltpu.VMEM((N_HEADS, n_blk, 8, Q_BLOCK), F32),
                        pltpu.VMEM((N_HEADS, n_blk, MOBA_BLOCK, Q_BLOCK), F32)],
        compiler_params=_params("parallel", "arbitrary"),
        name="moba_prompt",
    )(qT, kh, vT, kmean)
    return oT.transpose(0, 3, 1, 2).reshape(bn, t, GROUP_W)


def _rank_cols(g, n_cols):
    cid = lax.broadcasted_iota(jnp.int32, g.shape, 1)
    rank = jnp.zeros(g.shape, jnp.int32)
    for j in range(n_cols):
        gj = g[:, j:j + 1]
        rank = rank + jnp.where(gj > g, 1, jnp.where(jnp.logical_and(gj == g, cid > j), 1, 0))
    return rank


def _moba_decode_kernel(layer, n_pages, n_sel, t_new, pt_ref, q_ref, kn_ref, vn_ref, ck_hbm, cv_hbm, o_ref,
                        kbuf, vbuf, s_ref, ksem, vsem):
    b = pl.program_id(0)
    nb = pl.num_programs(0)
    slot = lax.rem(b, 2)
    G = GROUP_W
    R = N_HEADS * t_new
    scale = HEAD_DIM ** -0.5
    pages_per_blk = MOBA_BLOCK // PAGE_SIZE
    n_pf = n_pages // pages_per_blk

    def k_copy(row, p, sl):
        return pltpu.make_async_copy(ck_hbm.at[layer, pt_ref[row, p]], kbuf.at[sl, p], ksem.at[sl])

    def v_copy(p):
        return pltpu.make_async_copy(cv_hbm.at[layer, pt_ref[b, p]], vbuf.at[p], vsem.at[0])

    def for_pages(fn):
        def body(p, c):
            fn(p)
            return c
        lax.fori_loop(0, n_pages, body, 0, unroll=PAGE_UNROLL)

    @pl.when(b == 0)
    def _():
        for_pages(lambda p: k_copy(0, p, 0).start())

    for_pages(lambda p: v_copy(p).start())

    @pl.when(b + 1 < nb)
    def _():
        for_pages(lambda p: k_copy(b + 1, p, 1 - slot).start())

    for_pages(lambda p: k_copy(b, p, slot).wait())

    q = q_ref[0]
    row_head = lax.broadcasted_iota(jnp.int32, (R, G), 0) // t_new
    lane_head = lax.broadcasted_iota(jnp.int32, (R, G), 1) // HEAD_DIM
    own_head = row_head == lane_head
    qbd = jnp.where(own_head, jnp.concatenate([q] * N_HEADS, axis=0), 0.0)
    qb = qbd.astype(BF16)
    blk_lane = lax.broadcasted_iota(jnp.int32, (1, n_pf), 1)

    def pass1(p, ksum_t):
        k_t = kbuf[slot, p]
        s_ref[p] = jnp.dot(qb, k_t.astype(BF16), preferred_element_type=F32) * scale
        return ksum_t + jnp.where(blk_lane == p // pages_per_blk, jnp.sum(k_t, axis=-1, keepdims=True), 0.0)

    ksum_t = lax.fori_loop(0, n_pages, pass1, jnp.zeros((G, n_pf), F32), unroll=PAGE_UNROLL)
    gate = _dot_hi(qbd, ksum_t * (1.0 / MOBA_BLOCK))
    sel = jnp.where(_rank_cols(gate, n_pf) < n_sel, 1.0, 0.0)

    s_own = _dot_nt(qb, kn_ref[0].astype(BF16)) * scale
    tq = lax.broadcasted_iota(jnp.int32, s_own.shape, 0) % t_new
    tk = lax.broadcasted_iota(jnp.int32, s_own.shape, 1)
    s_own = jnp.where(tk <= tq, s_own, NEG_INF)

    def pass2(p, m):
        chosen = jnp.sum(jnp.where(blk_lane == p // pages_per_blk, sel, 0.0), axis=-1, keepdims=True)
        s = jnp.where(chosen > 0.0, s_ref[p], NEG_INF)
        s_ref[p] = s
        return jnp.maximum(m, jnp.max(s, axis=-1, keepdims=True))

    m = lax.fori_loop(0, n_pages, pass2, jnp.max(s_own, axis=-1, keepdims=True), unroll=PAGE_UNROLL)

    for_pages(lambda p: v_copy(p).wait())

    def pass3(p, carry):
        l, acc = carry
        pr = jnp.exp(s_ref[p] - m)
        return (l + jnp.sum(pr, axis=-1, keepdims=True),
                acc + _dot_nt(pr.astype(BF16), vbuf[p].astype(BF16)))

    p_own = jnp.exp(s_own - m)
    l, acc = lax.fori_loop(
        0, n_pages, pass3,
        (jnp.sum(p_own, axis=-1, keepdims=True),
         jnp.dot(p_own.astype(BF16), vn_ref[0].astype(BF16), preferred_element_type=F32)),
        unroll=PAGE_UNROLL)
    o = jnp.where(own_head, acc / l, 0.0)
    out = o[0:t_new]
    for h in range(1, N_HEADS):
        out = out + o[h * t_new:(h + 1) * t_new]
    o_ref[0] = out


def moba_sample(layer, q, k, v, cache_k, cache_v, page_table):
    db, t_new, G = q.shape
    n_pages = page_table.shape[1]
    past = n_pages * PAGE_SIZE
    assert past % MOBA_BLOCK == 0, "decode kernel expects the past to end on a MoBA block boundary"
    n_sel = min(MOBA_TOPK, past // MOBA_BLOCK)
    R = N_HEADS * t_new
    tok = pl.BlockSpec((1, t_new, G), lambda b, pt: (b, 0, 0))
    pool = pl.BlockSpec(memory_space=pl.ANY)
    return pl.pallas_call(
        functools.partial(_moba_decode_kernel, layer, n_pages, n_sel, t_new),
        grid_spec=pltpu.PrefetchScalarGridSpec(
            num_scalar_prefetch=1, grid=(db,),
            in_specs=[tok, tok, tok, pool, pool],
            out_specs=tok,
            scratch_shapes=[pltpu.VMEM((2, n_pages, G, PAGE_SIZE), F32),
                            pltpu.VMEM((n_pages, G, PAGE_SIZE), F32),
                            pltpu.VMEM((n_pages, R, PAGE_SIZE), F32),
                            pltpu.SemaphoreType.DMA((2,)), pltpu.SemaphoreType.DMA((1,))]),
        out_shape=jax.ShapeDtypeStruct((db, t_new, G), F32),
        compiler_params=_params("arbitrary"),
        name="moba_decode",
    )(page_table, q, k, v, cache_k, cache_v)


def _out_kernel(alpha, yrw_ref, yat_ref, ycv_ref, ycm_ref, x_ref, g1_ref, w_ref, lg_ref, lb_ref, o_ref):
    G = GROUP_W
    acc = None
    for i, ref in enumerate((yrw_ref, yat_ref, ycv_ref, ycm_ref)):
        part = jnp.dot(ref[0].astype(BF16), w_ref[i * G:(i + 1) * G, :], preferred_element_type=F32)
        acc = part if acc is None else acc + part
    z = alpha * x_ref[0] + (1.0 + g1_ref[0]) * acc
    o_ref[0] = _ln_rows(z, lg_ref[...], lb_ref[...])


def out_projection(ys, x, g1, w_bf16, ln_g, ln_b, alpha):
    bn, t, d = x.shape
    tm = min(256, t)
    rows = g1.shape[1]
    per_row = rows == t and t > 1
    mod_block = (1, tm, d) if per_row else (1, 1, d)
    mod_map = (lambda b, i: (b, i, 0)) if per_row else (lambda b, i: (b, 0, 0))
    tok = lambda w: pl.BlockSpec((1, tm, w), lambda b, i: (b, i, 0))
    return pl.pallas_call(
        functools.partial(_out_kernel, alpha),
        grid=(bn, t // tm),
        in_specs=[tok(GROUP_W)] * 4 + [tok(d), pl.BlockSpec(mod_block, mod_map),
                                       _full(w_bf16.shape), _full((1, d)), _full((1, d))],
        out_specs=tok(d),
        out_shape=jax.ShapeDtypeStruct((bn, t, d), F32),
        compiler_params=_params("parallel", "parallel"),
        name="out_proj_ln",
    )(*ys, x, g1, w_bf16, ln_g.reshape(1, d), ln_b.reshape(1, d))


_N_TOP = PEER_TOPK + 1


def _top_values(x, n, rows_out):
    rid = lax.broadcasted_iota(jnp.int32, (rows_out, x.shape[1]), 0)
    vals = jnp.full((rows_out, x.shape[1]), NEG_INF, F32)
    for i in range(n):
        mx = jnp.max(x, axis=0, keepdims=True)
        vals = jnp.where(rid == i, mx, vals)
        x = jnp.where(x == mx, NEG_INF, x)
    return vals


def _peer_a_kernel(x_ref, sc_ref, sh_ref, wqT_ref, keys_ref, h_ref, s1_ref, s2_ref, thr_ref):
    h = (x_ref[0] * (1.0 + sc_ref[0]) + sh_ref[0]).astype(BF16)
    h_ref[0] = h
    qT = _dot_nt(wqT_ref[...], h).astype(BF16)
    tm = h.shape[0]
    rid8 = lax.broadcasted_iota(jnp.int32, (PEER_HEADS, tm), 0)
    thr_all = jnp.zeros((PEER_HEADS, tm), F32)
    for hh in range(PEER_HEADS):
        r0 = hh * PEER_QDIM
        s1 = jnp.dot(keys_ref[2 * hh], qT[r0:r0 + PEER_HALF], preferred_element_type=F32)
        s2 = jnp.dot(keys_ref[2 * hh + 1], qT[r0 + PEER_HALF:r0 + PEER_QDIM], preferred_element_type=F32)
        v1 = _top_values(s1, _N_TOP, 24)
        v2 = _top_values(s2, _N_TOP, 24)
        cands = ([v1[0:1] + v2, v2[0:1] + v1] + [v1[a:a + 1] + v2[0:8] for a in (1, 2, 3)]
                 + [v1[4:8] + v2[1:2], v1[4:8] + v2[2:3]])
        best = _top_values(jnp.concatenate(cands, axis=0), _N_TOP, 24)
        m = best[0:1]
        z = jnp.sum(jnp.exp(best[0:PEER_TOPK] - m), axis=0, keepdims=True)
        off = m + jnp.log(z)
        thr = 0.5 * (best[PEER_TOPK - 1:PEER_TOPK] + best[PEER_TOPK:PEER_TOPK + 1]) - off
        thr = thr * LOG2_E
        s1_ref[hh * PEER_KEYS:(hh + 1) * PEER_KEYS, :] = (s1 - off) * LOG2_E
        s2_ref[hh * PEER_KEYS:(hh + 1) * PEER_KEYS, :] = s2 * LOG2_E
        thr_all = jnp.where(rid8 == hh, thr, thr_all)
    thr_ref[...] = thr_all


def peer_stage_a(x, sc, sh, wqT_bf16, keys_bf16):
    bn, t, d = x.shape
    n = bn * t
    tm = min(256, t)
    rows = sc.shape[1]
    per_row = rows == t and t > 1
    mod_block = (1, tm, d) if per_row else (1, 1, d)
    mod_map = (lambda b, i: (b, i, 0)) if per_row else (lambda b, i: (b, 0, 0))
    tpb = t // tm
    col = lambda r: pl.BlockSpec((r, tm), lambda b, i: (0, b * tpb + i))
    nk = PEER_HEADS * PEER_KEYS
    return pl.pallas_call(
        _peer_a_kernel,
        grid=(bn, tpb),
        in_specs=[pl.BlockSpec((1, tm, d), lambda b, i: (b, i, 0)),
                  pl.BlockSpec(mod_block, mod_map), pl.BlockSpec(mod_block, mod_map),
                  _full(wqT_bf16.shape), _full(keys_bf16.shape)],
        out_specs=[pl.BlockSpec((1, tm, d), lambda b, i: (b, i, 0)), col(nk), col(nk), col(PEER_HEADS)],
        out_shape=[jax.ShapeDtypeStruct((bn, t, d), BF16),
                   jax.ShapeDtypeStruct((nk, n), F32), jax.ShapeDtypeStruct((nk, n), F32),
                   jax.ShapeDtypeStruct((PEER_HEADS, n), F32)],
        compiler_params=_params("parallel", "parallel"),
        name="peer_scores",
    )(x, sc, sh, wqT_bf16, keys_bf16)


def _peer_b_kernel(alpha, te, h_ref, *refs):
    n_parts = PEER_B_PARTS
    u_refs, v_refs = refs[:n_parts], refs[n_parts:2 * n_parts]
    s1_ref, s2_ref, thr_ref, x_ref, g2_ref, lg_ref, lb_ref, o_ref, acc_ref, w_ref = refs[2 * n_parts:2 * n_parts + 10]
    a_refs = refs[2 * n_parts + 10:]
    e = pl.program_id(2)
    n_tiles = pl.num_programs(2) - 1
    tm = w_ref.shape[2]
    slot = lax.rem(e, 2)

    @pl.when(e == 0)
    def _():
        acc_ref[...] = jnp.zeros(acc_ref.shape, F32)
        w_ref[1] = jnp.zeros(w_ref.shape[1:], BF16)

    def drain():
        slab = acc_ref.shape[0] // n_parts
        for q, v_ref in enumerate(v_refs):
            acc_ref[q * slab:(q + 1) * slab, :] += jnp.dot(v_ref[0], w_ref[1 - slot], preferred_element_type=F32)

    groups = te // PEER_KEYS
    assert groups == 8
    part_groups = groups // len(a_refs)
    part_rows = part_groups * PEER_KEYS

    @pl.when(e < n_tiles)
    def _():
        row0 = [pl.multiple_of(hh * PEER_KEYS + e * groups, 8) for hh in range(PEER_HEADS)]
        for q, a_ref in enumerate(a_refs):
            a_ref[...] = _dot_nt(u_refs[q][...], h_ref[0])
            tl = min(128, tm)
            for c in range(tm // tl):
                lanes = slice(c * tl, (c + 1) * tl)
                s1_rows = [s1_ref[pl.ds(row0[hh], groups), lanes] for hh in range(PEER_HEADS)]
                thr = thr_ref[:, lanes]
                for gl in range(part_groups):
                    gi = q * part_groups + gl
                    gate = None
                    for hh in range(PEER_HEADS):
                        t = s1_rows[hh][gi:gi + 1, :] + s2_ref[hh * PEER_KEYS:(hh + 1) * PEER_KEYS, lanes]
                        part = jnp.where(t >= thr[hh:hh + 1, :], jnp.exp2(t), 0.0)
                        gate = part if gate is None else gate + part
                    act = _gelu(a_ref[gl * PEER_KEYS:(gl + 1) * PEER_KEYS, lanes])
                    w_ref[slot, gi * PEER_KEYS:(gi + 1) * PEER_KEYS, lanes] = (gate * act).astype(BF16)
        drain()

    @pl.when(e == n_tiles)
    def _():
        drain()
        z = alpha * x_ref[0] + (1.0 + g2_ref[0]) * acc_ref[...].T
        o_ref[0] = _ln_rows(z, lg_ref[...], lb_ref[...])


def peer_stage_b(h_bf16, u_bf16, vT_bf16, s1, s2, thr, x, g2, ln_g, ln_b, alpha):
    bn, t, d = x.shape
    tm = min(512, t)
    te = PEER_TILE
    n_tiles = u_bf16.shape[0] // te
    rows = g2.shape[1]
    per_row = rows == t and t > 1
    mod_block = (1, tm, d) if per_row else (1, 1, d)
    mod_map = (lambda b, i, e: (b, i, 0)) if per_row else (lambda b, i, e: (b, 0, 0))
    tpb = t // tm
    tok = lambda: pl.BlockSpec((1, tm, d), lambda b, i, e: (b, i, 0))
    col = lambda r: pl.BlockSpec((r, tm), lambda b, i, e: (0, b * tpb + i))
    nk = PEER_HEADS * PEER_KEYS
    n_parts = PEER_B_PARTS
    u_specs = [pl.BlockSpec((te // n_parts, d),
                            lambda b, i, e, q=q: (jnp.minimum(e, n_tiles - 1) * n_parts + q, 0))
               for q in range(n_parts)]
    v_specs = [pl.BlockSpec((1, d // n_parts, te), lambda b, i, e, q=q: (jnp.maximum(e - 1, 0), q, 0))
               for q in range(n_parts)]
    return pl.pallas_call(
        functools.partial(_peer_b_kernel, alpha, te),
        grid=(bn, tpb, n_tiles + 1),
        in_specs=[tok()] + u_specs + v_specs + [
                  col(nk), col(nk), col(PEER_HEADS),
                  tok(), pl.BlockSpec(mod_block, mod_map), _full((1, d)), _full((1, d))],
        out_specs=tok(),
        out_shape=jax.ShapeDtypeStruct((bn, t, d), F32),
        scratch_shapes=[pltpu.VMEM((d, tm), F32), pltpu.VMEM((2, te, tm), BF16)]
                       + [pltpu.VMEM((te // PEER_B_PARTS, tm), F32) for _ in range(PEER_B_PARTS)],
        compiler_params=_params("parallel", "parallel", "arbitrary"),
        name="peer_experts",
    )(h_bf16, *([u_bf16] * n_parts), *([vT_bf16] * n_parts), s1, s2, thr, x, g2,
      ln_g.reshape(1, d), ln_b.reshape(1, d))


def _pad_time(a, tp):
    return a if a.shape[1] == tp else jnp.pad(a, ((0, 0), (0, tp - a.shape[1]), (0, 0)))


def _forward_group(x, mods, rw_s0, shift0, conv0, attend, W, Wc, per_row, rw_bk):
    bn, t, d = x.shape
    depth = W['w_mix'].shape[0]
    alpha = (2.0 * depth) ** 0.25
    tp = -(-t // TIME_CHUNK) * TIME_CHUNK
    x = layer_norm_rows(x.reshape(bn * t, d), W['ln_in_g'], W['ln_in_b']).reshape(bn, t, d)
    if per_row:
        x = x.reshape(1, bn * t, d)
    ks, vs, rws, shs, cvs, cms = [], [], [], [], [], []
    for l in range(depth):
        wl = {name: W[name][l] for name in W if name not in ('ln_in_g', 'ln_in_b')}
        m = mods[l]
        if per_row:
            mv = [jnp.repeat(m[:, i], t, axis=0)[None] for i in range(6)]
        else:
            mv = [m[:, i][:, None, :] for i in range(6)]
        sh1, sc1, g1, sh2, sc2, g2 = mv
        p_rw, q, k, v, p_cv, p_cm = mix_projection(x, sc1, sh1, Wc['w_mix'][l])
        unflat = lambda a: a.reshape(bn, t, a.shape[-1])
        p_rw, q, k, v, p_cv, p_cm = map(unflat, (p_rw, q, k, v, p_cv, p_cm))
        y_rw, s_rw = rwkv_mixer(_pad_time(p_rw, tp), shift0[l], rw_s0[l], min(t, TIME_CHUNK) if tp == TIME_CHUNK
                                else TIME_CHUNK, rw_bk, wl)
        y_at = attend(l, q, k, v)
        y_cv, y_cm, v_rows, cv_st = local_mixers(_pad_time(p_cv, tp), _pad_time(p_cm, tp), conv0[l], t, wl)
        flat = (lambda a: a[:, :t].reshape(1, bn * t, -1)) if per_row else (lambda a: a[:, :t])
        ys = [flat(y_rw), flat(y_at), flat(y_cv), flat(y_cm)]
        x = out_projection(ys, x, g1, Wc['w_out'][l], wl['ln1_g'], wl['ln1_b'], alpha)
        h_bf, s1, s2, thr = peer_stage_a(x, sc2, sh2, Wc['peer_wqT'][l], Wc['peer_keys'][l])
        x = peer_stage_b(h_bf, Wc['peer_u'][l], Wc['peer_vT'][l], s1, s2, thr, x, g2,
                         wl['ln2_g'], wl['ln2_b'], alpha)
        ks.append(k.reshape(bn, t, N_HEADS, HEAD_DIM))
        vs.append(v.reshape(bn, t, N_HEADS, HEAD_DIM))
        rws.append(s_rw)
        shs.append(p_rw[:, t - 1])
        cvs.append(cv_st)
        cms.append(v_rows[:, :t])
    return (x.reshape(bn, t, d), jnp.stack(ks), jnp.stack(vs), jnp.stack(rws), jnp.stack(shs),
            jnp.stack(cvs), jnp.stack(cms))


def kernel(x_prompt, x_sample, cache_k, cache_v, state_rwkv, state_shift, state_conv, page_table, c_prompt, c_sample, ln_in_g, ln_in_b, w_ada, b_ada, w_mix, rw_mu, rw_w0, rw_w2, rw_a0, rw_a2, rw_g2, rw_kk, rw_ka, rw_rk, rw_lnx_g, rw_lnx_b, conv_w, cm_ln_g, cm_ln_b, cm_ws, cm_bs, w_out, ln1_g, ln1_b, ln2_g, ln2_b, peer_wq, peer_keys, peer_u, peer_v):
    W = {'ln_in_g': ln_in_g, 'ln_in_b': ln_in_b, 'w_mix': w_mix,
         'rw_mu': rw_mu, 'rw_w0': rw_w0, 'rw_w2': rw_w2, 'rw_a0': rw_a0, 'rw_a2': rw_a2,
         'rw_g2': rw_g2, 'rw_kk': rw_kk, 'rw_ka': rw_ka, 'rw_rk': rw_rk, 'rw_lnx_g': rw_lnx_g,
         'rw_lnx_b': rw_lnx_b, 'conv_w': conv_w, 'cm_ln_g': cm_ln_g, 'cm_ln_b': cm_ln_b,
         'cm_ws': cm_ws, 'cm_bs': cm_bs, 'ln1_g': ln1_g, 'ln1_b': ln1_b,
         'ln2_g': ln2_g, 'ln2_b': ln2_b}
    depth = w_mix.shape[0]
    Wc = {'w_mix': w_mix.astype(BF16), 'w_out': w_out.astype(BF16),
          'peer_wqT': jnp.swapaxes(peer_wq, 1, 2).astype(BF16),
          'peer_keys': peer_keys.reshape(depth, PEER_HEADS * 2, PEER_KEYS, PEER_HALF).astype(BF16),
          'peer_u': peer_u.astype(BF16),
          'peer_vT': peer_v.reshape(depth, -1, PEER_TILE, D_MODEL).transpose(0, 1, 3, 2).astype(BF16)}
    bp, dbn = x_prompt.shape[0], x_sample.shape[0]
    n_c = bp + dbn
    n_cp = -(-n_c // 8) * 8
    c_all = jnp.pad(jnp.concatenate([c_prompt, c_sample], axis=0), ((0, n_cp - n_c), (0, 0)))
    mods = ada_vectors(c_all, w_ada, b_ada).reshape(depth, n_cp, 6, D_MODEL)
    mods_p, mods_s = mods[:, :bp], mods[:, bp:n_c]
    dt = x_prompt.dtype
    z_rw = jnp.zeros((depth, bp, N_HEADS, HEAD_DIM, HEAD_DIM), dt)
    z_sh = jnp.zeros((depth, bp, RW_COLS), dt)
    z_cv = jnp.zeros((depth, bp, CONV_W - 1, GROUP_W), dt)
    pages = lambda c: c.transpose(0, 1, 3, 4, 2).reshape(c.shape[0], c.shape[1], GROUP_W, PAGE_SIZE)
    ck, cv = pages(cache_k), pages(cache_v)
    y_p, k_p, v_p, rw_p, sh_p, cv_p, _ = _forward_group(
        x_prompt, mods_p, z_rw, z_sh, z_cv, lambda l, q, k, v: moba_prompt(q, k, v), W, Wc,
        per_row=False, rw_bk=math.gcd(bp, 4))
    y_s, k_s, v_s, rw_s, sh_s, cv_s, cm_s = _forward_group(
        x_sample, mods_s, state_rwkv, state_shift, state_conv,
        lambda l, q, k, v: moba_sample(l, q, k, v, ck, cv, page_table), W, Wc,
        per_row=True, rw_bk=math.gcd(dbn, 4))
    return (y_p, y_s, k_p, v_p, k_s, v_s, rw_p, rw_s, sh_p, sh_s, cv_p, cv_s, cm_s)
```

```python
import functools
import math

import jax
import jax.numpy as jnp
from jax import lax
from jax.experimental import pallas as pl
from jax.experimental.pallas import tpu as pltpu

F32 = jnp.float32
BF16 = jnp.bfloat16
HI = lax.Precision.HIGHEST

D_MODEL = 1024
N_MIXERS = 4
GROUP_W = D_MODEL // N_MIXERS
HEAD_DIM = 64
N_HEADS = GROUP_W // HEAD_DIM
RW_W_RANK = 32
RW_A_RANK = 32
RW_G_RANK = 64
RW_COLS = 3 * GROUP_W + RW_W_RANK + RW_A_RANK + RW_G_RANK
MOBA_COLS = 3 * GROUP_W
CONV_COLS = 3 * GROUP_W
CMLP_COLS = 2 * GROUP_W
N_COLS = RW_COLS + MOBA_COLS + CONV_COLS + CMLP_COLS
MOBA_BLOCK = 256
MOBA_TOPK = 3
Q_BLOCK = 128
PAGE_SIZE = 128
CONV_W = 3
CHUNK = 128
PEER_KEYS = 128
PEER_EXPERTS = PEER_KEYS * PEER_KEYS
PEER_HEADS = 8
PEER_TOPK = 16
PEER_QDIM = 256
PEER_HALF = PEER_QDIM // 2
LN_EPS = 1e-5
GN_EPS = 64e-5
NEG_INF = float("-inf")
LOG2_E = 1.4426950408889634

VMEM_LIMIT = 48 * 1024 * 1024
TIME_CHUNK = 128
PAGE_UNROLL = 4
PEER_TILE = 8 * PEER_KEYS
BF16_ROWS = 16
PEER_B_PARTS = 4
RW_SUB = 64


def _params(*sem):
    return pltpu.CompilerParams(dimension_semantics=sem, vmem_limit_bytes=VMEM_LIMIT)


def _full(shape):
    n = len(shape)
    return pl.BlockSpec(shape, lambda *_: (0,) * n)


def _sigmoid(x):
    return 1.0 / (1.0 + jnp.exp(-x))


def _gelu(x):
    c = 0.7978845608028654
    return x * (0.5 + 0.5 * jnp.tanh(x * (c + (c * 0.044715) * (x * x))))


def _ln_rows(x, g, b):
    mu = jnp.mean(x, axis=-1, keepdims=True)
    d = x - mu
    var = jnp.mean(d * d, axis=-1, keepdims=True)
    return d * lax.rsqrt(var + LN_EPS) * g + b


def _dot_hi(a, b):
    return jnp.dot(a, b, precision=HI, preferred_element_type=F32)


def _dot_nt(a, b):
    return lax.dot_general(a, b, (((1,), (1,)), ((), ())), preferred_element_type=F32)


def _ln_kernel(x_ref, g_ref, b_ref, o_ref):
    o_ref[...] = _ln_rows(x_ref[...], g_ref[...], b_ref[...])


def layer_norm_rows(x2d, g, b):
    n, d = x2d.shape
    tm = min(512, n)
    return pl.pallas_call(
        _ln_kernel,
        grid=(n // tm,),
        in_specs=[pl.BlockSpec((tm, d), lambda i: (i, 0)), _full((1, d)), _full((1, d))],
        out_specs=pl.BlockSpec((tm, d), lambda i: (i, 0)),
        out_shape=jax.ShapeDtypeStruct((n, d), F32),
        compiler_params=_params("parallel"),
        name="ln_in",
    )(x2d, g.reshape(1, d), b.reshape(1, d))


def _ada_kernel(c_ref, w_ref, b_ref, o_ref):
    c = c_ref[...]
    s = (c * _sigmoid(c)).astype(BF16)
    o_ref[0] = jnp.dot(s, w_ref[0].astype(BF16), preferred_element_type=F32) + b_ref[0]


def ada_vectors(c_all, w_ada, b_ada):
    depth, d, n6 = w_ada.shape
    m = c_all.shape[0]
    tn = 1536
    return pl.pallas_call(
        _ada_kernel,
        grid=(depth, n6 // tn),
        in_specs=[_full((m, d)),
                  pl.BlockSpec((1, d, tn), lambda l, j: (l, 0, j)),
                  pl.BlockSpec((1, 1, tn), lambda l, j: (l, 0, j))],
        out_specs=pl.BlockSpec((1, m, tn), lambda l, j: (l, 0, j)),
        out_shape=jax.ShapeDtypeStruct((depth, m, n6), F32),
        compiler_params=_params("parallel", "parallel"),
        name="ada",
    )(c_all, w_ada, b_ada.reshape(depth, 1, n6))


_MIX_WIDTHS = (RW_COLS, GROUP_W, GROUP_W, GROUP_W, CONV_COLS, CMLP_COLS)


def _mix_kernel(x_ref, sc_ref, sh_ref, w_ref, *out_refs):
    h = (x_ref[0] * (1.0 + sc_ref[0]) + sh_ref[0]).astype(BF16)
    off = 0
    for ref, width in zip(out_refs, _MIX_WIDTHS):
        ref[0] = jnp.dot(h, w_ref[:, off:off + width], preferred_element_type=F32)
        off += width


def mix_projection(x, sc, sh, w_bf16):
    bn, t, d = x.shape
    tm = min(256, t)
    rows = sc.shape[1]
    per_row = rows == t and t > 1
    mod_block = (1, tm, d) if per_row else (1, 1, d)
    mod_map = (lambda b, i: (b, i, 0)) if per_row else (lambda b, i: (b, 0, 0))
    out_shape = [jax.ShapeDtypeStruct((bn, t, w), F32) for w in _MIX_WIDTHS]
    out_specs = [pl.BlockSpec((1, tm, w), lambda b, i: (b, i, 0)) for w in _MIX_WIDTHS]
    return pl.pallas_call(
        _mix_kernel,
        grid=(bn, t // tm),
        in_specs=[pl.BlockSpec((1, tm, d), lambda b, i: (b, i, 0)),
                  pl.BlockSpec(mod_block, mod_map),
                  pl.BlockSpec(mod_block, mod_map),
                  _full(w_bf16.shape)],
        out_specs=out_specs,
        out_shape=out_shape,
        compiler_params=_params("parallel", "parallel"),
        name="mix_proj",
    )(x, sc, sh, w_bf16)


def _softplus(x):
    return jnp.maximum(x, 0.0) + jnp.log(1.0 + jnp.exp(-jnp.abs(x)))


def _mm(a, b):
    return jnp.dot(a.astype(BF16), b.astype(BF16), preferred_element_type=F32)


def _dot_tn(a, b):
    return lax.dot_general(a, b, (((0,), (0,)), ((), ())), preferred_element_type=F32)


def _rwkv_subchunk(state, lw, kk, bb, k2, r, v):
    G = GROUP_W
    C = RW_SUB
    ri = lax.broadcasted_iota(jnp.int32, (C, C), 0)
    cj = lax.broadcasted_iota(jnp.int32, (C, C), 1)
    cum = _dot_hi(jnp.where(cj <= ri, 1.0, 0.0), lw)
    g_in = jnp.exp(cum)
    g_out = jnp.exp(-cum)
    at = -kk * jnp.exp(cum - lw)
    g_end = g_in[C - 1:C, :]
    lane_head = lax.broadcasted_iota(jnp.int32, (1, G), 1) // HEAD_DIM

    def stack(x):
        return jnp.concatenate([jnp.where(lane_head == h, x, 0.0) for h in range(N_HEADS)], axis=0)

    a_s, b_s, k_s, r_s, v_s = [stack(x) for x in (at, bb * g_out, k2 * g_out, r * g_in, v)]
    a_b, b_b, k_b, r_b, v_b = [x.astype(BF16) for x in (a_s, b_s, k_s, r_s, v_s)]
    s_b = state.astype(BF16)
    row = lax.broadcasted_iota(jnp.int32, (G, G), 0)
    col = lax.broadcasted_iota(jnp.int32, (G, G), 1)
    rt, ct = row % C, col % C
    strict = ct < rt
    incl = ct <= rt
    lmat = jnp.where(strict, _dot_nt(a_b, b_b), 0.0)
    a_ak = jnp.where(strict, _dot_nt(a_b, k_b), 0.0)
    a_rb = jnp.where(incl, _dot_nt(r_b, b_b), 0.0)
    a_rk = jnp.where(incl, _dot_nt(r_b, k_b), 0.0)
    rhs = _dot_nt(a_b, s_b) + _mm(a_ak, v_b)
    base = 8
    l1 = jnp.where(row // base == col // base, lmat, 0.0)
    l2 = _mm(l1, l1)
    n = l1 + l2 + _mm(l1, l2)
    l4 = _mm(l2, l2)
    n = n + l4 + _mm(n, l4)
    m = base
    while m < C:
        lower_left = jnp.logical_and(row // (2 * m) == col // (2 * m),
                                     jnp.logical_and((row // m) % 2 == 1, (col // m) % 2 == 0))
        lm = jnp.where(lower_left, lmat, 0.0)
        t1 = lm + _mm(n, lm)
        n = n + t1 + _mm(t1, n)
        m *= 2
    u_s = rhs + _mm(n, rhs)
    u_b = u_s.astype(BF16)
    y_s = _dot_nt(r_b, s_b) + _mm(a_rb, u_b) + _mm(a_rk, v_b)
    y = y_s[0:C] + y_s[C:2 * C] + y_s[2 * C:3 * C] + y_s[3 * C:4 * C]
    new_state = (state * g_end + _dot_tn(u_b, (b_s * g_end).astype(BF16))
                 + _dot_tn(v_b, (k_s * g_end).astype(BF16)))
    return y, new_state


def _rwkv_kernel(n_valid, bk,
                 p_ref, shift_ref, s0_ref, bd_ref, mu_ref, w0_ref, w2_ref, a0_ref, a2_ref, g2_ref,
                 kkp_ref, kap_ref, rk_ref, lng_ref, lnb_ref,
                 y_ref, s_ref, carry_ref):
    ci = pl.program_id(1)
    G = GROUP_W
    bd = bd_ref[...]

    @pl.when(ci == 0)
    def _():
        s_ref[...] = s0_ref[...]
        carry_ref[...] = shift_ref[...]

    row_id = lax.broadcasted_iota(jnp.int32, (TIME_CHUNK, 1), 0)
    sub_row = lax.broadcasted_iota(jnp.int32, (RW_SUB, 1), 0)

    for b in range(bk):
        p = p_ref[b]
        prev = jnp.where(row_id == 0, carry_ref[b], pltpu.roll(p, 1, axis=0))
        carry_ref[b] = p[TIME_CHUNK - 1:TIME_CHUNK, :]
        xm = p + (prev - p) * mu_ref[...]
        r, k, v = xm[:, :G], xm[:, G:2 * G], xm[:, 2 * G:3 * G]
        o = 3 * G
        wl = xm[:, o:o + RW_W_RANK]
        al = xm[:, o + RW_W_RANK:o + RW_W_RANK + RW_A_RANK]
        gl = xm[:, o + RW_W_RANK + RW_A_RANK:]
        w = -_softplus(-(w0_ref[...] + _dot_hi(jnp.tanh(wl), w2_ref[...]))) - 0.5
        lw = -jnp.exp(w)
        a = _sigmoid(a0_ref[...] + _dot_hi(al, a2_ref[...]))
        g = _dot_hi(_sigmoid(gl), g2_ref[...])
        kk = k * kkp_ref[...]
        kk = kk / jnp.maximum(jnp.sqrt(_dot_hi(kk * kk, bd)), 1e-12)
        k2 = k * (1.0 + (a - 1.0) * kap_ref[...])
        bb = kk * a
        bonus = _dot_hi(r * k2 * rk_ref[...], bd) * v

        state = s_ref[b]
        ys = []
        for j in range(TIME_CHUNK // RW_SUB):
            n_j = max(0, min(RW_SUB, n_valid - RW_SUB * j))
            rows = slice(j * RW_SUB, (j + 1) * RW_SUB)
            if n_j == 0:
                ys.append(jnp.zeros((RW_SUB, G), F32))
                continue
            lw_j, kk_j, bb_j, k2_j = lw[rows], kk[rows], bb[rows], k2[rows]
            if n_j < RW_SUB:
                live = sub_row < n_j
                lw_j, kk_j, bb_j, k2_j = [jnp.where(live, x, 0.0) for x in (lw_j, kk_j, bb_j, k2_j)]
            y_j, state = _rwkv_subchunk(state, lw_j, kk_j, bb_j, k2_j, r[rows], v[rows])
            ys.append(y_j)
        s_ref[b] = state
        y = jnp.concatenate(ys, axis=0)

        mean = _dot_hi(y, bd) * (1.0 / HEAD_DIM)
        d = y - mean
        var = _dot_hi(d * d, bd) * (1.0 / HEAD_DIM)
        yn = d * lax.rsqrt(var + GN_EPS) * lng_ref[...] + lnb_ref[...]
        y_ref[b] = (yn + bonus) * g


def rwkv_mixer(p, shift_prev, s0, n_valid, bk, wl):
    bn, tp, _ = p.shape
    G = GROUP_W
    eye_h = jnp.eye(N_HEADS, dtype=s0.dtype)
    s0bd = (s0[:, :, :, None, :] * eye_h[None, :, None, :, None]).reshape(bn, G, G)
    head_of = jnp.arange(G) // HEAD_DIM
    bd = (head_of[:, None] == head_of[None, :]).astype(F32)
    row = lambda a: a.reshape(1, -1)
    consts = [bd, row(wl['rw_mu']), row(wl['rw_w0']), wl['rw_w2'], row(wl['rw_a0']), wl['rw_a2'], wl['rw_g2'],
              row(wl['rw_kk']), row(wl['rw_ka']), row(wl['rw_rk']), row(wl['rw_lnx_g']), row(wl['rw_lnx_b'])]
    y, s_fin = pl.pallas_call(
        functools.partial(_rwkv_kernel, n_valid, bk),
        grid=(bn // bk, tp // TIME_CHUNK),
        in_specs=[pl.BlockSpec((bk, TIME_CHUNK, RW_COLS), lambda b, c: (b, c, 0)),
                  pl.BlockSpec((bk, 1, RW_COLS), lambda b, c: (b, 0, 0)),
                  pl.BlockSpec((bk, G, G), lambda b, c: (b, 0, 0))]
                 + [_full(a.shape) for a in consts],
        out_specs=[pl.BlockSpec((bk, TIME_CHUNK, G), lambda b, c: (b, c, 0)),
                   pl.BlockSpec((bk, G, G), lambda b, c: (b, 0, 0))],
        out_shape=[jax.ShapeDtypeStruct((bn, tp, G), F32),
                   jax.ShapeDtypeStruct((bn, G, G), F32)],
        scratch_shapes=[pltpu.VMEM((bk, 1, RW_COLS), F32)],
        compiler_params=_params("parallel", "arbitrary"),
        name="rwkv7",
    )(p, shift_prev.reshape(bn, 1, RW_COLS), s0bd, *consts)
    s4 = s_fin.reshape(bn, N_HEADS, HEAD_DIM, N_HEADS, HEAD_DIM)
    return y, jnp.stack([s4[:, h, :, h, :] for h in range(N_HEADS)], axis=1)


def _local_kernel(t_valid, cv_ref, cm_ref, prev_ref, cw_ref, lg_ref, lb_ref, ws_ref, bias_ref,
                  ycv_ref, ycm_ref, vrow_ref, st_ref, carry_ref):
    ci = pl.program_id(1)
    G = GROUP_W
    TC = TIME_CHUNK

    @pl.when(ci == 0)
    def _():
        carry_ref[...] = prev_ref[0]

    pc = cv_ref[0]
    bg, cg, hv = pc[:, :G], pc[:, G:2 * G], pc[:, 2 * G:]
    u = cg * hv
    row_id = lax.broadcasted_iota(jnp.int32, (TC, 1), 0)
    prev0, prev1 = carry_ref[0:1, :], carry_ref[1:2, :]
    u1 = jnp.where(row_id == 0, prev1, pltpu.roll(u, 1, axis=0))
    u2 = jnp.where(row_id == 0, prev0, jnp.where(row_id == 1, prev1, pltpu.roll(u, 2, axis=0)))
    ycv_ref[0] = bg * (cw_ref[0:1, :] * u2 + cw_ref[1:2, :] * u1 + cw_ref[2:3, :] * u)
    carry_ref[...] = u[TC - 2:, :]
    last = (t_valid - 1) // TC
    r = t_valid - last * TC

    @pl.when(ci == last)
    def _():
        st_ref[0] = u[r - 2:r, :]

    pm = cm_ref[0]
    uu = _gelu(pm[:, :G])
    vv = _ln_rows(_gelu(pm[:, G:]), lg_ref[...], lb_ref[...])
    vrow_ref[0] = vv
    head_of = lax.broadcasted_iota(jnp.int32, (1, G), 1) // HEAD_DIM
    causal = lax.broadcasted_iota(jnp.int32, (TC, TC), 0) >= lax.broadcasted_iota(jnp.int32, (TC, TC), 1)
    mixed = bias_ref[...]
    for h in range(N_HEADS):
        wm = jnp.where(causal, ws_ref[h], 0.0).astype(BF16)
        mixed = mixed + jnp.dot(wm, jnp.where(head_of == h, vv, 0.0).astype(BF16), preferred_element_type=F32)
    ycm_ref[0] = uu * mixed


def local_mixers(p_cv, p_cm, conv_prev, t_valid, wl):
    bn, tp, _ = p_cv.shape
    G = GROUP_W
    bias = jnp.repeat(wl['cm_bs'].T, HEAD_DIM, axis=1)
    row = lambda a: a.reshape(1, -1)
    tok = lambda w: pl.BlockSpec((1, TIME_CHUNK, w), lambda b, c: (b, c, 0))
    return pl.pallas_call(
        functools.partial(_local_kernel, t_valid),
        grid=(bn, tp // TIME_CHUNK),
        in_specs=[tok(CONV_COLS), tok(CMLP_COLS),
                  pl.BlockSpec((1, CONV_W - 1, G), lambda b, c: (b, 0, 0)),
                  _full((CONV_W, G)), _full((1, G)), _full((1, G)),
                  _full((N_HEADS, CHUNK, CHUNK)), _full((CHUNK, G))],
        out_specs=[tok(G), tok(G), tok(G), pl.BlockSpec((1, CONV_W - 1, G), lambda b, c: (b, 0, 0))],
        out_shape=[jax.ShapeDtypeStruct((bn, tp, G), F32)] * 3
                  + [jax.ShapeDtypeStruct((bn, CONV_W - 1, G), F32)],
        scratch_shapes=[pltpu.VMEM((CONV_W - 1, G), F32)],
        compiler_params=_params("parallel", "arbitrary"),
        name="conv_gmlp",
    )(p_cv, p_cm, conv_prev, wl['conv_w'], row(wl['cm_ln_g']), row(wl['cm_ln_b']), wl['cm_ws'], bias)


def _rank_rows(g, n_rows):
    rid = lax.broadcasted_iota(jnp.int32, g.shape, 0)
    rank = jnp.zeros(g.shape, jnp.int32)
    for j in range(n_rows):
        gj = g[j:j + 1, :]
        beats = jnp.where(gj > g, 1, jnp.where(jnp.logical_and(gj == g, rid > j), 1, 0))
        rank = rank + beats
    return rank


def _block_mean_kernel(k_ref, o_ref):
    o_ref[0, 0] = jnp.mean(k_ref[0], axis=0, keepdims=True)


def _moba_prompt_kernel(n_blk, n_sel, qT_ref, k_ref, vT_ref, kmean_ref, o_ref, sel_ref, s_ref):
    qi = pl.program_id(1)
    own = qi // (MOBA_BLOCK // Q_BLOCK)
    own_start = pl.multiple_of(own * MOBA_BLOCK, MOBA_BLOCK)
    kpos = own_start + lax.broadcasted_iota(jnp.int32, (MOBA_BLOCK, Q_BLOCK), 0)
    qpos = qi * Q_BLOCK + lax.broadcasted_iota(jnp.int32, (MOBA_BLOCK, Q_BLOCK), 1)
    causal = kpos <= qpos
    blk = lax.broadcasted_iota(jnp.int32, (n_blk, Q_BLOCK), 0)
    valid = blk < own

    def attend(h, start, qb, keep):
        s = jnp.dot(k_ref[0, h, pl.ds(start, MOBA_BLOCK), :], qb, preferred_element_type=F32)
        return jnp.where(keep, s, NEG_INF)

    def values(h, start, p):
        return jnp.dot(vT_ref[0, h, :, pl.ds(start, MOBA_BLOCK)], p.astype(BF16), preferred_element_type=F32)

    def fold8(s):
        return jnp.max(s.reshape(MOBA_BLOCK // 8, 8, Q_BLOCK), axis=0)

    qbs, peaks = [], []
    for h in range(N_HEADS):
        qT = qT_ref[0, h]
        gate = jnp.where(valid, _dot_hi(kmean_ref[0, h], qT), NEG_INF)
        sel = jnp.where(jnp.logical_and(valid, _rank_rows(gate, n_blk) < n_sel), 1.0, 0.0)
        for j in range(n_blk):
            sel_ref[h, j] = jnp.broadcast_to(sel[j:j + 1, :], (8, Q_BLOCK))
        qb = (qT * (HEAD_DIM ** -0.5 * LOG2_E)).astype(BF16)
        s = attend(h, own_start, qb, causal)
        s_ref[h, own] = s
        qbs.append(qb)
        peaks.append(fold8(s))

    def pass_a(j, peaks):
        start = pl.multiple_of(j * MOBA_BLOCK, MOBA_BLOCK)
        out = []
        for h in range(N_HEADS):
            s = attend(h, start, qbs[h], sel_ref[h, j][0:1, :] > 0.0)
            s_ref[h, j] = s
            out.append(jnp.maximum(peaks[h], fold8(s)))
        return tuple(out)

    peaks = lax.fori_loop(0, own, pass_a, tuple(peaks))
    ms = [jnp.max(pk, axis=0, keepdims=True) for pk in peaks]

    def pass_b(j, accs):
        start = pl.multiple_of(j * MOBA_BLOCK, MOBA_BLOCK)
        return tuple(accs[h] + values(h, start, jnp.exp2(s_ref[h, j] - ms[h])) for h in range(N_HEADS))

    rows = vT_ref.shape[2]
    accs = lax.fori_loop(0, own + 1, pass_b, tuple(jnp.zeros((rows, Q_BLOCK), F32) for _ in range(N_HEADS)))
    for h in range(N_HEADS):
        o_ref[0, h] = accs[h][:HEAD_DIM] / accs[h][HEAD_DIM:HEAD_DIM + 1]


def moba_prompt(q, k, v):
    bn, t, G = q.shape
    n_blk = t // MOBA_BLOCK
    n_sel = min(MOBA_TOPK, n_blk - 1)
    kmean = pl.pallas_call(
        _block_mean_kernel,
        grid=(bn, n_blk),
        in_specs=[pl.BlockSpec((1, MOBA_BLOCK, G), lambda b, j: (b, j, 0))],
        out_specs=pl.BlockSpec((1, 1, 1, G), lambda b, j: (b, j, 0, 0)),
        out_shape=jax.ShapeDtypeStruct((bn, n_blk, 1, G), F32),
        compiler_params=_params("parallel", "parallel"),
        name="moba_block_means",
    )(k)
    heads = lambda a: a.reshape(bn, -1, N_HEADS, HEAD_DIM)
    kmean = heads(kmean).transpose(0, 2, 1, 3)
    qT = heads(q).transpose(0, 2, 3, 1)
    kh = heads(k).transpose(0, 2, 1, 3).astype(BF16)
    vT = heads(v).transpose(0, 2, 3, 1).astype(BF16)
    v_rows = HEAD_DIM + 16
    vT = jnp.concatenate([vT, jnp.ones((bn, N_HEADS, 1, t), BF16),
                          jnp.zeros((bn, N_HEADS, v_rows - HEAD_DIM - 1, t), BF16)], axis=2)
    hq = lambda rows, cols: pl.BlockSpec((1, N_HEADS, rows, cols), lambda b, i: (b, 0, 0, 0))
    tile = pl.BlockSpec((1, N_HEADS, HEAD_DIM, Q_BLOCK), lambda b, i: (b, 0, 0, i))
    oT = pl.pallas_call(
        functools.partial(_moba_prompt_kernel, n_blk, n_sel),
        grid=(bn, t // Q_BLOCK),
        in_specs=[tile, hq(t, HEAD_DIM), hq(v_rows, t), hq(n_blk, HEAD_DIM)],
        out_specs=tile,
        out_shape=jax.ShapeDtypeStruct((bn, N_HEADS, HEAD_DIM, t), F32),
        scratch_shapes=[pltpu.VMEM((N_HEADS, n_blk, 8, Q_BLOCK), F32),
                        pltpu.VMEM((N_HEADS, n_blk, MOBA_BLOCK, Q_BLOCK), F32)],
        compiler_params=_params("parallel", "arbitrary"),
        name="moba_prompt",
    )(qT, kh, vT, kmean)
    return oT.transpose(0, 3, 1, 2).reshape(bn, t, GROUP_W)


def _rank_cols(g, n_cols):
    cid = lax.broadcasted_iota(jnp.int32, g.shape, 1)
    rank = jnp.zeros(g.shape, jnp.int32)
    for j in range(n_cols):
        gj = g[:, j:j + 1]
        rank = rank + jnp.where(gj > g, 1, jnp.where(jnp.logical_and(gj == g, cid > j), 1, 0))
    return rank


def _moba_decode_kernel(layer, n_pages, n_sel, t_new, pt_ref, q_ref, kn_ref, vn_ref, ck_hbm, cv_hbm, o_ref,
                        kbuf, vbuf, s_ref, ksem, vsem):
    b = pl.program_id(0)
    nb = pl.num_programs(0)
    slot = lax.rem(b, 2)
    G = GROUP_W
    R = N_HEADS * t_new
    scale = HEAD_DIM ** -0.5
    pages_per_blk = MOBA_BLOCK // PAGE_SIZE
    n_pf = n_pages // pages_per_blk

    def k_copy(row, p, sl):
        return pltpu.make_async_copy(ck_hbm.at[layer, pt_ref[row, p]], kbuf.at[sl, p], ksem.at[sl])

    def v_copy(p):
        return pltpu.make_async_copy(cv_hbm.at[layer, pt_ref[b, p]], vbuf.at[p], vsem.at[0])

    def for_pages(fn):
        def body(p, c):
            fn(p)
            return c
        lax.fori_loop(0, n_pages, body, 0, unroll=PAGE_UNROLL)

    @pl.when(b == 0)
    def _():
        for_pages(lambda p: k_copy(0, p, 0).start())

    for_pages(lambda p: v_copy(p).start())

    @pl.when(b + 1 < nb)
    def _():
        for_pages(lambda p: k_copy(b + 1, p, 1 - slot).start())

    for_pages(lambda p: k_copy(b, p, slot).wait())

    q = q_ref[0]
    row_head = lax.broadcasted_iota(jnp.int32, (R, G), 0) // t_new
    lane_head = lax.broadcasted_iota(jnp.int32, (R, G), 1) // HEAD_DIM
    own_head = row_head == lane_head
    qbd = jnp.where(own_head, jnp.concatenate([q] * N_HEADS, axis=0), 0.0)
    qb = qbd.astype(BF16)
    blk_lane = lax.broadcasted_iota(jnp.int32, (1, n_pf), 1)

    def pass1(p, ksum_t):
        k_t = kbuf[slot, p]
        s_ref[p] = jnp.dot(qb, k_t.astype(BF16), preferred_element_type=F32) * scale
        return ksum_t + jnp.where(blk_lane == p // pages_per_blk, jnp.sum(k_t, axis=-1, keepdims=True), 0.0)

    ksum_t = lax.fori_loop(0, n_pages, pass1, jnp.zeros((G, n_pf), F32), unroll=PAGE_UNROLL)
    gate = _dot_hi(qbd, ksum_t * (1.0 / MOBA_BLOCK))
    sel = jnp.where(_rank_cols(gate, n_pf) < n_sel, 1.0, 0.0)

    s_own = _dot_nt(qb, kn_ref[0].astype(BF16)) * scale
    tq = lax.broadcasted_iota(jnp.int32, s_own.shape, 0) % t_new
    tk = lax.broadcasted_iota(jnp.int32, s_own.shape, 1)
    s_own = jnp.where(tk <= tq, s_own, NEG_INF)

    def pass2(p, m):
        chosen = jnp.sum(jnp.where(blk_lane == p // pages_per_blk, sel, 0.0), axis=-1, keepdims=True)
        s = jnp.where(chosen > 0.0, s_ref[p], NEG_INF)
        s_ref[p] = s
        return jnp.maximum(m, jnp.max(s, axis=-1, keepdims=True))

    m = lax.fori_loop(0, n_pages, pass2, jnp.max(s_own, axis=-1, keepdims=True), unroll=PAGE_UNROLL)

    for_pages(lambda p: v_copy(p).wait())

    def pass3(p, carry):
        l, acc = carry
        pr = jnp.exp(s_ref[p] - m)
        return (l + jnp.sum(pr, axis=-1, keepdims=True),
                acc + _dot_nt(pr.astype(BF16), vbuf[p].astype(BF16)))

    p_own = jnp.exp(s_own - m)
    l, acc = lax.fori_loop(
        0, n_pages, pass3,
        (jnp.sum(p_own, axis=-1, keepdims=True),
         jnp.dot(p_own.astype(BF16), vn_ref[0].astype(BF16), preferred_element_type=F32)),
        unroll=PAGE_UNROLL)
    o = jnp.where(own_head, acc / l, 0.0)
    out = o[0:t_new]
    for h in range(1, N_HEADS):
        out = out + o[h * t_new:(h + 1) * t_new]
    o_ref[0] = out


def moba_sample(layer, q, k, v, cache_k, cache_v, page_table):
    db, t_new, G = q.shape
    n_pages = page_table.shape[1]
    past = n_pages * PAGE_SIZE
    assert past % MOBA_BLOCK == 0, "decode kernel expects the past to end on a MoBA block boundary"
    n_sel = min(MOBA_TOPK, past // MOBA_BLOCK)
    R = N_HEADS * t_new
    tok = pl.BlockSpec((1, t_new, G), lambda b, pt: (b, 0, 0))
    pool = pl.BlockSpec(memory_space=pl.ANY)
    return pl.pallas_call(
        functools.partial(_moba_decode_kernel, layer, n_pages, n_sel, t_new),
        grid_spec=pltpu.PrefetchScalarGridSpec(
            num_scalar_prefetch=1, grid=(db,),
            in_specs=[tok, tok, tok, pool, pool],
            out_specs=tok,
            scratch_shapes=[pltpu.VMEM((2, n_pages, G, PAGE_SIZE), F32),
                            pltpu.VMEM((n_pages, G, PAGE_SIZE), F32),
                            pltpu.VMEM((n_pages, R, PAGE_SIZE), F32),
                            pltpu.SemaphoreType.DMA((2,)), pltpu.SemaphoreType.DMA((1,))]),
        out_shape=jax.ShapeDtypeStruct((db, t_new, G), F32),
        compiler_params=_params("arbitrary"),
        name="moba_decode",
    )(page_table, q, k, v, cache_k, cache_v)


def _out_kernel(alpha, yrw_ref, yat_ref, ycv_ref, ycm_ref, x_ref, g1_ref, w_ref, lg_ref, lb_ref, o_ref):
    G = GROUP_W
    acc = None
    for i, ref in enumerate((yrw_ref, yat_ref, ycv_ref, ycm_ref)):
        part = jnp.dot(ref[0].astype(BF16), w_ref[i * G:(i + 1) * G, :], preferred_element_type=F32)
        acc = part if acc is None else acc + part
    z = alpha * x_ref[0] + (1.0 + g1_ref[0]) * acc
    o_ref[0] = _ln_rows(z, lg_ref[...], lb_ref[...])


def out_projection(ys, x, g1, w_bf16, ln_g, ln_b, alpha):
    bn, t, d = x.shape
    tm = min(256, t)
    rows = g1.shape[1]
    per_row = rows == t and t > 1
    mod_block = (1, tm, d) if per_row else (1, 1, d)
    mod_map = (lambda b, i: (b, i, 0)) if per_row else (lambda b, i: (b, 0, 0))
    tok = lambda w: pl.BlockSpec((1, tm, w), lambda b, i: (b, i, 0))
    return pl.pallas_call(
        functools.partial(_out_kernel, alpha),
        grid=(bn, t // tm),
        in_specs=[tok(GROUP_W)] * 4 + [tok(d), pl.BlockSpec(mod_block, mod_map),
                                       _full(w_bf16.shape), _full((1, d)), _full((1, d))],
        out_specs=tok(d),
        out_shape=jax.ShapeDtypeStruct((bn, t, d), F32),
        compiler_params=_params("parallel", "parallel"),
        name="out_proj_ln",
    )(*ys, x, g1, w_bf16, ln_g.reshape(1, d), ln_b.reshape(1, d))


_N_TOP = PEER_TOPK + 1


_UNRANKED = 127.0


def _top_values(x, n, rows_out, with_rank=False):
    rid = lax.broadcasted_iota(jnp.int32, (rows_out, x.shape[1]), 0)
    vals = jnp.full((rows_out, x.shape[1]), NEG_INF, F32)
    rank = jnp.full(x.shape, _UNRANKED, F32) if with_rank else None
    for i in range(n):
        mx = jnp.max(x, axis=0, keepdims=True)
        vals = jnp.where(rid == i, mx, vals)
        hit = x == mx
        if with_rank:
            rank = jnp.where(hit, float(i), rank)
        x = jnp.where(hit, NEG_INF, x)
    return (vals, rank) if with_rank else vals


def _pack_rows(x):
    return pltpu.bitcast(x.astype(BF16), jnp.uint32)


def _pack_twice(x):
    bits = pltpu.bitcast(x.astype(BF16).astype(F32), jnp.uint32)
    return jnp.bitwise_or(bits, lax.shift_right_logical(bits, jnp.uint32(16)))


def _peer_a_kernel(x_ref, sc_ref, sh_ref, wqT_ref, keys_ref, h_ref, e1_ref, cnt_ref, e2_ref, rank_ref):
    h = (x_ref[0] * (1.0 + sc_ref[0]) + sh_ref[0]).astype(BF16)
    h_ref[0] = h
    qT = _dot_nt(wqT_ref[...], h).astype(BF16)
    for hh in range(PEER_HEADS):
        r0 = hh * PEER_QDIM
        s1 = jnp.dot(keys_ref[2 * hh], qT[r0:r0 + PEER_HALF], preferred_element_type=F32)
        s2 = jnp.dot(keys_ref[2 * hh + 1], qT[r0 + PEER_HALF:r0 + PEER_QDIM], preferred_element_type=F32)
        v1 = _top_values(s1, _N_TOP, 24)
        v2, rank2 = _top_values(s2, _N_TOP, 24, with_rank=True)
        cands = ([v1[0:1] + v2, v2[0:1] + v1] + [v1[a:a + 1] + v2[0:8] for a in (1, 2, 3)]
                 + [v1[4:8] + v2[1:2], v1[4:8] + v2[2:3]])
        best = _top_values(jnp.concatenate(cands, axis=0), _N_TOP, 24)
        z = jnp.sum(jnp.exp(best[0:PEER_TOPK] - best[0:1]), axis=0, keepdims=True)
        thr = 0.5 * (best[PEER_TOPK - 1:PEER_TOPK] + best[PEER_TOPK:PEER_TOPK + 1])
        cnt = jnp.zeros(s1.shape, F32)
        for b in range(8):
            cnt = cnt + jnp.where(s1 >= thr - v2[b:b + 1], 1.0, 0.0)
        deep = jnp.sum(jnp.where(v1[0:1] + v2[8:_N_TOP] >= thr, 1.0, 0.0), axis=0, keepdims=True)
        cnt = cnt + jnp.where(s1 == v1[0:1], deep, 0.0)
        keys = slice(hh * PEER_KEYS, (hh + 1) * PEER_KEYS)
        half = slice(hh * PEER_KEYS // 2, (hh + 1) * PEER_KEYS // 2)
        e1_ref[keys, :] = _pack_twice(jnp.exp(s1 - v1[0:1]) / z)
        cnt_ref[keys, :] = _pack_twice(cnt)
        e2_ref[half, :] = _pack_rows(jnp.exp(s2 - v2[0:1]))
        rank_ref[half, :] = _pack_rows(rank2)


def peer_stage_a(x, sc, sh, wqT_bf16, keys_bf16):
    bn, t, d = x.shape
    n = bn * t
    tm = min(256, t)
    rows = sc.shape[1]
    per_row = rows == t and t > 1
    mod_block = (1, tm, d) if per_row else (1, 1, d)
    mod_map = (lambda b, i: (b, i, 0)) if per_row else (lambda b, i: (b, 0, 0))
    tpb = t // tm
    col = lambda r: pl.BlockSpec((r, tm), lambda b, i: (0, b * tpb + i))
    nk = PEER_HEADS * PEER_KEYS
    return pl.pallas_call(
        _peer_a_kernel,
        grid=(bn, tpb),
        in_specs=[pl.BlockSpec((1, tm, d), lambda b, i: (b, i, 0)),
                  pl.BlockSpec(mod_block, mod_map), pl.BlockSpec(mod_block, mod_map),
                  _full(wqT_bf16.shape), _full(keys_bf16.shape)],
        out_specs=[pl.BlockSpec((1, tm, d), lambda b, i: (b, i, 0)), col(nk), col(nk), col(nk // 2), col(nk // 2)],
        out_shape=[jax.ShapeDtypeStruct((bn, t, d), BF16),
                   jax.ShapeDtypeStruct((nk, n), jnp.uint32), jax.ShapeDtypeStruct((nk, n), jnp.uint32),
                   jax.ShapeDtypeStruct((nk // 2, n), jnp.uint32), jax.ShapeDtypeStruct((nk // 2, n), jnp.uint32)],
        compiler_params=_params("parallel", "parallel"),
        name="peer_scores",
    )(x, sc, sh, wqT_bf16, keys_bf16)


def _peer_b_kernel(alpha, te, h_ref, *refs):
    n_parts = PEER_B_PARTS
    u_refs = refs[:n_parts]
    (vT_ref, e1_ref, cnt_ref, e2_ref, rank_ref, x_ref, g2_ref, lg_ref, lb_ref,
     o_ref, acc_ref, w_ref) = refs[n_parts:n_parts + 12]
    a_refs = refs[n_parts + 12:]
    e = pl.program_id(2)
    n_tiles = pl.num_programs(2) - 1
    tm = w_ref.shape[2]
    slot = lax.rem(e, 2)

    @pl.when(e == 0)
    def _():
        acc_ref[...] = jnp.zeros(acc_ref.shape, F32)
        w_ref[1] = jnp.zeros(w_ref.shape[1:], BF16)

    def drain():
        acc_ref[...] += jnp.dot(vT_ref[0], w_ref[1 - slot], preferred_element_type=F32)

    groups = te // PEER_KEYS
    assert groups == 8
    part_groups = groups // len(a_refs)
    part_rows = part_groups * PEER_KEYS

    @pl.when(e < n_tiles)
    def _():
        row0 = [pl.multiple_of(hh * PEER_KEYS + e * groups, 8) for hh in range(PEER_HEADS)]
        for q, a_ref in enumerate(a_refs):
            a_ref[...] = _dot_nt(u_refs[q][...], h_ref[0])
            tl = min(128, tm)
            for c in range(tm // tl):
                lanes = slice(c * tl, (c + 1) * tl)
                e1_rows = [e1_ref[pl.ds(row0[hh], groups), lanes] for hh in range(PEER_HEADS)]
                cnt_rows = [cnt_ref[pl.ds(row0[hh], groups), lanes] for hh in range(PEER_HEADS)]
                n_sub = PEER_KEYS // BF16_ROWS
                for gl in range(part_groups):
                    gi = q * part_groups + gl
                    gates = [None] * n_sub
                    for hh in range(PEER_HEADS):
                        e1 = pltpu.bitcast(jnp.broadcast_to(e1_rows[hh][gi:gi + 1, :], (8, tl)), BF16)
                        cnt = pltpu.bitcast(jnp.broadcast_to(cnt_rows[hh][gi:gi + 1, :], (8, tl)), BF16)
                        for r in range(n_sub):
                            words = slice(hh * PEER_KEYS // 2 + r * 8, hh * PEER_KEYS // 2 + (r + 1) * 8)
                            rank = pltpu.bitcast(rank_ref[words, lanes], BF16)
                            e2 = pltpu.bitcast(e2_ref[words, lanes], BF16)
                            part = jnp.where(rank < cnt, e2 * e1, jnp.zeros((), BF16))
                            gates[r] = part if gates[r] is None else gates[r] + part
                    for r in range(n_sub):
                        rows = slice(gl * PEER_KEYS + r * BF16_ROWS, gl * PEER_KEYS + (r + 1) * BF16_ROWS)
                        out_rows = slice(gi * PEER_KEYS + r * BF16_ROWS, gi * PEER_KEYS + (r + 1) * BF16_ROWS)
                        w_ref[slot, out_rows, lanes] = gates[r] * _gelu(a_ref[rows, lanes]).astype(BF16)
        drain()

    @pl.when(e == n_tiles)
    def _():
        drain()
        z = alpha * x_ref[0] + (1.0 + g2_ref[0]) * acc_ref[...].T
        o_ref[0] = _ln_rows(z, lg_ref[...], lb_ref[...])


def peer_stage_b(h_bf16, u_bf16, vT_bf16, e1, cnt, e2, rank2, x, g2, ln_g, ln_b, alpha):
    bn, t, d = x.shape
    tm = min(512, t)
    te = PEER_TILE
    n_tiles = u_bf16.shape[0] // te
    rows = g2.shape[1]
    per_row = rows == t and t > 1
    mod_block = (1, tm, d) if per_row else (1, 1, d)
    mod_map = (lambda b, i, e: (b, i, 0)) if per_row else (lambda b, i, e: (b, 0, 0))
    tpb = t // tm
    tok = lambda: pl.BlockSpec((1, tm, d), lambda b, i, e: (b, i, 0))
    col = lambda r: pl.BlockSpec((r, tm), lambda b, i, e: (0, b * tpb + i))
    nk = PEER_HEADS * PEER_KEYS
    n_parts = PEER_B_PARTS
    u_specs = [pl.BlockSpec((te // n_parts, d),
                            lambda b, i, e, q=q: (jnp.minimum(e, n_tiles - 1) * n_parts + q, 0))
               for q in range(n_parts)]
    v_spec = pl.BlockSpec((1, d, te), lambda b, i, e: (jnp.maximum(e - 1, 0), 0, 0))
    return pl.pallas_call(
        functools.partial(_peer_b_kernel, alpha, te),
        grid=(bn, tpb, n_tiles + 1),
        in_specs=[tok()] + u_specs + [
                  v_spec,
                  col(nk), col(nk), col(nk // 2), col(nk // 2),
                  tok(), pl.BlockSpec(mod_block, mod_map), _full((1, d)), _full((1, d))],
        out_specs=tok(),
        out_shape=jax.ShapeDtypeStruct((bn, t, d), F32),
        scratch_shapes=[pltpu.VMEM((d, tm), F32), pltpu.VMEM((2, te, tm), BF16)]
                       + [pltpu.VMEM((te // PEER_B_PARTS, tm), F32) for _ in range(PEER_B_PARTS)],
        compiler_params=_params("parallel", "parallel", "arbitrary"),
        name="peer_experts",
    )(h_bf16, *([u_bf16] * n_parts), vT_bf16, e1, cnt, e2, rank2, x, g2,
      ln_g.reshape(1, d), ln_b.reshape(1, d))


def _pad_time(a, tp):
    return a if a.shape[1] == tp else jnp.pad(a, ((0, 0), (0, tp - a.shape[1]), (0, 0)))


def _forward_group(x, mods, rw_s0, shift0, conv0, attend, W, Wc, per_row, rw_bk):
    bn, t, d = x.shape
    depth = W['w_mix'].shape[0]
    alpha = (2.0 * depth) ** 0.25
    tp = -(-t // TIME_CHUNK) * TIME_CHUNK
    x = layer_norm_rows(x.reshape(bn * t, d), W['ln_in_g'], W['ln_in_b']).reshape(bn, t, d)
    if per_row:
        x = x.reshape(1, bn * t, d)
    ks, vs, rws, shs, cvs, cms = [], [], [], [], [], []
    for l in range(depth):
        wl = {name: W[name][l] for name in W if name not in ('ln_in_g', 'ln_in_b')}
        m = mods[l]
        if per_row:
            mv = [jnp.repeat(m[:, i], t, axis=0)[None] for i in range(6)]
        else:
            mv = [m[:, i][:, None, :] for i in range(6)]
        sh1, sc1, g1, sh2, sc2, g2 = mv
        p_rw, q, k, v, p_cv, p_cm = mix_projection(x, sc1, sh1, Wc['w_mix'][l])
        unflat = lambda a: a.reshape(bn, t, a.shape[-1])
        p_rw, q, k, v, p_cv, p_cm = map(unflat, (p_rw, q, k, v, p_cv, p_cm))
        y_rw, s_rw = rwkv_mixer(_pad_time(p_rw, tp), shift0[l], rw_s0[l], min(t, TIME_CHUNK) if tp == TIME_CHUNK
                                else TIME_CHUNK, rw_bk, wl)
        y_at = attend(l, q, k, v)
        y_cv, y_cm, v_rows, cv_st = local_mixers(_pad_time(p_cv, tp), _pad_time(p_cm, tp), conv0[l], t, wl)
        flat = (lambda a: a[:, :t].reshape(1, bn * t, -1)) if per_row else (lambda a: a[:, :t])
        ys = [flat(y_rw), flat(y_at), flat(y_cv), flat(y_cm)]
        x = out_projection(ys, x, g1, Wc['w_out'][l], wl['ln1_g'], wl['ln1_b'], alpha)
        h_bf, e1, cnt, e2, rank2 = peer_stage_a(x, sc2, sh2, Wc['peer_wqT'][l], Wc['peer_keys'][l])
        x = peer_stage_b(h_bf, Wc['peer_u'][l], Wc['peer_vT'][l], e1, cnt, e2, rank2, x, g2,
                         wl['ln2_g'], wl['ln2_b'], alpha)
        ks.append(k.reshape(bn, t, N_HEADS, HEAD_DIM))
        vs.append(v.reshape(bn, t, N_HEADS, HEAD_DIM))
        rws.append(s_rw)
        shs.append(p_rw[:, t - 1])
        cvs.append(cv_st)
        cms.append(v_rows[:, :t])
    return (x.reshape(bn, t, d), jnp.stack(ks), jnp.stack(vs), jnp.stack(rws), jnp.stack(shs),
            jnp.stack(cvs), jnp.stack(cms))


def kernel(x_prompt, x_sample, cache_k, cache_v, state_rwkv, state_shift, state_conv, page_table, c_prompt, c_sample, ln_in_g, ln_in_b, w_ada, b_ada, w_mix, rw_mu, rw_w0, rw_w2, rw_a0, rw_a2, rw_g2, rw_kk, rw_ka, rw_rk, rw_lnx_g, rw_lnx_b, conv_w, cm_ln_g, cm_ln_b, cm_ws, cm_bs, w_out, ln1_g, ln1_b, ln2_g, ln2_b, peer_wq, peer_keys, peer_u, peer_v):
    W = {'ln_in_g': ln_in_g, 'ln_in_b': ln_in_b, 'w_mix': w_mix,
         'rw_mu': rw_mu, 'rw_w0': rw_w0, 'rw_w2': rw_w2, 'rw_a0': rw_a0, 'rw_a2': rw_a2,
         'rw_g2': rw_g2, 'rw_kk': rw_kk, 'rw_ka': rw_ka, 'rw_rk': rw_rk, 'rw_lnx_g': rw_lnx_g,
         'rw_lnx_b': rw_lnx_b, 'conv_w': conv_w, 'cm_ln_g': cm_ln_g, 'cm_ln_b': cm_ln_b,
         'cm_ws': cm_ws, 'cm_bs': cm_bs, 'ln1_g': ln1_g, 'ln1_b': ln1_b,
         'ln2_g': ln2_g, 'ln2_b': ln2_b}
    depth = w_mix.shape[0]
    Wc = {'w_mix': w_mix.astype(BF16), 'w_out': w_out.astype(BF16),
          'peer_wqT': jnp.swapaxes(peer_wq, 1, 2).astype(BF16),
          'peer_keys': peer_keys.reshape(depth, PEER_HEADS * 2, PEER_KEYS, PEER_HALF).astype(BF16),
          'peer_u': peer_u.astype(BF16),
          'peer_vT': peer_v.reshape(depth, -1, PEER_TILE, D_MODEL).transpose(0, 1, 3, 2).astype(BF16)}
    bp, dbn = x_prompt.shape[0], x_sample.shape[0]
    n_c = bp + dbn
    n_cp = -(-n_c // 8) * 8
    c_all = jnp.pad(jnp.concatenate([c_prompt, c_sample], axis=0), ((0, n_cp - n_c), (0, 0)))
    mods = ada_vectors(c_all, w_ada, b_ada).reshape(depth, n_cp, 6, D_MODEL)
    mods_p, mods_s = mods[:, :bp], mods[:, bp:n_c]
    dt = x_prompt.dtype
    z_rw = jnp.zeros((depth, bp, N_HEADS, HEAD_DIM, HEAD_DIM), dt)
    z_sh = jnp.zeros((depth, bp, RW_COLS), dt)
    z_cv = jnp.zeros((depth, bp, CONV_W - 1, GROUP_W), dt)
    pages = lambda c: c.transpose(0, 1, 3, 4, 2).reshape(c.shape[0], c.shape[1], GROUP_W, PAGE_SIZE)
    ck, cv = pages(cache_k), pages(cache_v)
    y_p, k_p, v_p, rw_p, sh_p, cv_p, _ = _forward_group(
        x_prompt, mods_p, z_rw, z_sh, z_cv, lambda l, q, k, v: moba_prompt(q, k, v), W, Wc,
        per_row=False, rw_bk=math.gcd(bp, 4))
    y_s, k_s, v_s, rw_s, sh_s, cv_s, cm_s = _forward_group(
        x_sample, mods_s, state_rwkv, state_shift, state_conv,
        lambda l, q, k, v: moba_sample(l, q, k, v, ck, cv, page_table), W, Wc,
        per_row=True, rw_bk=math.gcd(dbn, 4))
    return (y_p, y_s, k_p, v_p, k_s, v_s, rw_p, rw_s, sh_p, sh_s, cv_p, cv_s, cm_s)
```

```python
import functools
import math

import jax
import jax.numpy as jnp
from jax import lax
from jax.experimental import pallas as pl
from jax.experimental.pallas import tpu as pltpu

F32 = jnp.float32
BF16 = jnp.bfloat16
HI = lax.Precision.HIGHEST

D_MODEL = 1024
N_MIXERS = 4
GROUP_W = D_MODEL // N_MIXERS
HEAD_DIM = 64
N_HEADS = GROUP_W // HEAD_DIM
RW_W_RANK = 32
RW_A_RANK = 32
RW_G_RANK = 64
RW_COLS = 3 * GROUP_W + RW_W_RANK + RW_A_RANK + RW_G_RANK
MOBA_COLS = 3 * GROUP_W
CONV_COLS = 3 * GROUP_W
CMLP_COLS = 2 * GROUP_W
N_COLS = RW_COLS + MOBA_COLS + CONV_COLS + CMLP_COLS
MOBA_BLOCK = 256
MOBA_TOPK = 3
Q_BLOCK = 128
PAGE_SIZE = 128
CONV_W = 3
CHUNK = 128
PEER_KEYS = 128
PEER_EXPERTS = PEER_KEYS * PEER_KEYS
PEER_HEADS = 8
PEER_TOPK = 16
PEER_QDIM = 256
PEER_HALF = PEER_QDIM // 2
LN_EPS = 1e-5
GN_EPS = 64e-5
NEG_INF = float("-inf")
LOG2_E = 1.4426950408889634

VMEM_LIMIT = 48 * 1024 * 1024
TIME_CHUNK = 128
PAGE_UNROLL = 4
PEER_TILE = 8 * PEER_KEYS
BF16_ROWS = 16
PEER_B_PARTS = 4
RW_SUB = 64


def _params(*sem):
    return pltpu.CompilerParams(dimension_semantics=sem, vmem_limit_bytes=VMEM_LIMIT)


def _full(shape):
    n = len(shape)
    return pl.BlockSpec(shape, lambda *_: (0,) * n)


def _sigmoid(x):
    return 1.0 / (1.0 + jnp.exp(-x))


def _gelu(x):
    c = 0.7978845608028654
    return x * (0.5 + 0.5 * jnp.tanh(x * (c + (c * 0.044715) * (x * x))))


def _ln_rows(x, g, b):
    mu = jnp.mean(x, axis=-1, keepdims=True)
    d = x - mu
    var = jnp.mean(d * d, axis=-1, keepdims=True)
    return d * lax.rsqrt(var + LN_EPS) * g + b


def _dot_hi(a, b):
    return jnp.dot(a, b, precision=HI, preferred_element_type=F32)


def _dot_nt(a, b):
    return lax.dot_general(a, b, (((1,), (1,)), ((), ())), preferred_element_type=F32)


def _ln_kernel(x_ref, g_ref, b_ref, o_ref):
    o_ref[...] = _ln_rows(x_ref[...], g_ref[...], b_ref[...])


def layer_norm_rows(x2d, g, b):
    n, d = x2d.shape
    tm = min(512, n)
    return pl.pallas_call(
        _ln_kernel,
        grid=(n // tm,),
        in_specs=[pl.BlockSpec((tm, d), lambda i: (i, 0)), _full((1, d)), _full((1, d))],
        out_specs=pl.BlockSpec((tm, d), lambda i: (i, 0)),
        out_shape=jax.ShapeDtypeStruct((n, d), F32),
        compiler_params=_params("parallel"),
        name="ln_in",
    )(x2d, g.reshape(1, d), b.reshape(1, d))


def _ada_kernel(c_ref, w_ref, b_ref, o_ref):
    c = c_ref[...]
    s = (c * _sigmoid(c)).astype(BF16)
    o_ref[0] = jnp.dot(s, w_ref[0].astype(BF16), preferred_element_type=F32) + b_ref[0]


def ada_vectors(c_all, w_ada, b_ada):
    depth, d, n6 = w_ada.shape
    m = c_all.shape[0]
    tn = 1536
    return pl.pallas_call(
        _ada_kernel,
        grid=(depth, n6 // tn),
        in_specs=[_full((m, d)),
                  pl.BlockSpec((1, d, tn), lambda l, j: (l, 0, j)),
                  pl.BlockSpec((1, 1, tn), lambda l, j: (l, 0, j))],
        out_specs=pl.BlockSpec((1, m, tn), lambda l, j: (l, 0, j)),
        out_shape=jax.ShapeDtypeStruct((depth, m, n6), F32),
        compiler_params=_params("parallel", "parallel"),
        name="ada",
    )(c_all, w_ada, b_ada.reshape(depth, 1, n6))


_MIX_WIDTHS = (RW_COLS, GROUP_W, GROUP_W, GROUP_W, CONV_COLS, CMLP_COLS)


def _mix_kernel(x_ref, sc_ref, sh_ref, w_ref, *out_refs):
    h = (x_ref[0] * (1.0 + sc_ref[0]) + sh_ref[0]).astype(BF16)
    off = 0
    for ref, width in zip(out_refs, _MIX_WIDTHS):
        ref[0] = jnp.dot(h, w_ref[:, off:off + width], preferred_element_type=F32)
        off += width


def mix_projection(x, sc, sh, w_bf16):
    bn, t, d = x.shape
    tm = min(256, t)
    rows = sc.shape[1]
    per_row = rows == t and t > 1
    mod_block = (1, tm, d) if per_row else (1, 1, d)
    mod_map = (lambda b, i: (b, i, 0)) if per_row else (lambda b, i: (b, 0, 0))
    out_shape = [jax.ShapeDtypeStruct((bn, t, w), F32) for w in _MIX_WIDTHS]
    out_specs = [pl.BlockSpec((1, tm, w), lambda b, i: (b, i, 0)) for w in _MIX_WIDTHS]
    return pl.pallas_call(
        _mix_kernel,
        grid=(bn, t // tm),
        in_specs=[pl.BlockSpec((1, tm, d), lambda b, i: (b, i, 0)),
                  pl.BlockSpec(mod_block, mod_map),
                  pl.BlockSpec(mod_block, mod_map),
                  _full(w_bf16.shape)],
        out_specs=out_specs,
        out_shape=out_shape,
        compiler_params=_params("parallel", "parallel"),
        name="mix_proj",
    )(x, sc, sh, w_bf16)


def _softplus(x):
    return jnp.maximum(x, 0.0) + jnp.log(1.0 + jnp.exp(-jnp.abs(x)))


def _mm(a, b):
    return jnp.dot(a.astype(BF16), b.astype(BF16), preferred_element_type=F32)


def _dot_tn(a, b):
    return lax.dot_general(a, b, (((0,), (0,)), ((), ())), preferred_element_type=F32)


def _rwkv_subchunk(state, lw, kk, bb, k2, r, v):
    G = GROUP_W
    C = RW_SUB
    ri = lax.broadcasted_iota(jnp.int32, (C, C), 0)
    cj = lax.broadcasted_iota(jnp.int32, (C, C), 1)
    cum = _dot_hi(jnp.where(cj <= ri, 1.0, 0.0), lw)
    g_in = jnp.exp(cum)
    g_out = jnp.exp(-cum)
    at = -kk * jnp.exp(cum - lw)
    g_end = g_in[C - 1:C, :]
    lane_head = lax.broadcasted_iota(jnp.int32, (1, G), 1) // HEAD_DIM

    def stack(x):
        return jnp.concatenate([jnp.where(lane_head == h, x, 0.0) for h in range(N_HEADS)], axis=0)

    a_s, b_s, k_s, r_s, v_s = [stack(x) for x in (at, bb * g_out, k2 * g_out, r * g_in, v)]
    a_b, b_b, k_b, r_b, v_b = [x.astype(BF16) for x in (a_s, b_s, k_s, r_s, v_s)]
    s_b = state.astype(BF16)
    row = lax.broadcasted_iota(jnp.int32, (G, G), 0)
    col = lax.broadcasted_iota(jnp.int32, (G, G), 1)
    rt, ct = row % C, col % C
    strict = ct < rt
    incl = ct <= rt
    lmat = jnp.where(strict, _dot_nt(a_b, b_b), 0.0)
    a_ak = jnp.where(strict, _dot_nt(a_b, k_b), 0.0)
    a_rb = jnp.where(incl, _dot_nt(r_b, b_b), 0.0)
    a_rk = jnp.where(incl, _dot_nt(r_b, k_b), 0.0)
    rhs = _dot_nt(a_b, s_b) + _mm(a_ak, v_b)
    base = 8
    l1 = jnp.where(row // base == col // base, lmat, 0.0)
    l2 = _mm(l1, l1)
    n = l1 + l2 + _mm(l1, l2)
    l4 = _mm(l2, l2)
    n = n + l4 + _mm(n, l4)
    m = base
    while m < C:
        lower_left = jnp.logical_and(row // (2 * m) == col // (2 * m),
                                     jnp.logical_and((row // m) % 2 == 1, (col // m) % 2 == 0))
        lm = jnp.where(lower_left, lmat, 0.0)
        t1 = lm + _mm(n, lm)
        n = n + t1 + _mm(t1, n)
        m *= 2
    u_s = rhs + _mm(n, rhs)
    u_b = u_s.astype(BF16)
    y_s = _dot_nt(r_b, s_b) + _mm(a_rb, u_b) + _mm(a_rk, v_b)
    y = y_s[0:C] + y_s[C:2 * C] + y_s[2 * C:3 * C] + y_s[3 * C:4 * C]
    new_state = (state * g_end + _dot_tn(u_b, (b_s * g_end).astype(BF16))
                 + _dot_tn(v_b, (k_s * g_end).astype(BF16)))
    return y, new_state


def _rwkv_kernel(n_valid, bk,
                 p_ref, shift_ref, s0_ref, bd_ref, mu_ref, w0_ref, w2_ref, a0_ref, a2_ref, g2_ref,
                 kkp_ref, kap_ref, rk_ref, lng_ref, lnb_ref,
                 y_ref, s_ref, carry_ref):
    ci = pl.program_id(1)
    G = GROUP_W
    bd = bd_ref[...]

    @pl.when(ci == 0)
    def _():
        s_ref[...] = s0_ref[...]
        carry_ref[...] = shift_ref[...]

    row_id = lax.broadcasted_iota(jnp.int32, (TIME_CHUNK, 1), 0)
    sub_row = lax.broadcasted_iota(jnp.int32, (RW_SUB, 1), 0)

    for b in range(bk):
        p = p_ref[b]
        prev = jnp.where(row_id == 0, carry_ref[b], pltpu.roll(p, 1, axis=0))
        carry_ref[b] = p[TIME_CHUNK - 1:TIME_CHUNK, :]
        xm = p + (prev - p) * mu_ref[...]
        r, k, v = xm[:, :G], xm[:, G:2 * G], xm[:, 2 * G:3 * G]
        o = 3 * G
        wl = xm[:, o:o + RW_W_RANK]
        al = xm[:, o + RW_W_RANK:o + RW_W_RANK + RW_A_RANK]
        gl = xm[:, o + RW_W_RANK + RW_A_RANK:]
        w = -_softplus(-(w0_ref[...] + _dot_hi(jnp.tanh(wl), w2_ref[...]))) - 0.5
        lw = -jnp.exp(w)
        a = _sigmoid(a0_ref[...] + _dot_hi(al, a2_ref[...]))
        g = _dot_hi(_sigmoid(gl), g2_ref[...])
        kk = k * kkp_ref[...]
        kk = kk / jnp.maximum(jnp.sqrt(_dot_hi(kk * kk, bd)), 1e-12)
        k2 = k * (1.0 + (a - 1.0) * kap_ref[...])
        bb = kk * a
        bonus = _dot_hi(r * k2 * rk_ref[...], bd) * v

        state = s_ref[b]
        ys = []
        for j in range(TIME_CHUNK // RW_SUB):
            n_j = max(0, min(RW_SUB, n_valid - RW_SUB * j))
            rows = slice(j * RW_SUB, (j + 1) * RW_SUB)
            if n_j == 0:
                ys.append(jnp.zeros((RW_SUB, G), F32))
                continue
            lw_j, kk_j, bb_j, k2_j = lw[rows], kk[rows], bb[rows], k2[rows]
            if n_j < RW_SUB:
                live = sub_row < n_j
                lw_j, kk_j, bb_j, k2_j = [jnp.where(live, x, 0.0) for x in (lw_j, kk_j, bb_j, k2_j)]
            y_j, state = _rwkv_subchunk(state, lw_j, kk_j, bb_j, k2_j, r[rows], v[rows])
            ys.append(y_j)
        s_ref[b] = state
        y = jnp.concatenate(ys, axis=0)

        mean = _dot_hi(y, bd) * (1.0 / HEAD_DIM)
        d = y - mean
        var = _dot_hi(d * d, bd) * (1.0 / HEAD_DIM)
        yn = d * lax.rsqrt(var + GN_EPS) * lng_ref[...] + lnb_ref[...]
        y_ref[b] = (yn + bonus) * g


def rwkv_mixer(p, shift_prev, s0, n_valid, bk, wl):
    bn, tp, _ = p.shape
    G = GROUP_W
    eye_h = jnp.eye(N_HEADS, dtype=s0.dtype)
    s0bd = (s0[:, :, :, None, :] * eye_h[None, :, None, :, None]).reshape(bn, G, G)
    head_of = jnp.arange(G) // HEAD_DIM
    bd = (head_of[:, None] == head_of[None, :]).astype(F32)
    row = lambda a: a.reshape(1, -1)
    consts = [bd, row(wl['rw_mu']), row(wl['rw_w0']), wl['rw_w2'], row(wl['rw_a0']), wl['rw_a2'], wl['rw_g2'],
              row(wl['rw_kk']), row(wl['rw_ka']), row(wl['rw_rk']), row(wl['rw_lnx_g']), row(wl['rw_lnx_b'])]
    y, s_fin = pl.pallas_call(
        functools.partial(_rwkv_kernel, n_valid, bk),
        grid=(bn // bk, tp // TIME_CHUNK),
        in_specs=[pl.BlockSpec((bk, TIME_CHUNK, RW_COLS), lambda b, c: (b, c, 0)),
                  pl.BlockSpec((bk, 1, RW_COLS), lambda b, c: (b, 0, 0)),
                  pl.BlockSpec((bk, G, G), lambda b, c: (b, 0, 0))]
                 + [_full(a.shape) for a in consts],
        out_specs=[pl.BlockSpec((bk, TIME_CHUNK, G), lambda b, c: (b, c, 0)),
                   pl.BlockSpec((bk, G, G), lambda b, c: (b, 0, 0))],
        out_shape=[jax.ShapeDtypeStruct((bn, tp, G), F32),
                   jax.ShapeDtypeStruct((bn, G, G), F32)],
        scratch_shapes=[pltpu.VMEM((bk, 1, RW_COLS), F32)],
        compiler_params=_params("parallel", "arbitrary"),
        name="rwkv7",
    )(p, shift_prev.reshape(bn, 1, RW_COLS), s0bd, *consts)
    s4 = s_fin.reshape(bn, N_HEADS, HEAD_DIM, N_HEADS, HEAD_DIM)
    return y, jnp.stack([s4[:, h, :, h, :] for h in range(N_HEADS)], axis=1)


def _local_kernel(t_valid, cv_ref, cm_ref, prev_ref, cw_ref, lg_ref, lb_ref, ws_ref, bias_ref,
                  ycv_ref, ycm_ref, vrow_ref, st_ref, carry_ref):
    ci = pl.program_id(1)
    G = GROUP_W
    TC = TIME_CHUNK

    @pl.when(ci == 0)
    def _():
        carry_ref[...] = prev_ref[0]

    pc = cv_ref[0]
    bg, cg, hv = pc[:, :G], pc[:, G:2 * G], pc[:, 2 * G:]
    u = cg * hv
    row_id = lax.broadcasted_iota(jnp.int32, (TC, 1), 0)
    prev0, prev1 = carry_ref[0:1, :], carry_ref[1:2, :]
    u1 = jnp.where(row_id == 0, prev1, pltpu.roll(u, 1, axis=0))
    u2 = jnp.where(row_id == 0, prev0, jnp.where(row_id == 1, prev1, pltpu.roll(u, 2, axis=0)))
    ycv_ref[0] = bg * (cw_ref[0:1, :] * u2 + cw_ref[1:2, :] * u1 + cw_ref[2:3, :] * u)
    carry_ref[...] = u[TC - 2:, :]
    last = (t_valid - 1) // TC
    r = t_valid - last * TC

    @pl.when(ci == last)
    def _():
        st_ref[0] = u[r - 2:r, :]

    pm = cm_ref[0]
    uu = _gelu(pm[:, :G])
    vv = _ln_rows(_gelu(pm[:, G:]), lg_ref[...], lb_ref[...])
    vrow_ref[0] = vv
    head_of = lax.broadcasted_iota(jnp.int32, (1, G), 1) // HEAD_DIM
    causal = lax.broadcasted_iota(jnp.int32, (TC, TC), 0) >= lax.broadcasted_iota(jnp.int32, (TC, TC), 1)
    mixed = bias_ref[...]
    for h in range(N_HEADS):
        wm = jnp.where(causal, ws_ref[h], 0.0).astype(BF16)
        mixed = mixed + jnp.dot(wm, jnp.where(head_of == h, vv, 0.0).astype(BF16), preferred_element_type=F32)
    ycm_ref[0] = uu * mixed


def local_mixers(p_cv, p_cm, conv_prev, t_valid, wl):
    bn, tp, _ = p_cv.shape
    G = GROUP_W
    bias = jnp.repeat(wl['cm_bs'].T, HEAD_DIM, axis=1)
    row = lambda a: a.reshape(1, -1)
    tok = lambda w: pl.BlockSpec((1, TIME_CHUNK, w), lambda b, c: (b, c, 0))
    return pl.pallas_call(
        functools.partial(_local_kernel, t_valid),
        grid=(bn, tp // TIME_CHUNK),
        in_specs=[tok(CONV_COLS), tok(CMLP_COLS),
                  pl.BlockSpec((1, CONV_W - 1, G), lambda b, c: (b, 0, 0)),
                  _full((CONV_W, G)), _full((1, G)), _full((1, G)),
                  _full((N_HEADS, CHUNK, CHUNK)), _full((CHUNK, G))],
        out_specs=[tok(G), tok(G), tok(G), pl.BlockSpec((1, CONV_W - 1, G), lambda b, c: (b, 0, 0))],
        out_shape=[jax.ShapeDtypeStruct((bn, tp, G), F32)] * 3
                  + [jax.ShapeDtypeStruct((bn, CONV_W - 1, G), F32)],
        scratch_shapes=[pltpu.VMEM((CONV_W - 1, G), F32)],
        compiler_params=_params("parallel", "arbitrary"),
        name="conv_gmlp",
    )(p_cv, p_cm, conv_prev, wl['conv_w'], row(wl['cm_ln_g']), row(wl['cm_ln_b']), wl['cm_ws'], bias)


def _rank_rows(g, n_rows):
    rid = lax.broadcasted_iota(jnp.int32, g.shape, 0)
    rank = jnp.zeros(g.shape, jnp.int32)
    for j in range(n_rows):
        gj = g[j:j + 1, :]
        beats = jnp.where(gj > g, 1, jnp.where(jnp.logical_and(gj == g, rid > j), 1, 0))
        rank = rank + beats
    return rank


def _block_mean_kernel(k_ref, o_ref):
    for i in range(o_ref.shape[1]):
        o_ref[0, i] = jnp.mean(k_ref[0, i * MOBA_BLOCK:(i + 1) * MOBA_BLOCK], axis=0, keepdims=True)


def _moba_prompt_kernel(n_blk, n_sel, qT_ref, k_ref, vT_ref, kmean_ref, o_ref, sel_ref, s_ref):
    qi = pl.program_id(1)
    own = qi // (MOBA_BLOCK // Q_BLOCK)
    own_start = pl.multiple_of(own * MOBA_BLOCK, MOBA_BLOCK)
    kpos = own_start + lax.broadcasted_iota(jnp.int32, (MOBA_BLOCK, Q_BLOCK), 0)
    qpos = qi * Q_BLOCK + lax.broadcasted_iota(jnp.int32, (MOBA_BLOCK, Q_BLOCK), 1)
    causal = kpos <= qpos
    blk = lax.broadcasted_iota(jnp.int32, (n_blk, Q_BLOCK), 0)
    valid = blk < own

    def attend(h, start, qb, keep):
        s = jnp.dot(k_ref[0, h, pl.ds(start, MOBA_BLOCK), :], qb, preferred_element_type=F32)
        return jnp.where(keep, s, NEG_INF)

    def values(h, start, p):
        return jnp.dot(vT_ref[0, h, :, pl.ds(start, MOBA_BLOCK)], p.astype(BF16), preferred_element_type=F32)

    def fold8(s):
        return jnp.max(s.reshape(MOBA_BLOCK // 8, 8, Q_BLOCK), axis=0)

    qbs, peaks = [], []
    for h in range(N_HEADS):
        qT = qT_ref[0, h]
        gate = jnp.where(valid, _dot_hi(kmean_ref[0, h], qT), NEG_INF)
        sel = jnp.where(jnp.logical_and(valid, _rank_rows(gate, n_blk) < n_sel), 1.0, 0.0)
        for j in range(n_blk):
            sel_ref[h, j] = jnp.broadcast_to(sel[j:j + 1, :], (8, Q_BLOCK))
        qb = (qT * (HEAD_DIM ** -0.5 * LOG2_E)).astype(BF16)
        s = attend(h, own_start, qb, causal)
        s_ref[h, own] = s
        qbs.append(qb)
        peaks.append(fold8(s))

    def pass_a(j, peaks):
        start = pl.multiple_of(j * MOBA_BLOCK, MOBA_BLOCK)
        out = []
        for h in range(N_HEADS):
            s = attend(h, start, qbs[h], sel_ref[h, j][0:1, :] > 0.0)
            s_ref[h, j] = s
            out.append(jnp.maximum(peaks[h], fold8(s)))
        return tuple(out)

    peaks = lax.fori_loop(0, own, pass_a, tuple(peaks))
    ms = [jnp.max(pk, axis=0, keepdims=True) for pk in peaks]

    def pass_b(j, accs):
        start = pl.multiple_of(j * MOBA_BLOCK, MOBA_BLOCK)
        return tuple(accs[h] + values(h, start, jnp.exp2(s_ref[h, j] - ms[h])) for h in range(N_HEADS))

    rows = vT_ref.shape[2]
    accs = lax.fori_loop(0, own + 1, pass_b, tuple(jnp.zeros((rows, Q_BLOCK), F32) for _ in range(N_HEADS)))
    for h in range(N_HEADS):
        o_ref[0, h] = accs[h][:HEAD_DIM] / accs[h][HEAD_DIM:HEAD_DIM + 1]


def moba_prompt(q, k, v):
    bn, t, G = q.shape
    n_blk = t // MOBA_BLOCK
    n_sel = min(MOBA_TOPK, n_blk - 1)
    per_step = math.gcd(n_blk, 4)
    kmean = pl.pallas_call(
        _block_mean_kernel,
        grid=(bn, n_blk // per_step),
        in_specs=[pl.BlockSpec((1, per_step * MOBA_BLOCK, G), lambda b, j: (b, j, 0))],
        out_specs=pl.BlockSpec((1, per_step, 1, G), lambda b, j: (b, j, 0, 0)),
        out_shape=jax.ShapeDtypeStruct((bn, n_blk, 1, G), F32),
        compiler_params=_params("parallel", "parallel"),
        name="moba_block_means",
    )(k)
    heads = lambda a: a.reshape(bn, -1, N_HEADS, HEAD_DIM)
    kmean = heads(kmean).transpose(0, 2, 1, 3)
    qT = heads(q).transpose(0, 2, 3, 1)
    kh = heads(k).transpose(0, 2, 1, 3).astype(BF16)
    vT = heads(v).transpose(0, 2, 3, 1).astype(BF16)
    v_rows = HEAD_DIM + 16
    vT = jnp.concatenate([vT, jnp.ones((bn, N_HEADS, 1, t), BF16),
                          jnp.zeros((bn, N_HEADS, v_rows - HEAD_DIM - 1, t), BF16)], axis=2)
    hq = lambda rows, cols: pl.BlockSpec((1, N_HEADS, rows, cols), lambda b, i: (b, 0, 0, 0))
    tile = pl.BlockSpec((1, N_HEADS, HEAD_DIM, Q_BLOCK), lambda b, i: (b, 0, 0, i))
    oT = pl.pallas_call(
        functools.partial(_moba_prompt_kernel, n_blk, n_sel),
        grid=(bn, t // Q_BLOCK),
        in_specs=[tile, hq(t, HEAD_DIM), hq(v_rows, t), hq(n_blk, HEAD_DIM)],
        out_specs=tile,
        out_shape=jax.ShapeDtypeStruct((bn, N_HEADS, HEAD_DIM, t), F32),
        scratch_shapes=[pltpu.VMEM((N_HEADS, n_blk, 8, Q_BLOCK), F32),
                        pltpu.VMEM((N_HEADS, n_blk, MOBA_BLOCK, Q_BLOCK), F32)],
        compiler_params=_params("parallel", "arbitrary"),
        name="moba_prompt",
    )(qT, kh, vT, kmean)
    return oT.transpose(0, 3, 1, 2).reshape(bn, t, GROUP_W)


def _rank_cols(g, n_cols):
    cid = lax.broadcasted_iota(jnp.int32, g.shape, 1)
    rank = jnp.zeros(g.shape, jnp.int32)
    for j in range(n_cols):
        gj = g[:, j:j + 1]
        rank = rank + jnp.where(gj > g, 1, jnp.where(jnp.logical_and(gj == g, cid > j), 1, 0))
    return rank


def _moba_decode_kernel(layer, n_pages, n_sel, t_new, pt_ref, q_ref, kn_ref, vn_ref, ck_hbm, cv_hbm, o_ref,
                        kbuf, vbuf, s_ref, ksem, vsem):
    b = pl.program_id(0)
    nb = pl.num_programs(0)
    slot = lax.rem(b, 2)
    G = GROUP_W
    R = N_HEADS * t_new
    scale = HEAD_DIM ** -0.5
    pages_per_blk = MOBA_BLOCK // PAGE_SIZE
    n_pf = n_pages // pages_per_blk

    def k_copy(row, p, sl):
        return pltpu.make_async_copy(ck_hbm.at[layer, pt_ref[row, p]], kbuf.at[sl, p], ksem.at[sl])

    def v_copy(p):
        return pltpu.make_async_copy(cv_hbm.at[layer, pt_ref[b, p]], vbuf.at[p], vsem.at[0])

    def for_pages(fn):
        def body(p, c):
            fn(p)
            return c
        lax.fori_loop(0, n_pages, body, 0, unroll=PAGE_UNROLL)

    @pl.when(b == 0)
    def _():
        for_pages(lambda p: k_copy(0, p, 0).start())

    for_pages(lambda p: v_copy(p).start())

    @pl.when(b + 1 < nb)
    def _():
        for_pages(lambda p: k_copy(b + 1, p, 1 - slot).start())

    for_pages(lambda p: k_copy(b, p, slot).wait())

    q = q_ref[0]
    row_head = lax.broadcasted_iota(jnp.int32, (R, G), 0) // t_new
    lane_head = lax.broadcasted_iota(jnp.int32, (R, G), 1) // HEAD_DIM
    own_head = row_head == lane_head
    qbd = jnp.where(own_head, jnp.concatenate([q] * N_HEADS, axis=0), 0.0)
    qb = qbd.astype(BF16)
    blk_lane = lax.broadcasted_iota(jnp.int32, (1, n_pf), 1)

    def pass1(p, ksum_t):
        k_t = kbuf[slot, p]
        s_ref[p] = jnp.dot(qb, k_t.astype(BF16), preferred_element_type=F32) * scale
        return ksum_t + jnp.where(blk_lane == p // pages_per_blk, jnp.sum(k_t, axis=-1, keepdims=True), 0.0)

    ksum_t = lax.fori_loop(0, n_pages, pass1, jnp.zeros((G, n_pf), F32), unroll=PAGE_UNROLL)
    gate = _dot_hi(qbd, ksum_t * (1.0 / MOBA_BLOCK))
    sel = jnp.where(_rank_cols(gate, n_pf) < n_sel, 1.0, 0.0)

    s_own = _dot_nt(qb, kn_ref[0].astype(BF16)) * scale
    tq = lax.broadcasted_iota(jnp.int32, s_own.shape, 0) % t_new
    tk = lax.broadcasted_iota(jnp.int32, s_own.shape, 1)
    s_own = jnp.where(tk <= tq, s_own, NEG_INF)

    def pass2(p, m):
        chosen = jnp.sum(jnp.where(blk_lane == p // pages_per_blk, sel, 0.0), axis=-1, keepdims=True)
        s = jnp.where(chosen > 0.0, s_ref[p], NEG_INF)
        s_ref[p] = s
        return jnp.maximum(m, jnp.max(s, axis=-1, keepdims=True))

    m = lax.fori_loop(0, n_pages, pass2, jnp.max(s_own, axis=-1, keepdims=True), unroll=PAGE_UNROLL)

    for_pages(lambda p: v_copy(p).wait())

    def pass3(p, carry):
        l, acc = carry
        pr = jnp.exp(s_ref[p] - m)
        return (l + jnp.sum(pr, axis=-1, keepdims=True),
                acc + _dot_nt(pr.astype(BF16), vbuf[p].astype(BF16)))

    p_own = jnp.exp(s_own - m)
    l, acc = lax.fori_loop(
        0, n_pages, pass3,
        (jnp.sum(p_own, axis=-1, keepdims=True),
         jnp.dot(p_own.astype(BF16), vn_ref[0].astype(BF16), preferred_element_type=F32)),
        unroll=PAGE_UNROLL)
    o = jnp.where(own_head, acc / l, 0.0)
    out = o[0:t_new]
    for h in range(1, N_HEADS):
        out = out + o[h * t_new:(h + 1) * t_new]
    o_ref[0] = out


def moba_sample(layer, q, k, v, cache_k, cache_v, page_table):
    db, t_new, G = q.shape
    n_pages = page_table.shape[1]
    past = n_pages * PAGE_SIZE
    assert past % MOBA_BLOCK == 0, "decode kernel expects the past to end on a MoBA block boundary"
    n_sel = min(MOBA_TOPK, past // MOBA_BLOCK)
    R = N_HEADS * t_new
    tok = pl.BlockSpec((1, t_new, G), lambda b, pt: (b, 0, 0))
    pool = pl.BlockSpec(memory_space=pl.ANY)
    return pl.pallas_call(
        functools.partial(_moba_decode_kernel, layer, n_pages, n_sel, t_new),
        grid_spec=pltpu.PrefetchScalarGridSpec(
            num_scalar_prefetch=1, grid=(db,),
            in_specs=[tok, tok, tok, pool, pool],
            out_specs=tok,
            scratch_shapes=[pltpu.VMEM((2, n_pages, G, PAGE_SIZE), F32),
                            pltpu.VMEM((n_pages, G, PAGE_SIZE), F32),
                            pltpu.VMEM((n_pages, R, PAGE_SIZE), F32),
                            pltpu.SemaphoreType.DMA((2,)), pltpu.SemaphoreType.DMA((1,))]),
        out_shape=jax.ShapeDtypeStruct((db, t_new, G), F32),
        compiler_params=_params("arbitrary"),
        name="moba_decode",
    )(page_table, q, k, v, cache_k, cache_v)


def _out_kernel(alpha, yrw_ref, yat_ref, ycv_ref, ycm_ref, x_ref, g1_ref, w_ref, lg_ref, lb_ref, o_ref):
    G = GROUP_W
    acc = None
    for i, ref in enumerate((yrw_ref, yat_ref, ycv_ref, ycm_ref)):
        part = jnp.dot(ref[0].astype(BF16), w_ref[i * G:(i + 1) * G, :], preferred_element_type=F32)
        acc = part if acc is None else acc + part
    z = alpha * x_ref[0] + (1.0 + g1_ref[0]) * acc
    o_ref[0] = _ln_rows(z, lg_ref[...], lb_ref[...])


def out_projection(ys, x, g1, w_bf16, ln_g, ln_b, alpha):
    bn, t, d = x.shape
    tm = min(256, t)
    rows = g1.shape[1]
    per_row = rows == t and t > 1
    mod_block = (1, tm, d) if per_row else (1, 1, d)
    mod_map = (lambda b, i: (b, i, 0)) if per_row else (lambda b, i: (b, 0, 0))
    tok = lambda w: pl.BlockSpec((1, tm, w), lambda b, i: (b, i, 0))
    return pl.pallas_call(
        functools.partial(_out_kernel, alpha),
        grid=(bn, t // tm),
        in_specs=[tok(GROUP_W)] * 4 + [tok(d), pl.BlockSpec(mod_block, mod_map),
                                       _full(w_bf16.shape), _full((1, d)), _full((1, d))],
        out_specs=tok(d),
        out_shape=jax.ShapeDtypeStruct((bn, t, d), F32),
        compiler_params=_params("parallel", "parallel"),
        name="out_proj_ln",
    )(*ys, x, g1, w_bf16, ln_g.reshape(1, d), ln_b.reshape(1, d))


_N_TOP = PEER_TOPK + 1


_UNRANKED = 127.0


def _top_values(x, n, rows_out, with_rank=False):
    rid = lax.broadcasted_iota(jnp.int32, (rows_out, x.shape[1]), 0)
    vals = jnp.full((rows_out, x.shape[1]), NEG_INF, F32)
    rank = jnp.full(x.shape, _UNRANKED, F32) if with_rank else None
    for i in range(n):
        mx = jnp.max(x, axis=0, keepdims=True)
        vals = jnp.where(rid == i, mx, vals)
        hit = x == mx
        if with_rank:
            rank = jnp.where(hit, float(i), rank)
        x = jnp.where(hit, NEG_INF, x)
    return (vals, rank) if with_rank else vals


def _pack_rows(x):
    return pltpu.bitcast(x.astype(BF16), jnp.uint32)


def _pack_twice(x):
    bits = pltpu.bitcast(x.astype(BF16).astype(F32), jnp.uint32)
    return jnp.bitwise_or(bits, lax.shift_right_logical(bits, jnp.uint32(16)))


def _peer_a_kernel(x_ref, sc_ref, sh_ref, wqT_ref, keys_ref, h_ref, e1_ref, cnt_ref, e2_ref, rank_ref):
    h = (x_ref[0] * (1.0 + sc_ref[0]) + sh_ref[0]).astype(BF16)
    h_ref[0] = h
    qT = _dot_nt(wqT_ref[...], h).astype(BF16)
    for hh in range(PEER_HEADS):
        r0 = hh * PEER_QDIM
        s1 = jnp.dot(keys_ref[2 * hh], qT[r0:r0 + PEER_HALF], preferred_element_type=F32)
        s2 = jnp.dot(keys_ref[2 * hh + 1], qT[r0 + PEER_HALF:r0 + PEER_QDIM], preferred_element_type=F32)
        v1 = _top_values(s1, _N_TOP, 24)
        v2, rank2 = _top_values(s2, _N_TOP, 24, with_rank=True)
        cands = ([v1[0:1] + v2, v2[0:1] + v1] + [v1[a:a + 1] + v2[0:8] for a in (1, 2, 3)]
                 + [v1[4:8] + v2[1:2], v1[4:8] + v2[2:3]])
        best = _top_values(jnp.concatenate(cands, axis=0), _N_TOP, 24)
        z = jnp.sum(jnp.exp(best[0:PEER_TOPK] - best[0:1]), axis=0, keepdims=True)
        thr = 0.5 * (best[PEER_TOPK - 1:PEER_TOPK] + best[PEER_TOPK:PEER_TOPK + 1])
        cnt = jnp.zeros(s1.shape, F32)
        for b in range(8):
            cnt = cnt + jnp.where(s1 >= thr - v2[b:b + 1], 1.0, 0.0)
        deep = jnp.sum(jnp.where(v1[0:1] + v2[8:_N_TOP] >= thr, 1.0, 0.0), axis=0, keepdims=True)
        cnt = cnt + jnp.where(s1 == v1[0:1], deep, 0.0)
        keys = slice(hh * PEER_KEYS, (hh + 1) * PEER_KEYS)
        half = slice(hh * PEER_KEYS // 2, (hh + 1) * PEER_KEYS // 2)
        e1_ref[keys, :] = _pack_twice(jnp.exp(s1 - v1[0:1]) / z)
        cnt_ref[keys, :] = _pack_twice(cnt)
        e2_ref[half, :] = _pack_rows(jnp.exp(s2 - v2[0:1]))
        rank_ref[half, :] = _pack_rows(rank2)


def peer_stage_a(x, sc, sh, wqT_bf16, keys_bf16):
    bn, t, d = x.shape
    n = bn * t
    tm = min(256, t)
    rows = sc.shape[1]
    per_row = rows == t and t > 1
    mod_block = (1, tm, d) if per_row else (1, 1, d)
    mod_map = (lambda b, i: (b, i, 0)) if per_row else (lambda b, i: (b, 0, 0))
    tpb = t // tm
    col = lambda r: pl.BlockSpec((r, tm), lambda b, i: (0, b * tpb + i))
    nk = PEER_HEADS * PEER_KEYS
    return pl.pallas_call(
        _peer_a_kernel,
        grid=(bn, tpb),
        in_specs=[pl.BlockSpec((1, tm, d), lambda b, i: (b, i, 0)),
                  pl.BlockSpec(mod_block, mod_map), pl.BlockSpec(mod_block, mod_map),
                  _full(wqT_bf16.shape), _full(keys_bf16.shape)],
        out_specs=[pl.BlockSpec((1, tm, d), lambda b, i: (b, i, 0)), col(nk), col(nk), col(nk // 2), col(nk // 2)],
        out_shape=[jax.ShapeDtypeStruct((bn, t, d), BF16),
                   jax.ShapeDtypeStruct((nk, n), jnp.uint32), jax.ShapeDtypeStruct((nk, n), jnp.uint32),
                   jax.ShapeDtypeStruct((nk // 2, n), jnp.uint32), jax.ShapeDtypeStruct((nk // 2, n), jnp.uint32)],
        compiler_params=_params("parallel", "parallel"),
        name="peer_scores",
    )(x, sc, sh, wqT_bf16, keys_bf16)


def _peer_b_kernel(alpha, te, h_ref, *refs):
    n_parts = PEER_B_PARTS
    u_refs = refs[:n_parts]
    (vT_ref, e1_ref, cnt_ref, e2_ref, rank_ref, x_ref, g2_ref, lg_ref, lb_ref,
     o_ref, acc_ref, w_ref) = refs[n_parts:n_parts + 12]
    a_refs = refs[n_parts + 12:]
    e = pl.program_id(2)
    n_tiles = pl.num_programs(2) - 1
    tm = w_ref.shape[2]
    slot = lax.rem(e, 2)

    @pl.when(e == 0)
    def _():
        acc_ref[...] = jnp.zeros(acc_ref.shape, F32)
        w_ref[1] = jnp.zeros(w_ref.shape[1:], BF16)

    def drain():
        acc_ref[...] += jnp.dot(vT_ref[0], w_ref[1 - slot], preferred_element_type=F32)

    groups = te // PEER_KEYS
    assert groups == 8
    part_groups = groups // len(a_refs)
    part_rows = part_groups * PEER_KEYS

    @pl.when(e < n_tiles)
    def _():
        row0 = [pl.multiple_of(hh * PEER_KEYS + e * groups, 8) for hh in range(PEER_HEADS)]
        for q, a_ref in enumerate(a_refs):
            a_ref[...] = _dot_nt(u_refs[q][...], h_ref[0])
            tl = min(128, tm)
            for c in range(tm // tl):
                lanes = slice(c * tl, (c + 1) * tl)
                e1_rows = [e1_ref[pl.ds(row0[hh], groups), lanes] for hh in range(PEER_HEADS)]
                cnt_rows = [cnt_ref[pl.ds(row0[hh], groups), lanes] for hh in range(PEER_HEADS)]
                n_sub = PEER_KEYS // BF16_ROWS
                for gl in range(part_groups):
                    gi = q * part_groups + gl
                    gates = [None] * n_sub
                    for hh in range(PEER_HEADS):
                        e1 = pltpu.bitcast(jnp.broadcast_to(e1_rows[hh][gi:gi + 1, :], (8, tl)), BF16)
                        cnt = pltpu.bitcast(jnp.broadcast_to(cnt_rows[hh][gi:gi + 1, :], (8, tl)), BF16)
                        for r in range(n_sub):
                            words = slice(hh * PEER_KEYS // 2 + r * 8, hh * PEER_KEYS // 2 + (r + 1) * 8)
                            rank = pltpu.bitcast(rank_ref[words, lanes], BF16)
                            e2 = pltpu.bitcast(e2_ref[words, lanes], BF16)
                            part = jnp.where(rank < cnt, e2 * e1, jnp.zeros((), BF16))
                            gates[r] = part if gates[r] is None else gates[r] + part
                    for r in range(n_sub):
                        rows = slice(gl * PEER_KEYS + r * BF16_ROWS, gl * PEER_KEYS + (r + 1) * BF16_ROWS)
                        out_rows = slice(gi * PEER_KEYS + r * BF16_ROWS, gi * PEER_KEYS + (r + 1) * BF16_ROWS)
                        w_ref[slot, out_rows, lanes] = gates[r] * _gelu(a_ref[rows, lanes].astype(BF16))
        drain()

    @pl.when(e == n_tiles)
    def _():
        drain()
        z = alpha * x_ref[0] + (1.0 + g2_ref[0]) * acc_ref[...].T
        o_ref[0] = _ln_rows(z, lg_ref[...], lb_ref[...])


def peer_stage_b(h_bf16, u_bf16, vT_bf16, e1, cnt, e2, rank2, x, g2, ln_g, ln_b, alpha):
    bn, t, d = x.shape
    tm = min(512, t)
    te = PEER_TILE
    n_tiles = u_bf16.shape[0] // te
    rows = g2.shape[1]
    per_row = rows == t and t > 1
    mod_block = (1, tm, d) if per_row else (1, 1, d)
    mod_map = (lambda b, i, e: (b, i, 0)) if per_row else (lambda b, i, e: (b, 0, 0))
    tpb = t // tm
    tok = lambda: pl.BlockSpec((1, tm, d), lambda b, i, e: (b, i, 0))
    col = lambda r: pl.BlockSpec((r, tm), lambda b, i, e: (0, b * tpb + i))
    nk = PEER_HEADS * PEER_KEYS
    n_parts = PEER_B_PARTS
    u_specs = [pl.BlockSpec((te // n_parts, d),
                            lambda b, i, e, q=q: (jnp.minimum(e, n_tiles - 1) * n_parts + q, 0))
               for q in range(n_parts)]
    v_spec = pl.BlockSpec((1, d, te), lambda b, i, e: (jnp.maximum(e - 1, 0), 0, 0))
    return pl.pallas_call(
        functools.partial(_peer_b_kernel, alpha, te),
        grid=(bn, tpb, n_tiles + 1),
        in_specs=[tok()] + u_specs + [
                  v_spec,
                  col(nk), col(nk), col(nk // 2), col(nk // 2),
                  tok(), pl.BlockSpec(mod_block, mod_map), _full((1, d)), _full((1, d))],
        out_specs=tok(),
        out_shape=jax.ShapeDtypeStruct((bn, t, d), F32),
        scratch_shapes=[pltpu.VMEM((d, tm), F32), pltpu.VMEM((2, te, tm), BF16)]
                       + [pltpu.VMEM((te // PEER_B_PARTS, tm), F32) for _ in range(PEER_B_PARTS)],
        compiler_params=_params("parallel", "parallel", "arbitrary"),
        name="peer_experts",
    )(h_bf16, *([u_bf16] * n_parts), vT_bf16, e1, cnt, e2, rank2, x, g2,
      ln_g.reshape(1, d), ln_b.reshape(1, d))


def _pad_time(a, tp):
    return a if a.shape[1] == tp else jnp.pad(a, ((0, 0), (0, tp - a.shape[1]), (0, 0)))


def _forward_group(x, mods, rw_s0, shift0, conv0, attend, W, Wc, per_row, rw_bk):
    bn, t, d = x.shape
    depth = W['w_mix'].shape[0]
    alpha = (2.0 * depth) ** 0.25
    tp = -(-t // TIME_CHUNK) * TIME_CHUNK
    x = layer_norm_rows(x.reshape(bn * t, d), W['ln_in_g'], W['ln_in_b']).reshape(bn, t, d)
    if per_row:
        x = x.reshape(1, bn * t, d)
    ks, vs, rws, shs, cvs, cms = [], [], [], [], [], []
    for l in range(depth):
        wl = {name: W[name][l] for name in W if name not in ('ln_in_g', 'ln_in_b')}
        m = mods[l]
        if per_row:
            mv = [jnp.repeat(m[:, i], t, axis=0)[None] for i in range(6)]
        else:
            mv = [m[:, i][:, None, :] for i in range(6)]
        sh1, sc1, g1, sh2, sc2, g2 = mv
        p_rw, q, k, v, p_cv, p_cm = mix_projection(x, sc1, sh1, Wc['w_mix'][l])
        unflat = lambda a: a.reshape(bn, t, a.shape[-1])
        p_rw, q, k, v, p_cv, p_cm = map(unflat, (p_rw, q, k, v, p_cv, p_cm))
        y_rw, s_rw = rwkv_mixer(_pad_time(p_rw, tp), shift0[l], rw_s0[l], min(t, TIME_CHUNK) if tp == TIME_CHUNK
                                else TIME_CHUNK, rw_bk, wl)
        y_at = attend(l, q, k, v)
        y_cv, y_cm, v_rows, cv_st = local_mixers(_pad_time(p_cv, tp), _pad_time(p_cm, tp), conv0[l], t, wl)
        flat = (lambda a: a[:, :t].reshape(1, bn * t, -1)) if per_row else (lambda a: a[:, :t])
        ys = [flat(y_rw), flat(y_at), flat(y_cv), flat(y_cm)]
        x = out_projection(ys, x, g1, Wc['w_out'][l], wl['ln1_g'], wl['ln1_b'], alpha)
        h_bf, e1, cnt, e2, rank2 = peer_stage_a(x, sc2, sh2, Wc['peer_wqT'][l], Wc['peer_keys'][l])
        x = peer_stage_b(h_bf, Wc['peer_u'][l], Wc['peer_vT'][l], e1, cnt, e2, rank2, x, g2,
                         wl['ln2_g'], wl['ln2_b'], alpha)
        ks.append(k.reshape(bn, t, N_HEADS, HEAD_DIM))
        vs.append(v.reshape(bn, t, N_HEADS, HEAD_DIM))
        rws.append(s_rw)
        shs.append(p_rw[:, t - 1])
        cvs.append(cv_st)
        cms.append(v_rows[:, :t])
    return (x.reshape(bn, t, d), jnp.stack(ks), jnp.stack(vs), jnp.stack(rws), jnp.stack(shs),
            jnp.stack(cvs), jnp.stack(cms))


def kernel(x_prompt, x_sample, cache_k, cache_v, state_rwkv, state_shift, state_conv, page_table, c_prompt, c_sample, ln_in_g, ln_in_b, w_ada, b_ada, w_mix, rw_mu, rw_w0, rw_w2, rw_a0, rw_a2, rw_g2, rw_kk, rw_ka, rw_rk, rw_lnx_g, rw_lnx_b, conv_w, cm_ln_g, cm_ln_b, cm_ws, cm_bs, w_out, ln1_g, ln1_b, ln2_g, ln2_b, peer_wq, peer_keys, peer_u, peer_v):
    W = {'ln_in_g': ln_in_g, 'ln_in_b': ln_in_b, 'w_mix': w_mix,
         'rw_mu': rw_mu, 'rw_w0': rw_w0, 'rw_w2': rw_w2, 'rw_a0': rw_a0, 'rw_a2': rw_a2,
         'rw_g2': rw_g2, 'rw_kk': rw_kk, 'rw_ka': rw_ka, 'rw_rk': rw_rk, 'rw_lnx_g': rw_lnx_g,
         'rw_lnx_b': rw_lnx_b, 'conv_w': conv_w, 'cm_ln_g': cm_ln_g, 'cm_ln_b': cm_ln_b,
         'cm_ws': cm_ws, 'cm_bs': cm_bs, 'ln1_g': ln1_g, 'ln1_b': ln1_b,
         'ln2_g': ln2_g, 'ln2_b': ln2_b}
    depth = w_mix.shape[0]
    Wc = {'w_mix': w_mix.astype(BF16), 'w_out': w_out.astype(BF16),
          'peer_wqT': jnp.swapaxes(peer_wq, 1, 2).astype(BF16),
          'peer_keys': peer_keys.reshape(depth, PEER_HEADS * 2, PEER_KEYS, PEER_HALF).astype(BF16),
          'peer_u': peer_u.astype(BF16),
          'peer_vT': peer_v.reshape(depth, -1, PEER_TILE, D_MODEL).transpose(0, 1, 3, 2).astype(BF16)}
    bp, dbn = x_prompt.shape[0], x_sample.shape[0]
    n_c = bp + dbn
    n_cp = -(-n_c // 8) * 8
    c_all = jnp.pad(jnp.concatenate([c_prompt, c_sample], axis=0), ((0, n_cp - n_c), (0, 0)))
    mods = ada_vectors(c_all, w_ada, b_ada).reshape(depth, n_cp, 6, D_MODEL)
    mods_p, mods_s = mods[:, :bp], mods[:, bp:n_c]
    dt = x_prompt.dtype
    z_rw = jnp.zeros((depth, bp, N_HEADS, HEAD_DIM, HEAD_DIM), dt)
    z_sh = jnp.zeros((depth, bp, RW_COLS), dt)
    z_cv = jnp.zeros((depth, bp, CONV_W - 1, GROUP_W), dt)
    pages = lambda c: c.transpose(0, 1, 3, 4, 2).reshape(c.shape[0], c.shape[1], GROUP_W, PAGE_SIZE)
    ck, cv = pages(cache_k), pages(cache_v)
    y_p, k_p, v_p, rw_p, sh_p, cv_p, _ = _forward_group(
        x_prompt, mods_p, z_rw, z_sh, z_cv, lambda l, q, k, v: moba_prompt(q, k, v), W, Wc,
        per_row=False, rw_bk=math.gcd(bp, 4))
    y_s, k_s, v_s, rw_s, sh_s, cv_s, cm_s = _forward_group(
        x_sample, mods_s, state_rwkv, state_shift, state_conv,
        lambda l, q, k, v: moba_sample(l, q, k, v, ck, cv, page_table), W, Wc,
        per_row=True, rw_bk=math.gcd(dbn, 4))
    return (y_p, y_s, k_p, v_p, k_s, v_s, rw_p, rw_s, sh_p, sh_s, cv_p, cv_s, cm_s)
```

```python
import functools
import math

import jax
import jax.numpy as jnp
from jax import lax
from jax.experimental import pallas as pl
from jax.experimental.pallas import tpu as pltpu

F32 = jnp.float32
BF16 = jnp.bfloat16
HI = lax.Precision.HIGHEST

D_MODEL = 1024
N_MIXERS = 4
GROUP_W = D_MODEL // N_MIXERS
HEAD_DIM = 64
N_HEADS = GROUP_W // HEAD_DIM
RW_W_RANK = 32
RW_A_RANK = 32
RW_G_RANK = 64
RW_COLS = 3 * GROUP_W + RW_W_RANK + RW_A_RANK + RW_G_RANK
MOBA_COLS = 3 * GROUP_W
CONV_COLS = 3 * GROUP_W
CMLP_COLS = 2 * GROUP_W
N_COLS = RW_COLS + MOBA_COLS + CONV_COLS + CMLP_COLS
MOBA_BLOCK = 256
MOBA_TOPK = 3
Q_BLOCK = 128
PAGE_SIZE = 128
CONV_W = 3
CHUNK = 128
PEER_KEYS = 128
PEER_EXPERTS = PEER_KEYS * PEER_KEYS
PEER_HEADS = 8
PEER_TOPK = 16
PEER_QDIM = 256
PEER_HALF = PEER_QDIM // 2
LN_EPS = 1e-5
GN_EPS = 64e-5
NEG_INF = float("-inf")
LOG2_E = 1.4426950408889634

VMEM_LIMIT = 48 * 1024 * 1024
TIME_CHUNK = 128
PAGE_UNROLL = 4
PEER_TILE = 8 * PEER_KEYS
BF16_ROWS = 16
PEER_B_PARTS = 4
RW_SUB = 64


def _params(*sem):
    return pltpu.CompilerParams(dimension_semantics=sem, vmem_limit_bytes=VMEM_LIMIT)


def _full(shape):
    n = len(shape)
    return pl.BlockSpec(shape, lambda *_: (0,) * n)


def _sigmoid(x):
    return 1.0 / (1.0 + jnp.exp(-x))


def _gelu(x):
    c = 0.7978845608028654
    return x * (0.5 + 0.5 * jnp.tanh(x * (c + (c * 0.044715) * (x * x))))


def _ln_rows(x, g, b):
    mu = jnp.mean(x, axis=-1, keepdims=True)
    d = x - mu
    var = jnp.mean(d * d, axis=-1, keepdims=True)
    return d * lax.rsqrt(var + LN_EPS) * g + b


def _dot_hi(a, b):
    return jnp.dot(a, b, precision=HI, preferred_element_type=F32)


def _dot_nt(a, b):
    return lax.dot_general(a, b, (((1,), (1,)), ((), ())), preferred_element_type=F32)


def _ln_kernel(x_ref, g_ref, b_ref, o_ref):
    o_ref[...] = _ln_rows(x_ref[...], g_ref[...], b_ref[...])


def layer_norm_rows(x2d, g, b):
    n, d = x2d.shape
    tm = min(512, n)
    return pl.pallas_call(
        _ln_kernel,
        grid=(n // tm,),
        in_specs=[pl.BlockSpec((tm, d), lambda i: (i, 0)), _full((1, d)), _full((1, d))],
        out_specs=pl.BlockSpec((tm, d), lambda i: (i, 0)),
        out_shape=jax.ShapeDtypeStruct((n, d), F32),
        compiler_params=_params("parallel"),
        name="ln_in",
    )(x2d, g.reshape(1, d), b.reshape(1, d))


def _ada_kernel(c_ref, w_ref, b_ref, o_ref):
    c = c_ref[...]
    s = (c * _sigmoid(c)).astype(BF16)
    o_ref[0] = jnp.dot(s, w_ref[0].astype(BF16), preferred_element_type=F32) + b_ref[0]


def ada_vectors(c_all, w_ada, b_ada):
    depth, d, n6 = w_ada.shape
    m = c_all.shape[0]
    tn = 1536
    return pl.pallas_call(
        _ada_kernel,
        grid=(depth, n6 // tn),
        in_specs=[_full((m, d)),
                  pl.BlockSpec((1, d, tn), lambda l, j: (l, 0, j)),
                  pl.BlockSpec((1, 1, tn), lambda l, j: (l, 0, j))],
        out_specs=pl.BlockSpec((1, m, tn), lambda l, j: (l, 0, j)),
        out_shape=jax.ShapeDtypeStruct((depth, m, n6), F32),
        compiler_params=_params("parallel", "parallel"),
        name="ada",
    )(c_all, w_ada, b_ada.reshape(depth, 1, n6))


_MIX_WIDTHS = (RW_COLS, GROUP_W, GROUP_W, GROUP_W, CONV_COLS, CMLP_COLS)


def _mix_kernel(x_ref, sc_ref, sh_ref, w_ref, *out_refs):
    h = (x_ref[0] * (1.0 + sc_ref[0]) + sh_ref[0]).astype(BF16)
    off = 0
    for ref, width in zip(out_refs, _MIX_WIDTHS):
        ref[0] = jnp.dot(h, w_ref[:, off:off + width], preferred_element_type=F32)
        off += width


def mix_projection(x, sc, sh, w_bf16):
    bn, t, d = x.shape
    tm = min(256, t)
    rows = sc.shape[1]
    per_row = rows == t and t > 1
    mod_block = (1, tm, d) if per_row else (1, 1, d)
    mod_map = (lambda b, i: (b, i, 0)) if per_row else (lambda b, i: (b, 0, 0))
    out_shape = [jax.ShapeDtypeStruct((bn, t, w), F32) for w in _MIX_WIDTHS]
    out_specs = [pl.BlockSpec((1, tm, w), lambda b, i: (b, i, 0)) for w in _MIX_WIDTHS]
    return pl.pallas_call(
        _mix_kernel,
        grid=(bn, t // tm),
        in_specs=[pl.BlockSpec((1, tm, d), lambda b, i: (b, i, 0)),
                  pl.BlockSpec(mod_block, mod_map),
                  pl.BlockSpec(mod_block, mod_map),
                  _full(w_bf16.shape)],
        out_specs=out_specs,
        out_shape=out_shape,
        compiler_params=_params("parallel", "parallel"),
        name="mix_proj",
    )(x, sc, sh, w_bf16)


def _softplus(x):
    return jnp.maximum(x, 0.0) + jnp.log(1.0 + jnp.exp(-jnp.abs(x)))


def _mm(a, b):
    return jnp.dot(a.astype(BF16), b.astype(BF16), preferred_element_type=F32)


def _dot_tn(a, b):
    return lax.dot_general(a, b, (((0,), (0,)), ((), ())), preferred_element_type=F32)


def _rwkv_subchunk(state, lw, kk, bb, k2, r, v):
    G = GROUP_W
    C = RW_SUB
    ri = lax.broadcasted_iota(jnp.int32, (C, C), 0)
    cj = lax.broadcasted_iota(jnp.int32, (C, C), 1)
    cum = _dot_hi(jnp.where(cj <= ri, 1.0, 0.0), lw)
    g_in = jnp.exp(cum)
    g_out = jnp.exp(-cum)
    at = -kk * jnp.exp(cum - lw)
    g_end = g_in[C - 1:C, :]
    lane_head = lax.broadcasted_iota(jnp.int32, (1, G), 1) // HEAD_DIM

    def stack(x):
        return jnp.concatenate([jnp.where(lane_head == h, x, 0.0) for h in range(N_HEADS)], axis=0)

    a_s, b_s, k_s, r_s, v_s = [stack(x) for x in (at, bb * g_out, k2 * g_out, r * g_in, v)]
    a_b, b_b, k_b, r_b, v_b = [x.astype(BF16) for x in (a_s, b_s, k_s, r_s, v_s)]
    s_b = state.astype(BF16)
    row = lax.broadcasted_iota(jnp.int32, (G, G), 0)
    col = lax.broadcasted_iota(jnp.int32, (G, G), 1)
    rt, ct = row % C, col % C
    strict = ct < rt
    incl = ct <= rt
    lmat = jnp.where(strict, _dot_nt(a_b, b_b), 0.0)
    a_ak = jnp.where(strict, _dot_nt(a_b, k_b), 0.0)
    a_rb = jnp.where(incl, _dot_nt(r_b, b_b), 0.0)
    a_rk = jnp.where(incl, _dot_nt(r_b, k_b), 0.0)
    rhs = _dot_nt(a_b, s_b) + _mm(a_ak, v_b)
    base = 8
    l1 = jnp.where(row // base == col // base, lmat, 0.0)
    l2 = _mm(l1, l1)
    n = l1 + l2 + _mm(l1, l2)
    l4 = _mm(l2, l2)
    n = n + l4 + _mm(n, l4)
    m = base
    while m < C:
        lower_left = jnp.logical_and(row // (2 * m) == col // (2 * m),
                                     jnp.logical_and((row // m) % 2 == 1, (col // m) % 2 == 0))
        lm = jnp.where(lower_left, lmat, 0.0)
        t1 = lm + _mm(n, lm)
        n = n + t1 + _mm(t1, n)
        m *= 2
    u_s = rhs + _mm(n, rhs)
    u_b = u_s.astype(BF16)
    y_s = _dot_nt(r_b, s_b) + _mm(a_rb, u_b) + _mm(a_rk, v_b)
    y = y_s[0:C] + y_s[C:2 * C] + y_s[2 * C:3 * C] + y_s[3 * C:4 * C]
    new_state = (state * g_end + _dot_tn(u_b, (b_s * g_end).astype(BF16))
                 + _dot_tn(v_b, (k_s * g_end).astype(BF16)))
    return y, new_state


def _rwkv_kernel(n_valid, bk,
                 p_ref, shift_ref, s0_ref, bd_ref, mu_ref, w0_ref, w2_ref, a0_ref, a2_ref, g2_ref,
                 kkp_ref, kap_ref, rk_ref, lng_ref, lnb_ref,
                 y_ref, s_ref, carry_ref):
    ci = pl.program_id(1)
    G = GROUP_W
    bd = bd_ref[...]

    @pl.when(ci == 0)
    def _():
        s_ref[...] = s0_ref[...]
        carry_ref[...] = shift_ref[...]

    row_id = lax.broadcasted_iota(jnp.int32, (TIME_CHUNK, 1), 0)
    sub_row = lax.broadcasted_iota(jnp.int32, (RW_SUB, 1), 0)

    for b in range(bk):
        p = p_ref[b]
        prev = jnp.where(row_id == 0, carry_ref[b], pltpu.roll(p, 1, axis=0))
        carry_ref[b] = p[TIME_CHUNK - 1:TIME_CHUNK, :]
        xm = p + (prev - p) * mu_ref[...]
        r, k, v = xm[:, :G], xm[:, G:2 * G], xm[:, 2 * G:3 * G]
        o = 3 * G
        wl = xm[:, o:o + RW_W_RANK]
        al = xm[:, o + RW_W_RANK:o + RW_W_RANK + RW_A_RANK]
        gl = xm[:, o + RW_W_RANK + RW_A_RANK:]
        w = -_softplus(-(w0_ref[...] + _dot_hi(jnp.tanh(wl), w2_ref[...]))) - 0.5
        lw = -jnp.exp(w)
        a = _sigmoid(a0_ref[...] + _dot_hi(al, a2_ref[...]))
        g = _dot_hi(_sigmoid(gl), g2_ref[...])
        kk = k * kkp_ref[...]
        kk = kk / jnp.maximum(jnp.sqrt(_dot_hi(kk * kk, bd)), 1e-12)
        k2 = k * (1.0 + (a - 1.0) * kap_ref[...])
        bb = kk * a
        bonus = _dot_hi(r * k2 * rk_ref[...], bd) * v

        state = s_ref[b]
        ys = []
        for j in range(TIME_CHUNK // RW_SUB):
            n_j = max(0, min(RW_SUB, n_valid - RW_SUB * j))
            rows = slice(j * RW_SUB, (j + 1) * RW_SUB)
            if n_j == 0:
                ys.append(jnp.zeros((RW_SUB, G), F32))
                continue
            lw_j, kk_j, bb_j, k2_j = lw[rows], kk[rows], bb[rows], k2[rows]
            if n_j < RW_SUB:
                live = sub_row < n_j
                lw_j, kk_j, bb_j, k2_j = [jnp.where(live, x, 0.0) for x in (lw_j, kk_j, bb_j, k2_j)]
            y_j, state = _rwkv_subchunk(state, lw_j, kk_j, bb_j, k2_j, r[rows], v[rows])
            ys.append(y_j)
        s_ref[b] = state
        y = jnp.concatenate(ys, axis=0)

        mean = _dot_hi(y, bd) * (1.0 / HEAD_DIM)
        d = y - mean
        var = _dot_hi(d * d, bd) * (1.0 / HEAD_DIM)
        yn = d * lax.rsqrt(var + GN_EPS) * lng_ref[...] + lnb_ref[...]
        y_ref[b] = (yn + bonus) * g


def rwkv_mixer(p, shift_prev, s0, n_valid, bk, wl):
    bn, tp, _ = p.shape
    G = GROUP_W
    eye_h = jnp.eye(N_HEADS, dtype=s0.dtype)
    s0bd = (s0[:, :, :, None, :] * eye_h[None, :, None, :, None]).reshape(bn, G, G)
    head_of = jnp.arange(G) // HEAD_DIM
    bd = (head_of[:, None] == head_of[None, :]).astype(F32)
    row = lambda a: a.reshape(1, -1)
    consts = [bd, row(wl['rw_mu']), row(wl['rw_w0']), wl['rw_w2'], row(wl['rw_a0']), wl['rw_a2'], wl['rw_g2'],
              row(wl['rw_kk']), row(wl['rw_ka']), row(wl['rw_rk']), row(wl['rw_lnx_g']), row(wl['rw_lnx_b'])]
    y, s_fin = pl.pallas_call(
        functools.partial(_rwkv_kernel, n_valid, bk),
        grid=(bn // bk, tp // TIME_CHUNK),
        in_specs=[pl.BlockSpec((bk, TIME_CHUNK, RW_COLS), lambda b, c: (b, c, 0)),
                  pl.BlockSpec((bk, 1, RW_COLS), lambda b, c: (b, 0, 0)),
                  pl.BlockSpec((bk, G, G), lambda b, c: (b, 0, 0))]
                 + [_full(a.shape) for a in consts],
        out_specs=[pl.BlockSpec((bk, TIME_CHUNK, G), lambda b, c: (b, c, 0)),
                   pl.BlockSpec((bk, G, G), lambda b, c: (b, 0, 0))],
        out_shape=[jax.ShapeDtypeStruct((bn, tp, G), F32),
                   jax.ShapeDtypeStruct((bn, G, G), F32)],
        scratch_shapes=[pltpu.VMEM((bk, 1, RW_COLS), F32)],
        compiler_params=_params("parallel", "arbitrary"),
        name="rwkv7",
    )(p, shift_prev.reshape(bn, 1, RW_COLS), s0bd, *consts)
    s4 = s_fin.reshape(bn, N_HEADS, HEAD_DIM, N_HEADS, HEAD_DIM)
    return y, jnp.stack([s4[:, h, :, h, :] for h in range(N_HEADS)], axis=1)


def _local_kernel(t_valid, cv_ref, cm_ref, prev_ref, cw_ref, lg_ref, lb_ref, ws_ref, bias_ref,
                  ycv_ref, ycm_ref, vrow_ref, st_ref, carry_ref):
    ci = pl.program_id(1)
    G = GROUP_W
    TC = TIME_CHUNK

    @pl.when(ci == 0)
    def _():
        carry_ref[...] = prev_ref[0]

    pc = cv_ref[0]
    bg, cg, hv = pc[:, :G], pc[:, G:2 * G], pc[:, 2 * G:]
    u = cg * hv
    row_id = lax.broadcasted_iota(jnp.int32, (TC, 1), 0)
    prev0, prev1 = carry_ref[0:1, :], carry_ref[1:2, :]
    u1 = jnp.where(row_id == 0, prev1, pltpu.roll(u, 1, axis=0))
    u2 = jnp.where(row_id == 0, prev0, jnp.where(row_id == 1, prev1, pltpu.roll(u, 2, axis=0)))
    ycv_ref[0] = bg * (cw_ref[0:1, :] * u2 + cw_ref[1:2, :] * u1 + cw_ref[2:3, :] * u)
    carry_ref[...] = u[TC - 2:, :]
    last = (t_valid - 1) // TC
    r = t_valid - last * TC

    @pl.when(ci == last)
    def _():
        st_ref[0] = u[r - 2:r, :]

    pm = cm_ref[0]
    uu = _gelu(pm[:, :G])
    vv = _ln_rows(_gelu(pm[:, G:]), lg_ref[...], lb_ref[...])
    vrow_ref[0] = vv
    head_of = lax.broadcasted_iota(jnp.int32, (1, G), 1) // HEAD_DIM
    causal = lax.broadcasted_iota(jnp.int32, (TC, TC), 0) >= lax.broadcasted_iota(jnp.int32, (TC, TC), 1)
    mixed = bias_ref[...]
    for h in range(N_HEADS):
        wm = jnp.where(causal, ws_ref[h], 0.0).astype(BF16)
        mixed = mixed + jnp.dot(wm, jnp.where(head_of == h, vv, 0.0).astype(BF16), preferred_element_type=F32)
    ycm_ref[0] = uu * mixed


def local_mixers(p_cv, p_cm, conv_prev, t_valid, wl):
    bn, tp, _ = p_cv.shape
    G = GROUP_W
    bias = jnp.repeat(wl['cm_bs'].T, HEAD_DIM, axis=1)
    row = lambda a: a.reshape(1, -1)
    tok = lambda w: pl.BlockSpec((1, TIME_CHUNK, w), lambda b, c: (b, c, 0))
    return pl.pallas_call(
        functools.partial(_local_kernel, t_valid),
        grid=(bn, tp // TIME_CHUNK),
        in_specs=[tok(CONV_COLS), tok(CMLP_COLS),
                  pl.BlockSpec((1, CONV_W - 1, G), lambda b, c: (b, 0, 0)),
                  _full((CONV_W, G)), _full((1, G)), _full((1, G)),
                  _full((N_HEADS, CHUNK, CHUNK)), _full((CHUNK, G))],
        out_specs=[tok(G), tok(G), tok(G), pl.BlockSpec((1, CONV_W - 1, G), lambda b, c: (b, 0, 0))],
        out_shape=[jax.ShapeDtypeStruct((bn, tp, G), F32)] * 3
                  + [jax.ShapeDtypeStruct((bn, CONV_W - 1, G), F32)],
        scratch_shapes=[pltpu.VMEM((CONV_W - 1, G), F32)],
        compiler_params=_params("parallel", "arbitrary"),
        name="conv_gmlp",
    )(p_cv, p_cm, conv_prev, wl['conv_w'], row(wl['cm_ln_g']), row(wl['cm_ln_b']), wl['cm_ws'], bias)


def _rank_rows(g, n_rows):
    rid = lax.broadcasted_iota(jnp.int32, g.shape, 0)
    rank = jnp.zeros(g.shape, jnp.int32)
    for j in range(n_rows):
        gj = g[j:j + 1, :]
        beats = jnp.where(gj > g, 1, jnp.where(jnp.logical_and(gj == g, rid > j), 1, 0))
        rank = rank + beats
    return rank


def _block_mean_kernel(k_ref, o_ref):
    for i in range(o_ref.shape[1]):
        o_ref[0, i] = jnp.mean(k_ref[0, i * MOBA_BLOCK:(i + 1) * MOBA_BLOCK], axis=0, keepdims=True)


def _moba_prompt_kernel(n_blk, n_sel, qT_ref, k_ref, vT_ref, kmean_ref, o_ref, sel_ref, s_ref):
    qi = pl.program_id(1)
    own = qi // (MOBA_BLOCK // Q_BLOCK)
    own_start = pl.multiple_of(own * MOBA_BLOCK, MOBA_BLOCK)
    kpos = own_start + lax.broadcasted_iota(jnp.int32, (MOBA_BLOCK, Q_BLOCK), 0)
    qpos = qi * Q_BLOCK + lax.broadcasted_iota(jnp.int32, (MOBA_BLOCK, Q_BLOCK), 1)
    causal = kpos <= qpos
    blk = lax.broadcasted_iota(jnp.int32, (n_blk, Q_BLOCK), 0)
    valid = blk < own

    def attend(h, start, qb, keep):
        s = jnp.dot(k_ref[0, h, pl.ds(start, MOBA_BLOCK), :], qb, preferred_element_type=F32)
        return jnp.where(keep, s, NEG_INF)

    def values(h, start, p):
        return jnp.dot(vT_ref[0, h, :, pl.ds(start, MOBA_BLOCK)], p.astype(BF16), preferred_element_type=F32)

    def fold8(s):
        return jnp.max(s.reshape(MOBA_BLOCK // 8, 8, Q_BLOCK), axis=0)

    qbs, peaks = [], []
    for h in range(N_HEADS):
        qT = qT_ref[0, h]
        gate = jnp.where(valid, _dot_hi(kmean_ref[0, h], qT), NEG_INF)
        sel = jnp.where(jnp.logical_and(valid, _rank_rows(gate, n_blk) < n_sel), 1.0, 0.0)
        for j in range(n_blk):
            sel_ref[h, j] = jnp.broadcast_to(sel[j:j + 1, :], (8, Q_BLOCK))
        qb = (qT * (HEAD_DIM ** -0.5 * LOG2_E)).astype(BF16)
        s = attend(h, own_start, qb, causal)
        s_ref[h, own] = s
        qbs.append(qb)
        peaks.append(fold8(s))

    def pass_a(i, peaks):
        j0 = 2 * i
        j1 = jnp.where(j0 + 1 < own, j0 + 1, j0)
        out = list(peaks)
        for j in (j0, j1):
            start = pl.multiple_of(j * MOBA_BLOCK, MOBA_BLOCK)
            for h in range(N_HEADS):
                s = attend(h, start, qbs[h], sel_ref[h, j][0:1, :] > 0.0)
                s_ref[h, j] = s
                out[h] = jnp.maximum(out[h], fold8(s))
        return tuple(out)

    peaks = lax.fori_loop(0, (own + 1) // 2, pass_a, tuple(peaks))
    ms = [jnp.max(pk, axis=0, keepdims=True) for pk in peaks]

    def pass_b(i, accs):
        j0 = 2 * i
        live = j0 + 1 <= own
        j1 = jnp.minimum(j0 + 1, own)
        out = list(accs)
        for j, keep in ((j0, None), (j1, live)):
            start = pl.multiple_of(j * MOBA_BLOCK, MOBA_BLOCK)
            for h in range(N_HEADS):
                peak = ms[h] if keep is None else jnp.where(keep, ms[h], jnp.inf)
                out[h] = out[h] + values(h, start, jnp.exp2(s_ref[h, j] - peak))
        return tuple(out)

    rows = vT_ref.shape[2]
    accs = lax.fori_loop(0, (own + 2) // 2, pass_b, tuple(jnp.zeros((rows, Q_BLOCK), F32) for _ in range(N_HEADS)))
    for h in range(N_HEADS):
        o_ref[0, h] = accs[h][:HEAD_DIM] / accs[h][HEAD_DIM:HEAD_DIM + 1]


def moba_prompt(q, k, v):
    bn, t, G = q.shape
    n_blk = t // MOBA_BLOCK
    n_sel = min(MOBA_TOPK, n_blk - 1)
    per_step = math.gcd(n_blk, 4)
    kmean = pl.pallas_call(
        _block_mean_kernel,
        grid=(bn, n_blk // per_step),
        in_specs=[pl.BlockSpec((1, per_step * MOBA_BLOCK, G), lambda b, j: (b, j, 0))],
        out_specs=pl.BlockSpec((1, per_step, 1, G), lambda b, j: (b, j, 0, 0)),
        out_shape=jax.ShapeDtypeStruct((bn, n_blk, 1, G), F32),
        compiler_params=_params("parallel", "parallel"),
        name="moba_block_means",
    )(k)
    heads = lambda a: a.reshape(bn, -1, N_HEADS, HEAD_DIM)
    kmean = heads(kmean).transpose(0, 2, 1, 3)
    qT = heads(q).transpose(0, 2, 3, 1)
    kh = heads(k).transpose(0, 2, 1, 3).astype(BF16)
    vT = heads(v).transpose(0, 2, 3, 1).astype(BF16)
    v_rows = HEAD_DIM + 16
    vT = jnp.concatenate([vT, jnp.ones((bn, N_HEADS, 1, t), BF16),
                          jnp.zeros((bn, N_HEADS, v_rows - HEAD_DIM - 1, t), BF16)], axis=2)
    hq = lambda rows, cols: pl.BlockSpec((1, N_HEADS, rows, cols), lambda b, i: (b, 0, 0, 0))
    tile = pl.BlockSpec((1, N_HEADS, HEAD_DIM, Q_BLOCK), lambda b, i: (b, 0, 0, i))
    oT = pl.pallas_call(
        functools.partial(_moba_prompt_kernel, n_blk, n_sel),
        grid=(bn, t // Q_BLOCK),
        in_specs=[tile, hq(t, HEAD_DIM), hq(v_rows, t), hq(n_blk, HEAD_DIM)],
        out_specs=tile,
        out_shape=jax.ShapeDtypeStruct((bn, N_HEADS, HEAD_DIM, t), F32),
        scratch_shapes=[pltpu.VMEM((N_HEADS, n_blk, 8, Q_BLOCK), F32),
                        pltpu.VMEM((N_HEADS, n_blk, MOBA_BLOCK, Q_BLOCK), F32)],
        compiler_params=_params("parallel", "arbitrary"),
        name="moba_prompt",
    )(qT, kh, vT, kmean)
    return oT.transpose(0, 3, 1, 2).reshape(bn, t, GROUP_W)


def _rank_cols(g, n_cols):
    cid = lax.broadcasted_iota(jnp.int32, g.shape, 1)
    rank = jnp.zeros(g.shape, jnp.int32)
    for j in range(n_cols):
        gj = g[:, j:j + 1]
        rank = rank + jnp.where(gj > g, 1, jnp.where(jnp.logical_and(gj == g, cid > j), 1, 0))
    return rank


def _moba_decode_kernel(layer, n_pages, n_sel, t_new, pt_ref, q_ref, kn_ref, vn_ref, ck_hbm, cv_hbm, o_ref,
                        kbuf, vbuf, s_ref, ksem, vsem):
    b = pl.program_id(0)
    nb = pl.num_programs(0)
    slot = lax.rem(b, 2)
    G = GROUP_W
    R = N_HEADS * t_new
    scale = HEAD_DIM ** -0.5
    pages_per_blk = MOBA_BLOCK // PAGE_SIZE
    n_pf = n_pages // pages_per_blk

    def k_copy(row, p, sl):
        return pltpu.make_async_copy(ck_hbm.at[layer, pt_ref[row, p]], kbuf.at[sl, p], ksem.at[sl])

    def v_copy(p):
        return pltpu.make_async_copy(cv_hbm.at[layer, pt_ref[b, p]], vbuf.at[p], vsem.at[0])

    def for_pages(fn):
        def body(p, c):
            fn(p)
            return c
        lax.fori_loop(0, n_pages, body, 0, unroll=PAGE_UNROLL)

    @pl.when(b == 0)
    def _():
        for_pages(lambda p: k_copy(0, p, 0).start())

    for_pages(lambda p: v_copy(p).start())

    @pl.when(b + 1 < nb)
    def _():
        for_pages(lambda p: k_copy(b + 1, p, 1 - slot).start())

    for_pages(lambda p: k_copy(b, p, slot).wait())

    q = q_ref[0]
    row_head = lax.broadcasted_iota(jnp.int32, (R, G), 0) // t_new
    lane_head = lax.broadcasted_iota(jnp.int32, (R, G), 1) // HEAD_DIM
    own_head = row_head == lane_head
    qbd = jnp.where(own_head, jnp.concatenate([q] * N_HEADS, axis=0), 0.0)
    qb = qbd.astype(BF16)
    blk_lane = lax.broadcasted_iota(jnp.int32, (1, n_pf), 1)

    def pass1(p, ksum_t):
        k_t = kbuf[slot, p]
        s_ref[p] = jnp.dot(qb, k_t.astype(BF16), preferred_element_type=F32) * scale
        return ksum_t + jnp.where(blk_lane == p // pages_per_blk, jnp.sum(k_t, axis=-1, keepdims=True), 0.0)

    ksum_t = lax.fori_loop(0, n_pages, pass1, jnp.zeros((G, n_pf), F32), unroll=PAGE_UNROLL)
    gate = _dot_hi(qbd, ksum_t * (1.0 / MOBA_BLOCK))
    sel = jnp.where(_rank_cols(gate, n_pf) < n_sel, 1.0, 0.0)

    s_own = _dot_nt(qb, kn_ref[0].astype(BF16)) * scale
    tq = lax.broadcasted_iota(jnp.int32, s_own.shape, 0) % t_new
    tk = lax.broadcasted_iota(jnp.int32, s_own.shape, 1)
    s_own = jnp.where(tk <= tq, s_own, NEG_INF)

    def pass2(p, m):
        chosen = jnp.sum(jnp.where(blk_lane == p // pages_per_blk, sel, 0.0), axis=-1, keepdims=True)
        s = jnp.where(chosen > 0.0, s_ref[p], NEG_INF)
        s_ref[p] = s
        return jnp.maximum(m, jnp.max(s, axis=-1, keepdims=True))

    m = lax.fori_loop(0, n_pages, pass2, jnp.max(s_own, axis=-1, keepdims=True), unroll=PAGE_UNROLL)

    for_pages(lambda p: v_copy(p).wait())

    def pass3(p, carry):
        l, acc = carry
        pr = jnp.exp(s_ref[p] - m)
        return (l + jnp.sum(pr, axis=-1, keepdims=True),
                acc + _dot_nt(pr.astype(BF16), vbuf[p].astype(BF16)))

    p_own = jnp.exp(s_own - m)
    l, acc = lax.fori_loop(
        0, n_pages, pass3,
        (jnp.sum(p_own, axis=-1, keepdims=True),
         jnp.dot(p_own.astype(BF16), vn_ref[0].astype(BF16), preferred_element_type=F32)),
        unroll=PAGE_UNROLL)
    o = jnp.where(own_head, acc / l, 0.0)
    out = o[0:t_new]
    for h in range(1, N_HEADS):
        out = out + o[h * t_new:(h + 1) * t_new]
    o_ref[0] = out


def moba_sample(layer, q, k, v, cache_k, cache_v, page_table):
    db, t_new, G = q.shape
    n_pages = page_table.shape[1]
    past = n_pages * PAGE_SIZE
    assert past % MOBA_BLOCK == 0, "decode kernel expects the past to end on a MoBA block boundary"
    n_sel = min(MOBA_TOPK, past // MOBA_BLOCK)
    R = N_HEADS * t_new
    tok = pl.BlockSpec((1, t_new, G), lambda b, pt: (b, 0, 0))
    pool = pl.BlockSpec(memory_space=pl.ANY)
    return pl.pallas_call(
        functools.partial(_moba_decode_kernel, layer, n_pages, n_sel, t_new),
        grid_spec=pltpu.PrefetchScalarGridSpec(
            num_scalar_prefetch=1, grid=(db,),
            in_specs=[tok, tok, tok, pool, pool],
            out_specs=tok,
            scratch_shapes=[pltpu.VMEM((2, n_pages, G, PAGE_SIZE), F32),
                            pltpu.VMEM((n_pages, G, PAGE_SIZE), F32),
                            pltpu.VMEM((n_pages, R, PAGE_SIZE), F32),
                            pltpu.SemaphoreType.DMA((2,)), pltpu.SemaphoreType.DMA((1,))]),
        out_shape=jax.ShapeDtypeStruct((db, t_new, G), F32),
        compiler_params=_params("arbitrary"),
        name="moba_decode",
    )(page_table, q, k, v, cache_k, cache_v)


def _out_kernel(alpha, yrw_ref, yat_ref, ycv_ref, ycm_ref, x_ref, g1_ref, w_ref, lg_ref, lb_ref, o_ref):
    G = GROUP_W
    acc = None
    for i, ref in enumerate((yrw_ref, yat_ref, ycv_ref, ycm_ref)):
        part = jnp.dot(ref[0].astype(BF16), w_ref[i * G:(i + 1) * G, :], preferred_element_type=F32)
        acc = part if acc is None else acc + part
    z = alpha * x_ref[0] + (1.0 + g1_ref[0]) * acc
    o_ref[0] = _ln_rows(z, lg_ref[...], lb_ref[...])


def out_projection(ys, x, g1, w_bf16, ln_g, ln_b, alpha):
    bn, t, d = x.shape
    tm = min(256, t)
    rows = g1.shape[1]
    per_row = rows == t and t > 1
    mod_block = (1, tm, d) if per_row else (1, 1, d)
    mod_map = (lambda b, i: (b, i, 0)) if per_row else (lambda b, i: (b, 0, 0))
    tok = lambda w: pl.BlockSpec((1, tm, w), lambda b, i: (b, i, 0))
    return pl.pallas_call(
        functools.partial(_out_kernel, alpha),
        grid=(bn, t // tm),
        in_specs=[tok(GROUP_W)] * 4 + [tok(d), pl.BlockSpec(mod_block, mod_map),
                                       _full(w_bf16.shape), _full((1, d)), _full((1, d))],
        out_specs=tok(d),
        out_shape=jax.ShapeDtypeStruct((bn, t, d), F32),
        compiler_params=_params("parallel", "parallel"),
        name="out_proj_ln",
    )(*ys, x, g1, w_bf16, ln_g.reshape(1, d), ln_b.reshape(1, d))


_N_TOP = PEER_TOPK + 1


_UNRANKED = 127.0


def _top_values(x, n, rows_out, with_rank=False):
    rid = lax.broadcasted_iota(jnp.int32, (rows_out, x.shape[1]), 0)
    vals = jnp.full((rows_out, x.shape[1]), NEG_INF, F32)
    rank = jnp.full(x.shape, _UNRANKED, F32) if with_rank else None
    for i in range(n):
        mx = jnp.max(x, axis=0, keepdims=True)
        vals = jnp.where(rid == i, mx, vals)
        hit = x == mx
        if with_rank:
            rank = jnp.where(hit, float(i), rank)
        x = jnp.where(hit, NEG_INF, x)
    return (vals, rank) if with_rank else vals


def _pack_rows(x):
    return pltpu.bitcast(x.astype(BF16), jnp.uint32)


def _pack_twice(x):
    bits = pltpu.bitcast(x.astype(BF16).astype(F32), jnp.uint32)
    return jnp.bitwise_or(bits, lax.shift_right_logical(bits, jnp.uint32(16)))


def _peer_a_kernel(x_ref, sc_ref, sh_ref, wqT_ref, keys_ref, h_ref, e1_ref, cnt_ref, e2_ref, rank_ref):
    h = (x_ref[0] * (1.0 + sc_ref[0]) + sh_ref[0]).astype(BF16)
    h_ref[0] = h
    qT = _dot_nt(wqT_ref[...], h).astype(BF16)
    for hh in range(PEER_HEADS):
        r0 = hh * PEER_QDIM
        s1 = jnp.dot(keys_ref[2 * hh], qT[r0:r0 + PEER_HALF], preferred_element_type=F32)
        s2 = jnp.dot(keys_ref[2 * hh + 1], qT[r0 + PEER_HALF:r0 + PEER_QDIM], preferred_element_type=F32)
        v1 = _top_values(s1, _N_TOP, 24)
        v2, rank2 = _top_values(s2, _N_TOP, 24, with_rank=True)
        cands = ([v1[0:1] + v2, v2[0:1] + v1] + [v1[a:a + 1] + v2[0:8] for a in (1, 2, 3)]
                 + [v1[4:8] + v2[1:2], v1[4:8] + v2[2:3]])
        best = _top_values(jnp.concatenate(cands, axis=0), _N_TOP, 24)
        z = jnp.sum(jnp.exp(best[0:PEER_TOPK] - best[0:1]), axis=0, keepdims=True)
        thr = 0.5 * (best[PEER_TOPK - 1:PEER_TOPK] + best[PEER_TOPK:PEER_TOPK + 1])
        cnt = jnp.zeros(s1.shape, F32)
        for b in range(8):
            cnt = cnt + jnp.where(s1 >= thr - v2[b:b + 1], 1.0, 0.0)
        deep = jnp.sum(jnp.where(v1[0:1] + v2[8:_N_TOP] >= thr, 1.0, 0.0), axis=0, keepdims=True)
        cnt = cnt + jnp.where(s1 == v1[0:1], deep, 0.0)
        keys = slice(hh * PEER_KEYS, (hh + 1) * PEER_KEYS)
        half = slice(hh * PEER_KEYS // 2, (hh + 1) * PEER_KEYS // 2)
        e1_ref[keys, :] = _pack_twice(jnp.exp(s1 - v1[0:1]) / z)
        cnt_ref[keys, :] = _pack_twice(cnt)
        e2_ref[half, :] = _pack_rows(jnp.exp(s2 - v2[0:1]))
        rank_ref[half, :] = _pack_rows(rank2)


def peer_stage_a(x, sc, sh, wqT_bf16, keys_bf16):
    bn, t, d = x.shape
    n = bn * t
    tm = min(256, t)
    rows = sc.shape[1]
    per_row = rows == t and t > 1
    mod_block = (1, tm, d) if per_row else (1, 1, d)
    mod_map = (lambda b, i: (b, i, 0)) if per_row else (lambda b, i: (b, 0, 0))
    tpb = t // tm
    col = lambda r: pl.BlockSpec((r, tm), lambda b, i: (0, b * tpb + i))
    nk = PEER_HEADS * PEER_KEYS
    return pl.pallas_call(
        _peer_a_kernel,
        grid=(bn, tpb),
        in_specs=[pl.BlockSpec((1, tm, d), lambda b, i: (b, i, 0)),
                  pl.BlockSpec(mod_block, mod_map), pl.BlockSpec(mod_block, mod_map),
                  _full(wqT_bf16.shape), _full(keys_bf16.shape)],
        out_specs=[pl.BlockSpec((1, tm, d), lambda b, i: (b, i, 0)), col(nk), col(nk), col(nk // 2), col(nk // 2)],
        out_shape=[jax.ShapeDtypeStruct((bn, t, d), BF16),
                   jax.ShapeDtypeStruct((nk, n), jnp.uint32), jax.ShapeDtypeStruct((nk, n), jnp.uint32),
                   jax.ShapeDtypeStruct((nk // 2, n), jnp.uint32), jax.ShapeDtypeStruct((nk // 2, n), jnp.uint32)],
        compiler_params=_params("parallel", "parallel"),
        name="peer_scores",
    )(x, sc, sh, wqT_bf16, keys_bf16)


def _peer_b_kernel(alpha, te, h_ref, *refs):
    n_parts = PEER_B_PARTS
    u_refs = refs[:n_parts]
    (vT_ref, e1_ref, cnt_ref, e2_ref, rank_ref, x_ref, g2_ref, lg_ref, lb_ref,
     o_ref, acc_ref, w_ref) = refs[n_parts:n_parts + 12]
    a_refs = refs[n_parts + 12:]
    e = pl.program_id(2)
    n_tiles = pl.num_programs(2) - 1
    tm = w_ref.shape[2]
    slot = lax.rem(e, 2)

    @pl.when(e == 0)
    def _():
        acc_ref[...] = jnp.zeros(acc_ref.shape, F32)
        w_ref[1] = jnp.zeros(w_ref.shape[1:], BF16)

    def drain():
        acc_ref[...] += jnp.dot(vT_ref[0], w_ref[1 - slot], preferred_element_type=F32)

    groups = te // PEER_KEYS
    assert groups == 8
    part_groups = groups // len(a_refs)
    part_rows = part_groups * PEER_KEYS

    @pl.when(e < n_tiles)
    def _():
        row0 = [pl.multiple_of(hh * PEER_KEYS + e * groups, 8) for hh in range(PEER_HEADS)]
        for q, a_ref in enumerate(a_refs):
            a_ref[...] = _dot_nt(u_refs[q][...], h_ref[0])
            tl = min(128, tm)
            for c in range(tm // tl):
                lanes = slice(c * tl, (c + 1) * tl)
                e1_rows = [e1_ref[pl.ds(row0[hh], groups), lanes] for hh in range(PEER_HEADS)]
                cnt_rows = [cnt_ref[pl.ds(row0[hh], groups), lanes] for hh in range(PEER_HEADS)]
                n_sub = PEER_KEYS // BF16_ROWS
                for gl in range(part_groups):
                    gi = q * part_groups + gl
                    gates = [None] * n_sub
                    for hh in range(PEER_HEADS):
                        e1 = pltpu.bitcast(jnp.broadcast_to(e1_rows[hh][gi:gi + 1, :], (8, tl)), BF16)
                        cnt = pltpu.bitcast(jnp.broadcast_to(cnt_rows[hh][gi:gi + 1, :], (8, tl)), BF16)
                        for r in range(n_sub):
                            words = slice(hh * PEER_KEYS // 2 + r * 8, hh * PEER_KEYS // 2 + (r + 1) * 8)
                            rank = pltpu.bitcast(rank_ref[words, lanes], BF16)
                            e2 = pltpu.bitcast(e2_ref[words, lanes], BF16)
                            part = jnp.where(rank < cnt, e2 * e1, jnp.zeros((), BF16))
                            gates[r] = part if gates[r] is None else gates[r] + part
                    for r in range(n_sub):
                        rows = slice(gl * PEER_KEYS + r * BF16_ROWS, gl * PEER_KEYS + (r + 1) * BF16_ROWS)
                        out_rows = slice(gi * PEER_KEYS + r * BF16_ROWS, gi * PEER_KEYS + (r + 1) * BF16_ROWS)
                        w_ref[slot, out_rows, lanes] = gates[r] * _gelu(a_ref[rows, lanes].astype(BF16))
        drain()

    @pl.when(e == n_tiles)
    def _():
        drain()
        z = alpha * x_ref[0] + (1.0 + g2_ref[0]) * acc_ref[...].T
        o_ref[0] = _ln_rows(z, lg_ref[...], lb_ref[...])


def peer_stage_b(h_bf16, u_bf16, vT_bf16, e1, cnt, e2, rank2, x, g2, ln_g, ln_b, alpha):
    bn, t, d = x.shape
    tm = min(512, t)
    te = PEER_TILE
    n_tiles = u_bf16.shape[0] // te
    rows = g2.shape[1]
    per_row = rows == t and t > 1
    mod_block = (1, tm, d) if per_row else (1, 1, d)
    mod_map = (lambda b, i, e: (b, i, 0)) if per_row else (lambda b, i, e: (b, 0, 0))
    tpb = t // tm
    tok = lambda: pl.BlockSpec((1, tm, d), lambda b, i, e: (b, i, 0))
    col = lambda r: pl.BlockSpec((r, tm), lambda b, i, e: (0, b * tpb + i))
    nk = PEER_HEADS * PEER_KEYS
    n_parts = PEER_B_PARTS
    u_specs = [pl.BlockSpec((te // n_parts, d),
                            lambda b, i, e, q=q: (jnp.minimum(e, n_tiles - 1) * n_parts + q, 0))
               for q in range(n_parts)]
    v_spec = pl.BlockSpec((1, d, te), lambda b, i, e: (jnp.maximum(e - 1, 0), 0, 0))
    return pl.pallas_call(
        functools.partial(_peer_b_kernel, alpha, te),
        grid=(bn, tpb, n_tiles + 1),
        in_specs=[tok()] + u_specs + [
                  v_spec,
                  col(nk), col(nk), col(nk // 2), col(nk // 2),
                  tok(), pl.BlockSpec(mod_block, mod_map), _full((1, d)), _full((1, d))],
        out_specs=tok(),
        out_shape=jax.ShapeDtypeStruct((bn, t, d), F32),
        scratch_shapes=[pltpu.VMEM((d, tm), F32), pltpu.VMEM((2, te, tm), BF16)]
                       + [pltpu.VMEM((te // PEER_B_PARTS, tm), F32) for _ in range(PEER_B_PARTS)],
        compiler_params=_params("parallel", "parallel", "arbitrary"),
        name="peer_experts",
    )(h_bf16, *([u_bf16] * n_parts), vT_bf16, e1, cnt, e2, rank2, x, g2,
      ln_g.reshape(1, d), ln_b.reshape(1, d))


def _pad_time(a, tp):
    return a if a.shape[1] == tp else jnp.pad(a, ((0, 0), (0, tp - a.shape[1]), (0, 0)))


def _forward_group(x, mods, rw_s0, shift0, conv0, attend, W, Wc, per_row, rw_bk):
    bn, t, d = x.shape
    depth = W['w_mix'].shape[0]
    alpha = (2.0 * depth) ** 0.25
    tp = -(-t // TIME_CHUNK) * TIME_CHUNK
    x = layer_norm_rows(x.reshape(bn * t, d), W['ln_in_g'], W['ln_in_b']).reshape(bn, t, d)
    if per_row:
        x = x.reshape(1, bn * t, d)
    ks, vs, rws, shs, cvs, cms = [], [], [], [], [], []
    for l in range(depth):
        wl = {name: W[name][l] for name in W if name not in ('ln_in_g', 'ln_in_b')}
        m = mods[l]
        if per_row:
            mv = [jnp.repeat(m[:, i], t, axis=0)[None] for i in range(6)]
        else:
            mv = [m[:, i][:, None, :] for i in range(6)]
        sh1, sc1, g1, sh2, sc2, g2 = mv
        p_rw, q, k, v, p_cv, p_cm = mix_projection(x, sc1, sh1, Wc['w_mix'][l])
        unflat = lambda a: a.reshape(bn, t, a.shape[-1])
        p_rw, q, k, v, p_cv, p_cm = map(unflat, (p_rw, q, k, v, p_cv, p_cm))
        y_rw, s_rw = rwkv_mixer(_pad_time(p_rw, tp), shift0[l], rw_s0[l], min(t, TIME_CHUNK) if tp == TIME_CHUNK
                                else TIME_CHUNK, rw_bk, wl)
        y_at = attend(l, q, k, v)
        y_cv, y_cm, v_rows, cv_st = local_mixers(_pad_time(p_cv, tp), _pad_time(p_cm, tp), conv0[l], t, wl)
        flat = (lambda a: a[:, :t].reshape(1, bn * t, -1)) if per_row else (lambda a: a[:, :t])
        ys = [flat(y_rw), flat(y_at), flat(y_cv), flat(y_cm)]
        x = out_projection(ys, x, g1, Wc['w_out'][l], wl['ln1_g'], wl['ln1_b'], alpha)
        h_bf, e1, cnt, e2, rank2 = peer_stage_a(x, sc2, sh2, Wc['peer_wqT'][l], Wc['peer_keys'][l])
        x = peer_stage_b(h_bf, Wc['peer_u'][l], Wc['peer_vT'][l], e1, cnt, e2, rank2, x, g2,
                         wl['ln2_g'], wl['ln2_b'], alpha)
        ks.append(k.reshape(bn, t, N_HEADS, HEAD_DIM))
        vs.append(v.reshape(bn, t, N_HEADS, HEAD_DIM))
        rws.append(s_rw)
        shs.append(p_rw[:, t - 1])
        cvs.append(cv_st)
        cms.append(v_rows[:, :t])
    return (x.reshape(bn, t, d), jnp.stack(ks), jnp.stack(vs), jnp.stack(rws), jnp.stack(shs),
            jnp.stack(cvs), jnp.stack(cms))


def kernel(x_prompt, x_sample, cache_k, cache_v, state_rwkv, state_shift, state_conv, page_table, c_prompt, c_sample, ln_in_g, ln_in_b, w_ada, b_ada, w_mix, rw_mu, rw_w0, rw_w2, rw_a0, rw_a2, rw_g2, rw_kk, rw_ka, rw_rk, rw_lnx_g, rw_lnx_b, conv_w, cm_ln_g, cm_ln_b, cm_ws, cm_bs, w_out, ln1_g, ln1_b, ln2_g, ln2_b, peer_wq, peer_keys, peer_u, peer_v):
    W = {'ln_in_g': ln_in_g, 'ln_in_b': ln_in_b, 'w_mix': w_mix,
         'rw_mu': rw_mu, 'rw_w0': rw_w0, 'rw_w2': rw_w2, 'rw_a0': rw_a0, 'rw_a2': rw_a2,
         'rw_g2': rw_g2, 'rw_kk': rw_kk, 'rw_ka': rw_ka, 'rw_rk': rw_rk, 'rw_lnx_g': rw_lnx_g,
         'rw_lnx_b': rw_lnx_b, 'conv_w': conv_w, 'cm_ln_g': cm_ln_g, 'cm_ln_b': cm_ln_b,
         'cm_ws': cm_ws, 'cm_bs': cm_bs, 'ln1_g': ln1_g, 'ln1_b': ln1_b,
         'ln2_g': ln2_g, 'ln2_b': ln2_b}
    depth = w_mix.shape[0]
    Wc = {'w_mix': w_mix.astype(BF16), 'w_out': w_out.astype(BF16),
          'peer_wqT': jnp.swapaxes(peer_wq, 1, 2).astype(BF16),
          'peer_keys': peer_keys.reshape(depth, PEER_HEADS * 2, PEER_KEYS, PEER_HALF).astype(BF16),
          'peer_u': peer_u.astype(BF16),
          'peer_vT': peer_v.reshape(depth, -1, PEER_TILE, D_MODEL).transpose(0, 1, 3, 2).astype(BF16)}
    bp, dbn = x_prompt.shape[0], x_sample.shape[0]
    n_c = bp + dbn
    n_cp = -(-n_c // 8) * 8
    c_all = jnp.pad(jnp.concatenate([c_prompt, c_sample], axis=0), ((0, n_cp - n_c), (0, 0)))
    mods = ada_vectors(c_all, w_ada, b_ada).reshape(depth, n_cp, 6, D_MODEL)
    mods_p, mods_s = mods[:, :bp], mods[:, bp:n_c]
    dt = x_prompt.dtype
    z_rw = jnp.zeros((depth, bp, N_HEADS, HEAD_DIM, HEAD_DIM), dt)
    z_sh = jnp.zeros((depth, bp, RW_COLS), dt)
    z_cv = jnp.zeros((depth, bp, CONV_W - 1, GROUP_W), dt)
    pages = lambda c: c.transpose(0, 1, 3, 4, 2).reshape(c.shape[0], c.shape[1], GROUP_W, PAGE_SIZE)
    ck, cv = pages(cache_k), pages(cache_v)
    y_p, k_p, v_p, rw_p, sh_p, cv_p, _ = _forward_group(
        x_prompt, mods_p, z_rw, z_sh, z_cv, lambda l, q, k, v: moba_prompt(q, k, v), W, Wc,
        per_row=False, rw_bk=math.gcd(bp, 4))
    y_s, k_s, v_s, rw_s, sh_s, cv_s, cm_s = _forward_group(
        x_sample, mods_s, state_rwkv, state_shift, state_conv,
        lambda l, q, k, v: moba_sample(l, q, k, v, ck, cv, page_table), W, Wc,
        per_row=True, rw_bk=math.gcd(dbn, 4))
    return (y_p, y_s, k_p, v_p, k_s, v_s, rw_p, rw_s, sh_p, sh_s, cv_p, cv_s, cm_s)
```
